```python
import math
import jax, jax.numpy as jnp
from jax import lax
import numpy as np

D_MODEL = 1024
BATCH = 8
SEQ = 8192
DEPTH = 2
DEC_BATCH = 32
DEC_SEQ = 2048
PAST_LEN = 128

HEAD_DIM = 64
A_Q_HEADS = 8
A_KV_HEADS = 2
A_GROUP = A_Q_HEADS // A_KV_HEADS
WINDOW = 128
BLOCK = 128
B_HEADS = 4
ATTN_HEADS = A_Q_HEADS + B_HEADS
NUM_BUCKETS = 32
MAX_DISTANCE = 128
C_WIDTH = 512
D_WIDTH = 512
SHORT_CONV = 3
CONF_CONV = 31
FFN_HIDDEN = ((8 * D_MODEL // 3 + 255) // 256) * 256
EPS = 1e-6

A_Q_W = A_Q_HEADS * HEAD_DIM
A_KV_W = A_KV_HEADS * HEAD_DIM
B_QK_W = B_HEADS * 2 * HEAD_DIM
B_V_W = B_HEADS * 2 * HEAD_DIM
ATTN_IN = A_Q_W + 2 * A_KV_W + 2 * B_QK_W + B_V_W
ATTN_OUT = A_Q_W + B_V_W
ATTN_SPLITS = [A_Q_W, A_Q_W + A_KV_W, A_Q_W + 2 * A_KV_W,
               A_Q_W + 2 * A_KV_W + B_QK_W, A_Q_W + 2 * A_KV_W + 2 * B_QK_W]
CONV_IN = 3 * C_WIDTH + 2 * D_WIDTH
CONV_OUT = C_WIDTH + D_WIDTH
N_EVEN = (DEPTH + 1) // 2
N_ODD = DEPTH // 2

kernel_name = "hybrid_bidir_encoder_attn_conv"


def rms_norm(x, g):
    xf = x.astype(jnp.float32)
    y = xf * lax.rsqrt(jnp.mean(xf * xf, axis=-1, keepdims=True) + EPS) * g.astype(jnp.float32)
    return y.astype(x.dtype)


def layer_norm(x, g, b):
    xf = x.astype(jnp.float32)
    mu = jnp.mean(xf, axis=-1, keepdims=True)
    xc = xf - mu
    var = jnp.mean(xc * xc, axis=-1, keepdims=True)
    y = xc * lax.rsqrt(var + EPS) * g.astype(jnp.float32) + b.astype(jnp.float32)
    return y.astype(x.dtype)


def rel_bucket(rel):
    half = NUM_BUCKETS // 2
    max_exact = half // 2
    n = jnp.abs(rel)
    large = max_exact + (jnp.log(jnp.maximum(n, 1).astype(jnp.float32) / max_exact)
                         / math.log(MAX_DISTANCE / max_exact) * (half - max_exact)).astype(jnp.int32)
    large = jnp.minimum(large, half - 1)
    return jnp.where(rel > 0, half, 0) + jnp.where(n < max_exact, n, large)


def depthwise_conv(x, w):
    width, ch = w.shape
    return lax.conv_general_dilated(
        x, w[:, None, :].astype(x.dtype), window_strides=(1,),
        padding=[(width // 2, width // 2)],
        dimension_numbers=('NWC', 'WIO', 'NWC'), feature_group_count=ch)


def windowed_gqa(q, k, v, bias_table, sink):
    bsz, s_len = q.shape[0], q.shape[1]
    nb = s_len // BLOCK
    qb = q.reshape(bsz, nb, BLOCK, A_KV_HEADS, A_GROUP, HEAD_DIM)

    def band(t):
        tp = jnp.pad(t, ((0, 0), (BLOCK, BLOCK), (0, 0), (0, 0)))
        tp = tp.reshape(bsz, nb + 2, BLOCK, A_KV_HEADS, HEAD_DIM)
        return jnp.concatenate([tp[:, :-2], tp[:, 1:-1], tp[:, 2:]], axis=2)

    kb, vb = band(k), band(v)
    rel = (jnp.arange(3 * BLOCK)[None, :] - BLOCK) - jnp.arange(BLOCK)[:, None]
    bias = bias_table[rel_bucket(rel)].astype(jnp.float32)
    bias = bias.transpose(2, 0, 1).reshape(A_KV_HEADS, A_GROUP, BLOCK, 3 * BLOCK)
    key_pos = jnp.arange(nb)[:, None] * BLOCK - BLOCK + jnp.arange(3 * BLOCK)[None, :]
    valid = (key_pos >= 0) & (key_pos < s_len)
    mask = (jnp.abs(rel) <= WINDOW)[None, :, :] & valid[:, None, :]
    s = jnp.einsum('bnqhgd,bnkhd->bnhgqk', qb, kb).astype(jnp.float32) * (HEAD_DIM ** -0.5) + bias
    s = jnp.where(mask[None, :, None, None], s, -1e30)
    sk = sink.astype(jnp.float32).reshape(A_KV_HEADS, A_GROUP)[None, None, :, :, None, None]
    m = jnp.maximum(jnp.max(s, axis=-1, keepdims=True), sk)
    e = jnp.exp(s - m)
    p = e / (jnp.sum(e, axis=-1, keepdims=True) + jnp.exp(sk - m))
    o = jnp.einsum('bnhgqk,bnkhd->bnqhgd', p.astype(v.dtype), vb)
    return o.reshape(bsz, s_len, A_Q_HEADS * HEAD_DIM)


def differential_attention(q, k, v, bias_table, lam, subln_g, layer):
    bsz, s_len = q.shape[0], q.shape[1]
    nb = s_len // BLOCK
    lam_init = 0.8 - 0.6 * math.exp(-0.3 * layer)
    lf = lam.astype(jnp.float32)
    lam_full = jnp.exp(jnp.sum(lf[0] * lf[1])) - jnp.exp(jnp.sum(lf[2] * lf[3])) + lam_init
    k_pos = jnp.arange(s_len)
    qblocks = q.reshape(bsz, nb, BLOCK, B_HEADS, 2, HEAD_DIM).transpose(1, 0, 2, 3, 4, 5)

    def block(args):
        qblk, n = args
        q_pos = n * BLOCK + jnp.arange(BLOCK)
        bias = bias_table[rel_bucket(k_pos[None, :] - q_pos[:, None])].astype(jnp.float32)
        bias = bias.transpose(2, 0, 1)[None, :, None]
        s = jnp.einsum('bqhcd,bkhcd->bhcqk', qblk, k).astype(jnp.float32) * (HEAD_DIM ** -0.5) + bias
        p = jax.nn.softmax(s, axis=-1)
        p = p[:, :, 0] - lam_full * p[:, :, 1]
        return jnp.einsum('bhqk,bkhe->bqhe', p.astype(v.dtype), v)

    o = lax.map(block, (qblocks, jnp.arange(nb)))
    o = o.transpose(1, 0, 2, 3, 4).reshape(bsz, s_len, B_HEADS, 2 * HEAD_DIM)
    o = rms_norm(o, subln_g) * (1.0 - lam_init)
    return o.reshape(bsz, s_len, B_HEADS * 2 * HEAD_DIM)


def attention_mixer(h, w_in, w_out, a_qn, a_kn, a_sink, b_qn, b_kn, b_lam, b_subln, rel_bias, layer):
    bsz, s_len, _ = h.shape
    proj = h @ w_in
    qa, ka, va, qb, kb, vb = jnp.split(proj, ATTN_SPLITS, axis=-1)
    qa = rms_norm(qa.reshape(bsz, s_len, A_Q_HEADS, HEAD_DIM), a_qn)
    ka = rms_norm(ka.reshape(bsz, s_len, A_KV_HEADS, HEAD_DIM), a_kn)
    va = va.reshape(bsz, s_len, A_KV_HEADS, HEAD_DIM)
    ya = windowed_gqa(qa, ka, va, rel_bias[:, :A_Q_HEADS], a_sink)
    qb = rms_norm(qb.reshape(bsz, s_len, B_HEADS, 2, HEAD_DIM), b_qn)
    kb = rms_norm(kb.reshape(bsz, s_len, B_HEADS, 2, HEAD_DIM), b_kn)
    vb = vb.reshape(bsz, s_len, B_HEADS, 2 * HEAD_DIM)
    yb = differential_attention(qb, kb, vb, rel_bias[:, A_Q_HEADS:], b_lam, b_subln, layer)
    return jnp.concatenate([ya, yb], axis=-1) @ w_out


def conv_mixer(h, w_in, w_out, sc_w, dw_w, dw_b, ln_g, ln_b):
    proj = h @ w_in
    gb, gc, xc, conf = jnp.split(proj, [C_WIDTH, 2 * C_WIDTH, 3 * C_WIDTH], axis=-1)
    yc = gb * depthwise_conv(gc * xc, sc_w)
    a, gate = jnp.split(conf, 2, axis=-1)
    u = a * jax.nn.sigmoid(gate)
    u = depthwise_conv(u, dw_w) + dw_b
    u = jax.nn.silu(layer_norm(u, ln_g, ln_b))
    return jnp.concatenate([yc, u], axis=-1) @ w_out


def swiglu(h, w_gate, w_up, w_down):
    return (jax.nn.silu(h @ w_gate) * (h @ w_up)) @ w_down


def encoder_trunk(x, rel_bias, mix_norm, ffn_norm, w_gate, w_up, w_down,
                  attn_w_in, attn_w_out, a_q_norm, a_k_norm, a_sink,
                  b_q_norm, b_k_norm, b_lambda, b_subln,
                  conv_w_in, conv_w_out, short_conv_w, conf_dw_w, conf_dw_b, conf_ln_g, conf_ln_b):
    for l in range(DEPTH):
        i = l // 2
        h = rms_norm(x, mix_norm[l])
        if l % 2 == 0:
            mix = attention_mixer(h, attn_w_in[i], attn_w_out[i], a_q_norm[i], a_k_norm[i], a_sink[i],
                                  b_q_norm[i], b_k_norm[i], b_lambda[i], b_subln[i], rel_bias, l)
        else:
            mix = conv_mixer(h, conv_w_in[i], conv_w_out[i], short_conv_w[i], conf_dw_w[i],
                             conf_dw_b[i], conf_ln_g[i], conf_ln_b[i])
        x = x + mix
        x = x + swiglu(rms_norm(x, ffn_norm[l]), w_gate[l], w_up[l], w_down[l])
    return x


def setup_inputs(seed: int = 0) -> dict:
    key = jax.random.key(seed)
    ks = jax.random.split(key, 24)
    f32 = jnp.float32
    nrm = lambda k, shape, scale: jax.random.normal(k, shape, f32) * scale
    gain = lambda k, shape: 1.0 + 0.05 * jax.random.normal(k, shape, f32)
    return {
        "x_prompt": nrm(ks[0], (BATCH, SEQ, D_MODEL), 1.0),
        "x_sample": nrm(ks[1], (DEC_BATCH, DEC_SEQ, D_MODEL), 1.0),
        "rel_bias": nrm(ks[2], (NUM_BUCKETS, ATTN_HEADS), 0.5),
        "mix_norm": gain(ks[3], (DEPTH, D_MODEL)),
        "ffn_norm": gain(ks[4], (DEPTH, D_MODEL)),
        "w_gate": nrm(ks[5], (DEPTH, D_MODEL, FFN_HIDDEN), D_MODEL ** -0.5),
        "w_up": nrm(ks[6], (DEPTH, D_MODEL, FFN_HIDDEN), D_MODEL ** -0.5),
        "w_down": nrm(ks[7], (DEPTH, FFN_HIDDEN, D_MODEL), FFN_HIDDEN ** -0.5),
        "attn_w_in": nrm(ks[8], (N_EVEN, D_MODEL, ATTN_IN), D_MODEL ** -0.5),
        "attn_w_out": nrm(ks[9], (N_EVEN, ATTN_OUT, D_MODEL), ATTN_OUT ** -0.5),
        "a_q_norm": gain(ks[10], (N_EVEN, HEAD_DIM)),
        "a_k_norm": gain(ks[11], (N_EVEN, HEAD_DIM)),
        "a_sink": nrm(ks[12], (N_EVEN, A_Q_HEADS), 0.5),
        "b_q_norm": gain(ks[13], (N_EVEN, HEAD_DIM)),
        "b_k_norm": gain(ks[14], (N_EVEN, HEAD_DIM)),
        "b_lambda": nrm(ks[15], (N_EVEN, 4, HEAD_DIM), 0.1),
        "b_subln": gain(ks[16], (N_EVEN, 2 * HEAD_DIM)),
        "conv_w_in": nrm(ks[17], (N_ODD, D_MODEL, CONV_IN), D_MODEL ** -0.5),
        "conv_w_out": nrm(ks[18], (N_ODD, CONV_OUT, D_MODEL), CONV_OUT ** -0.5),
        "short_conv_w": nrm(ks[19], (N_ODD, SHORT_CONV, C_WIDTH), SHORT_CONV ** -0.5),
        "conf_dw_w": nrm(ks[20], (N_ODD, CONF_CONV, D_WIDTH), CONF_CONV ** -0.5),
        "conf_dw_b": nrm(ks[21], (N_ODD, D_WIDTH), 0.02),
        "conf_ln_g": gain(ks[22], (N_ODD, D_WIDTH)),
        "conf_ln_b": nrm(ks[23], (N_ODD, D_WIDTH), 0.02),
    }


def reference(x_prompt, x_sample, rel_bias, mix_norm, ffn_norm, w_gate, w_up, w_down,
              attn_w_in, attn_w_out, a_q_norm, a_k_norm, a_sink,
              b_q_norm, b_k_norm, b_lambda, b_subln,
              conv_w_in, conv_w_out, short_conv_w, conf_dw_w, conf_dw_b, conf_ln_g, conf_ln_b):
    y_prompt = encoder_trunk(x_prompt, rel_bias, mix_norm, ffn_norm, w_gate, w_up, w_down,
                             attn_w_in, attn_w_out, a_q_norm, a_k_norm, a_sink,
                             b_q_norm, b_k_norm, b_lambda, b_subln,
                             conv_w_in, conv_w_out, short_conv_w, conf_dw_w, conf_dw_b, conf_ln_g, conf_ln_b)
    y_sample = encoder_trunk(x_sample, rel_bias, mix_norm, ffn_norm, w_gate, w_up, w_down,
                             attn_w_in, attn_w_out, a_q_norm, a_k_norm, a_sink,
                             b_q_norm, b_k_norm, b_lambda, b_subln,
                             conv_w_in, conv_w_out, short_conv_w, conf_dw_w, conf_dw_b, conf_ln_g, conf_ln_b)
    return (y_prompt, y_sample)
```

```python
import functools
import math

import jax
import jax.numpy as jnp
from jax import lax
from jax.experimental import pallas as pl
from jax.experimental.pallas import tpu as pltpu

D_MODEL = 1024
HEAD_DIM = 64
A_Q_HEADS = 8
A_KV_HEADS = 2
WINDOW = 128
BLOCK = 128
B_HEADS = 4
NUM_BUCKETS = 32
MAX_DISTANCE = 128
C_WIDTH = 512
D_WIDTH = 512
SHORT_CONV = 3
CONF_CONV = 31
FFN_HIDDEN = 2816
EPS = 1e-6
NEG = -1e30

LANES = 128
MXU_COLS = 256
VMEM_LIMIT_BYTES = 56 * 1024 * 1024

BF16 = jnp.bfloat16
F32 = jnp.float32

QA_OFF = 0
KA_OFF = 512
VA_OFF = 768
QB_OFF = 1024
KB_OFF = 1536
VB_OFF = 2048
PROJ0_W = 2560
NORM_CHUNKS0 = (True, True, True, False, True, True, True, True, False, False)

GB_OFF = 0
GX_OFF = 512
GLU_OFF = 1024
PROJ1_W = 1536
HALO = 16


def _cparams(sem):
    return pltpu.CompilerParams(dimension_semantics=sem, vmem_limit_bytes=VMEM_LIMIT_BYTES)


def _const_spec(shape):
    nd = len(shape)
    return pl.BlockSpec(shape, lambda *_: (0,) * nd)


def _rms(x, g):
    ms = jnp.mean(x * x, axis=-1, keepdims=True)
    return x * lax.rsqrt(ms + EPS) * g


def _attn_in_kernel(x_ref, g_ref, w_ref, gain_ref, seg_ref, o_ref):
    h = _rms(x_ref[...], g_ref[...]).astype(BF16)
    seg = seg_ref[...]
    for c, normed in enumerate(NORM_CHUNKS0):
        lo, hi = c * MXU_COLS, (c + 1) * MXU_COLS
        acc = jnp.dot(h, w_ref[:, lo:hi], preferred_element_type=F32)
        if normed:
            sq = acc * acc
            sq_hi = sq.astype(BF16)
            sq_lo = (sq - sq_hi.astype(F32)).astype(BF16)
            ms = (jnp.dot(sq_hi, seg, preferred_element_type=F32)
                  + jnp.dot(sq_lo, seg, preferred_element_type=F32))
            acc = acc * lax.rsqrt(ms + EPS) * gain_ref[:, lo:hi]
        o_ref[:, lo:hi] = acc.astype(BF16)


def _attn_in(x2d, g, w, gain, seg, tm):
    rows = x2d.shape[0]
    return pl.pallas_call(
        _attn_in_kernel,
        grid=(rows // tm,),
        in_specs=[
            pl.BlockSpec((tm, D_MODEL), lambda i: (i, 0)),
            _const_spec((1, D_MODEL)),
            _const_spec((D_MODEL, PROJ0_W)),
            _const_spec((1, PROJ0_W)),
            _const_spec((MXU_COLS, MXU_COLS)),
        ],
        out_specs=pl.BlockSpec((tm, PROJ0_W), lambda i: (i, 0)),
        out_shape=jax.ShapeDtypeStruct((rows, PROJ0_W), BF16),
        compiler_params=_cparams(("parallel",)),
        name="attn_in",
    )(x2d, g, w, gain, seg)


def _win_attn_kernel(sink_ref, q_ref, kp_ref, kc_ref, kn_ref, vp_ref, vc_ref, vn_ref, bias_ref, o_ref):
    n = pl.program_id(1)
    nb = pl.num_programs(1)
    col = lax.broadcasted_iota(jnp.int32, (1, 3 * BLOCK), 1)
    invalid = ((col < BLOCK) & (n == 0)) | ((col >= 2 * BLOCK) & (n == nb - 1))
    lane = lax.broadcasted_iota(jnp.int32, (1, LANES), 1)
    low = lane < HEAD_DIM
    row = lax.broadcasted_iota(jnp.int32, (2 * BLOCK, 1), 0)
    first = row < BLOCK
    for j in range(A_Q_HEADS // 2):
        kh = j // 2
        kcat = jnp.concatenate([kp_ref[0, :, kh * LANES:(kh + 1) * LANES],
                                kc_ref[0, :, kh * LANES:(kh + 1) * LANES],
                                kn_ref[0, :, kh * LANES:(kh + 1) * LANES]], axis=0)
        vcat = jnp.concatenate([vp_ref[0, :, kh * LANES:(kh + 1) * LANES],
                                vc_ref[0, :, kh * LANES:(kh + 1) * LANES],
                                vn_ref[0, :, kh * LANES:(kh + 1) * LANES]], axis=0)
        q = q_ref[0, :, j * LANES:(j + 1) * LANES]
        zero = jnp.zeros_like(q)
        q2 = jnp.concatenate([jnp.where(low, q, zero), jnp.where(low, zero, q)], axis=0)
        s = lax.dot_general(q2, kcat, (((1,), (1,)), ((), ())), preferred_element_type=F32)
        s = s + bias_ref[j]
        s = jnp.where(invalid, NEG, s)
        sk = jnp.where(first, sink_ref[2 * j], sink_ref[2 * j + 1])
        m = jnp.maximum(jnp.max(s, axis=-1, keepdims=True), sk)
        e = jnp.exp(s - m)
        den = jnp.sum(e, axis=-1, keepdims=True) + jnp.exp(sk - m)
        o = jnp.dot(e.astype(BF16), vcat, preferred_element_type=F32) / den
        o_ref[0, :, j * LANES:(j + 1) * LANES] = jnp.where(low, o[:BLOCK], o[BLOCK:]).astype(BF16)


def _win_attn(proj, sink, bias):
    bsz, s_len, _ = proj.shape
    nb = s_len // BLOCK
    qcol, kcol, vcol = QA_OFF // 512, KA_OFF // 256, VA_OFF // 256
    kv_specs = []
    for cb in (kcol, vcol):
        kv_specs += [
            pl.BlockSpec((1, BLOCK, 256), lambda b, n, cb=cb: (b, jnp.maximum(n - 1, 0), cb)),
            pl.BlockSpec((1, BLOCK, 256), lambda b, n, cb=cb: (b, n, cb)),
            pl.BlockSpec((1, BLOCK, 256), lambda b, n, cb=cb: (b, jnp.minimum(n + 1, nb - 1), cb)),
        ]
    return pl.pallas_call(
        _win_attn_kernel,
        grid=(bsz, nb),
        in_specs=[pl.BlockSpec(memory_space=pltpu.SMEM),
                  pl.BlockSpec((1, BLOCK, 512), lambda b, n: (b, n, qcol))] + kv_specs
                 + [_const_spec((A_Q_HEADS // 2, 2 * BLOCK, 3 * BLOCK))],
        out_specs=pl.BlockSpec((1, BLOCK, 512), lambda b, n: (b, n, 0)),
        out_shape=jax.ShapeDtypeStruct((bsz, s_len, 512), BF16),
        compiler_params=_cparams(("parallel", "parallel")),
        name="win_attn",
    )(sink, proj, proj, proj, proj, proj, proj, proj, bias)


def _diff_attn_kernel(far_ref, q_ref, k_ref, v_ref, bias_ref, lam_ref, g_ref, o_ref,
                      qz_ref, m_ref, l_ref, acc_ref, *, lam_init):
    h = pl.program_id(1)
    qi = pl.program_id(2)
    ki = pl.program_id(3)
    nk = pl.num_programs(3)

    @pl.when(ki == 0)
    def _init():
        q = q_ref[0]
        lane = lax.broadcasted_iota(jnp.int32, (1, LANES), 1)
        zero = jnp.zeros_like(q)
        qz_ref[0] = jnp.where(lane < HEAD_DIM, q, zero)
        qz_ref[1] = jnp.where(lane < HEAD_DIM, zero, q)
        m_ref[...] = jnp.full(m_ref.shape, NEG, F32)
        l_ref[...] = jnp.zeros(l_ref.shape, F32)
        acc_ref[...] = jnp.zeros(acc_ref.shape, F32)

    delta = ki - qi

    def step(c, bias_tile, bias_row):
        s_t = lax.dot_general(k_ref[0], qz_ref[c], (((1,), (1,)), ((), ())), preferred_element_type=F32)
        if bias_tile is not None:
            s_t = s_t + bias_tile
        m_old = m_ref[c]
        m_new = jnp.maximum(m_old, jnp.max(s_t, axis=0, keepdims=True) + bias_row)
        alpha = jnp.exp(m_old - m_new)
        p = jnp.exp(s_t - (m_new - bias_row))
        l_ref[c] = alpha * l_ref[c] + jnp.sum(p, axis=0, keepdims=True)
        pv = lax.dot_general(v_ref[0], p.astype(BF16), (((0,), (0,)), ((), ())), preferred_element_type=F32)
        acc_ref[c] = alpha * acc_ref[c] + pv
        m_ref[c] = m_new

    @pl.when(jnp.abs(delta) <= 1)
    def _near():
        tile = bias_ref[0, delta + 1]
        for c in range(2):
            step(c, tile, 0.0)

    @pl.when(jnp.abs(delta) > 1)
    def _far():
        side = jnp.where(delta < 0, far_ref[h, 0], far_ref[h, 1])
        for c in range(2):
            step(c, None, side)

    @pl.when(ki == nk - 1)
    def _finish():
        lam = lam_ref[...]
        lam_full = (jnp.exp(jnp.sum(lam[0:1] * lam[1:2], axis=-1, keepdims=True))
                    - jnp.exp(jnp.sum(lam[2:3] * lam[3:4], axis=-1, keepdims=True)) + lam_init)
        o = acc_ref[0] / l_ref[0] - lam_full * (acc_ref[1] / l_ref[1])
        ms = jnp.mean(o * o, axis=0, keepdims=True)
        y = o * lax.rsqrt(ms + EPS) * g_ref[...] * (1.0 - lam_init)
        o_ref[0] = y.T.astype(BF16)


def _diff_attn(proj, far, bias_t, lam, g_col, lam_init, tile):
    bsz, s_len, _ = proj.shape
    nt = s_len // tile
    qcol, kcol, vcol = QB_OFF // LANES, KB_OFF // LANES, VB_OFF // LANES
    return pl.pallas_call(
        functools.partial(_diff_attn_kernel, lam_init=lam_init),
        grid=(bsz, B_HEADS, nt, nt),
        in_specs=[
            pl.BlockSpec(memory_space=pltpu.SMEM),
            pl.BlockSpec((1, tile, LANES), lambda b, h, qi, ki: (b, qi, qcol + h)),
            pl.BlockSpec((1, tile, LANES), lambda b, h, qi, ki: (b, ki, kcol + h)),
            pl.BlockSpec((1, tile, LANES), lambda b, h, qi, ki: (b, ki, vcol + h)),
            pl.BlockSpec((1, 3, tile, tile), lambda b, h, qi, ki: (h, 0, 0, 0)),
            _const_spec((4, HEAD_DIM)),
            _const_spec((2 * HEAD_DIM, 1)),
        ],
        out_specs=pl.BlockSpec((1, tile, LANES), lambda b, h, qi, ki: (b, qi, h)),
        out_shape=jax.ShapeDtypeStruct((bsz, s_len, B_HEADS * 2 * HEAD_DIM), BF16),
        scratch_shapes=[
            pltpu.VMEM((2, tile, LANES), BF16),
            pltpu.VMEM((2, 1, tile), F32),
            pltpu.VMEM((2, 1, tile), F32),
            pltpu.VMEM((2, 2 * HEAD_DIM, tile), F32),
        ],
        compiler_params=_cparams(("parallel", "parallel", "parallel", "arbitrary")),
        name="diff_attn",
    )(far, proj, proj, proj, bias_t, lam, g_col)


FFN_CHUNKS = ((0, 1536), (1536, FFN_HIDDEN))


def _mix_ffn_kernel(x_ref, a_ref, b_ref, wa_ref, wb_ref, g_ref, wg_ref, wu_ref, wd_ref, o_ref):
    mix = (jnp.dot(a_ref[...], wa_ref[...], preferred_element_type=F32)
           + jnp.dot(b_ref[...], wb_ref[...], preferred_element_type=F32))
    x1 = x_ref[...] + mix
    h = _rms(x1, g_ref[...]).astype(BF16)
    down = None
    for lo, hi in FFN_CHUNKS:
        gate = jnp.dot(h, wg_ref[:, lo:hi], preferred_element_type=F32)
        up = jnp.dot(h, wu_ref[:, lo:hi], preferred_element_type=F32)
        act = (gate * jax.nn.sigmoid(gate) * up).astype(BF16)
        part = jnp.dot(act, wd_ref[lo:hi, :], preferred_element_type=F32)
        down = part if down is None else down + part
    o_ref[...] = x1 + down


def _mix_ffn(x2d, a2d, b2d, wa, wb, g, wg, wu, wd, tm):
    rows = x2d.shape[0]
    half = a2d.shape[1]
    row_spec = lambda w: pl.BlockSpec((tm, w), lambda i: (i, 0))
    single = lambda shape: pl.BlockSpec(memory_space=pltpu.VMEM)
    return pl.pallas_call(
        _mix_ffn_kernel,
        grid=(rows // tm,),
        in_specs=[
            row_spec(D_MODEL), row_spec(half), row_spec(half),
            single((half, D_MODEL)), single((half, D_MODEL)),
            single((1, D_MODEL)),
            single((D_MODEL, FFN_HIDDEN)), single((D_MODEL, FFN_HIDDEN)), single((FFN_HIDDEN, D_MODEL)),
        ],
        out_specs=row_spec(D_MODEL),
        out_shape=jax.ShapeDtypeStruct((rows, D_MODEL), F32),
        compiler_params=_cparams(("parallel",)),
        name="mix_ffn",
    )(x2d, a2d, b2d, wa, wb, g, wg, wu, wd)


def _conv_in_kernel(x_ref, g_ref, w_ref, o_ref):
    h = _rms(x_ref[...], g_ref[...]).astype(BF16)
    c = C_WIDTH
    dot = lambda lo, hi: jnp.dot(h, w_ref[:, lo:hi], preferred_element_type=F32)
    o_ref[:, GB_OFF:GB_OFF + c] = dot(0, c).astype(BF16)
    o_ref[:, GX_OFF:GX_OFF + c] = (dot(c, 2 * c) * dot(2 * c, 3 * c)).astype(BF16)
    a = dot(3 * c, 3 * c + D_WIDTH)
    gate = dot(3 * c + D_WIDTH, 3 * c + 2 * D_WIDTH)
    o_ref[:, GLU_OFF:GLU_OFF + D_WIDTH] = (a * jax.nn.sigmoid(gate)).astype(BF16)


def _conv_in(x2d, g, w, tm):
    rows = x2d.shape[0]
    conv_in_w = w.shape[1]
    return pl.pallas_call(
        _conv_in_kernel,
        grid=(rows // tm,),
        in_specs=[
            pl.BlockSpec((tm, D_MODEL), lambda i: (i, 0)),
            _const_spec((1, D_MODEL)),
            _const_spec((D_MODEL, conv_in_w)),
        ],
        out_specs=pl.BlockSpec((tm, PROJ1_W), lambda i: (i, 0)),
        out_shape=jax.ShapeDtypeStruct((rows, PROJ1_W), BF16),
        compiler_params=_cparams(("parallel",)),
        name="conv_in",
    )(x2d, g, w)


def _conv_mix_kernel(gb_ref, gx_ref, gxp_ref, gxn_ref, u_ref, up_ref, un_ref,
                     scw_ref, dww_ref, dwb_ref, lng_ref, lnb_ref, yc_ref, yu_ref, pad_ref, *, ts):
    t = pl.program_id(1)
    nt = pl.num_programs(1)
    has_prev = (t > 0).astype(F32)
    has_next = (t < nt - 1).astype(F32)

    def fill(cur, prev, nxt):
        pad_ref[0:HALO, :] = prev[0].astype(F32) * has_prev
        pad_ref[HALO:HALO + ts, :] = cur[0].astype(F32)
        pad_ref[HALO + ts:2 * HALO + ts, :] = nxt[0].astype(F32) * has_next

    def conv(w_ref, width):
        off = HALO - width // 2
        acc = pad_ref[off:off + ts, :] * w_ref[0:1, :]
        for j in range(1, width):
            acc = acc + pad_ref[off + j:off + j + ts, :] * w_ref[j:j + 1, :]
        return acc

    fill(gx_ref, gxp_ref, gxn_ref)
    yc_ref[0] = (gb_ref[0].astype(F32) * conv(scw_ref, SHORT_CONV)).astype(BF16)

    fill(u_ref, up_ref, un_ref)
    u = conv(dww_ref, CONF_CONV) + dwb_ref[...]
    mu = jnp.mean(u, axis=-1, keepdims=True)
    uc = u - mu
    var = jnp.mean(uc * uc, axis=-1, keepdims=True)
    y = uc * lax.rsqrt(var + EPS) * lng_ref[...] + lnb_ref[...]
    yu_ref[0] = (y * jax.nn.sigmoid(y)).astype(BF16)


def _conv_mix(proj, scw, dww, dwb, lng, lnb, ts):
    bsz, s_len, _ = proj.shape
    nt = s_len // ts
    r = ts // HALO
    nh = s_len // HALO
    cur = lambda cb: pl.BlockSpec((1, ts, 512), lambda b, t: (b, t, cb))
    prev = lambda cb: pl.BlockSpec((1, HALO, 512), lambda b, t: (b, jnp.maximum(t * r - 1, 0), cb))
    nxt = lambda cb: pl.BlockSpec((1, HALO, 512), lambda b, t: (b, jnp.minimum((t + 1) * r, nh - 1), cb))
    gb, gx, gl = GB_OFF // 512, GX_OFF // 512, GLU_OFF // 512
    out_spec = pl.BlockSpec((1, ts, 512), lambda b, t: (b, t, 0))
    return pl.pallas_call(
        functools.partial(_conv_mix_kernel, ts=ts),
        grid=(bsz, nt),
        in_specs=[cur(gb), cur(gx), prev(gx), nxt(gx), cur(gl), prev(gl), nxt(gl),
                  _const_spec((SHORT_CONV, C_WIDTH)), _const_spec((CONF_CONV, D_WIDTH)),
                  _const_spec((1, D_WIDTH)), _const_spec((1, D_WIDTH)), _const_spec((1, D_WIDTH))],
        out_specs=[out_spec, out_spec],
        out_shape=[jax.ShapeDtypeStruct((bsz, s_len, C_WIDTH), BF16),
                   jax.ShapeDtypeStruct((bsz, s_len, D_WIDTH), BF16)],
        scratch_shapes=[pltpu.VMEM((ts + 2 * HALO, 512), F32)],
        compiler_params=_cparams(("parallel", "parallel")),
        name="conv_mix",
    )(proj, proj, proj, proj, proj, proj, proj, scw, dww, dwb, lng, lnb)


def _rel_bucket(rel):
    half = NUM_BUCKETS // 2
    max_exact = half // 2
    n = jnp.abs(rel)
    large = max_exact + (jnp.log(jnp.maximum(n, 1).astype(F32) / max_exact)
                         / math.log(MAX_DISTANCE / max_exact) * (half - max_exact)).astype(jnp.int32)
    large = jnp.minimum(large, half - 1)
    return jnp.where(rel > 0, half, 0) + jnp.where(n < max_exact, n, large)


def _win_bias(rel_bias):
    rel = (jnp.arange(3 * BLOCK)[None, :] - BLOCK) - jnp.arange(BLOCK)[:, None]
    bias = rel_bias[:, :A_Q_HEADS][_rel_bucket(rel)].astype(F32).transpose(2, 0, 1)
    bias = jnp.where((jnp.abs(rel) <= WINDOW)[None], bias, NEG)
    return bias.reshape(A_Q_HEADS // 2, 2 * BLOCK, 3 * BLOCK)


def _diff_bias(rel_bias, tile):
    table = rel_bias[:, A_Q_HEADS:]
    kk = jnp.arange(tile)[:, None]
    qq = jnp.arange(tile)[None, :]
    near = jnp.stack([table[_rel_bucket(kk - qq + d * tile)] for d in (-1, 0, 1)], axis=0)
    far = table[_rel_bucket(jnp.array([-2 * tile, 2 * tile]))]
    return near.transpose(3, 0, 1, 2).astype(F32), far.T.astype(F32)


def _attn_in_params(w_in, a_qn, a_kn, b_qn, b_kn):
    d = HEAD_DIM
    idx = jnp.arange
    k0, v0 = A_Q_HEADS * d, A_Q_HEADS * d + A_KV_HEADS * d
    cols = jnp.concatenate([
        idx(k0),
        k0 + idx(d), k0 + idx(d), k0 + d + idx(d), k0 + d + idx(d),
        v0 + idx(d), v0 + idx(d), v0 + d + idx(d), v0 + d + idx(d),
        idx(v0 + A_KV_HEADS * d, w_in.shape[1]),
    ])
    w = w_in[:, cols].astype(BF16)
    scale = HEAD_DIM ** -0.5
    ones = lambda n: jnp.ones((n,), F32)
    gain = jnp.concatenate([
        jnp.tile(a_qn, A_Q_HEADS) * scale, jnp.tile(a_kn, 4), ones(256),
        jnp.tile(b_qn, 2 * B_HEADS) * scale, jnp.tile(b_kn, 2 * B_HEADS), ones(512),
    ]).astype(F32)[None, :]
    seg_id = jnp.arange(MXU_COLS) // d
    seg = jnp.where(seg_id[:, None] == seg_id[None, :], 1.0 / d, 0.0).astype(BF16)
    return w, gain, seg


def _trunk(x, p, *, tm, tile, ts):
    bsz, s_len, _ = x.shape
    rows = bsz * s_len
    x2d = x.reshape(rows, D_MODEL)

    proj = _attn_in(x2d, p["mix_g"][0], p["attn_w"], p["attn_gain"], p["seg"], tm)
    proj = proj.reshape(bsz, s_len, PROJ0_W)
    ya = _win_attn(proj, p["sink"], p["win_bias"])
    near, far = p["diff_bias"][tile]
    yb = _diff_attn(proj, far, near, p["lam"], p["subln"], p["lam_init"], tile)
    x2d = _mix_ffn(x2d, ya.reshape(rows, -1), yb.reshape(rows, -1), p["attn_wo_a"], p["attn_wo_b"],
                   p["ffn_g"][0], p["wg"][0], p["wu"][0], p["wd"][0], tm)

    proj = _conv_in(x2d, p["mix_g"][1], p["conv_w"], tm).reshape(bsz, s_len, PROJ1_W)
    yc, yu = _conv_mix(proj, p["scw"], p["dww"], p["dwb"], p["lng"], p["lnb"], ts)
    x2d = _mix_ffn(x2d, yc.reshape(rows, -1), yu.reshape(rows, -1), p["conv_wo_a"], p["conv_wo_b"],
                   p["ffn_g"][1], p["wg"][1], p["wu"][1], p["wd"][1], tm)
    return x2d.reshape(bsz, s_len, D_MODEL)


def _prepare(rel_bias, mix_norm, ffn_norm, w_gate, w_up, w_down, attn_w_in, attn_w_out, a_q_norm, a_k_norm,
             a_sink, b_q_norm, b_k_norm, b_lambda, b_subln, conv_w_in, conv_w_out, short_conv_w, conf_dw_w,
             conf_dw_b, conf_ln_g, conf_ln_b, tiles):
    attn_w, attn_gain, seg = _attn_in_params(attn_w_in[0], a_q_norm[0], a_k_norm[0], b_q_norm[0], b_k_norm[0])
    half = A_Q_HEADS * HEAD_DIM
    return {
        "mix_g": [mix_norm[l][None, :].astype(F32) for l in range(2)],
        "ffn_g": [ffn_norm[l][None, :].astype(F32) for l in range(2)],
        "wg": [w_gate[l].astype(BF16) for l in range(2)],
        "wu": [w_up[l].astype(BF16) for l in range(2)],
        "wd": [w_down[l].astype(BF16) for l in range(2)],
        "attn_w": attn_w, "attn_gain": attn_gain, "seg": seg,
        "attn_wo_a": attn_w_out[0][:half].astype(BF16), "attn_wo_b": attn_w_out[0][half:].astype(BF16),
        "sink": a_sink[0].astype(F32),
        "win_bias": _win_bias(rel_bias),
        "diff_bias": {t: _diff_bias(rel_bias, t) for t in tiles},
        "lam": b_lambda[0].astype(F32),
        "subln": b_subln[0].astype(F32)[:, None],
        "lam_init": 0.8 - 0.6 * math.exp(-0.3 * 0),
        "conv_w": conv_w_in[0].astype(BF16),
        "conv_wo_a": conv_w_out[0][:C_WIDTH].astype(BF16), "conv_wo_b": conv_w_out[0][C_WIDTH:].astype(BF16),
        "scw": short_conv_w[0].astype(F32), "dww": conf_dw_w[0].astype(F32),
        "dwb": conf_dw_b[0][None, :].astype(F32),
        "lng": conf_ln_g[0][None, :].astype(F32), "lnb": conf_ln_b[0][None, :].astype(F32),
    }


def _tiling(s_len):
    return dict(tm=512, tile=min(512, s_len), ts=min(512, s_len))


def kernel(x_prompt, x_sample, rel_bias, mix_norm, ffn_norm, w_gate, w_up, w_down, attn_w_in, attn_w_out,
           a_q_norm, a_k_norm, a_sink, b_q_norm, b_k_norm, b_lambda, b_subln, conv_w_in, conv_w_out,
           short_conv_w, conf_dw_w, conf_dw_b, conf_ln_g, conf_ln_b):
    tp, tsm = _tiling(x_prompt.shape[1]), _tiling(x_sample.shape[1])
    p = _prepare(rel_bias, mix_norm, ffn_norm, w_gate, w_up, w_down, attn_w_in, attn_w_out, a_q_norm,
                 a_k_norm, a_sink, b_q_norm, b_k_norm, b_lambda, b_subln, conv_w_in, conv_w_out,
                 short_conv_w, conf_dw_w, conf_dw_b, conf_ln_g, conf_ln_b, {tp["tile"], tsm["tile"]})
    return (_trunk(x_prompt, p, **tp), _trunk(x_sample, p, **tsm))
```

```python
import functools
import math

import jax
import jax.numpy as jnp
from jax import lax
from jax.experimental import pallas as pl
from jax.experimental.pallas import tpu as pltpu

D_MODEL = 1024
HEAD_DIM = 64
A_Q_HEADS = 8
A_KV_HEADS = 2
WINDOW = 128
BLOCK = 128
B_HEADS = 4
NUM_BUCKETS = 32
MAX_DISTANCE = 128
C_WIDTH = 512
D_WIDTH = 512
SHORT_CONV = 3
CONF_CONV = 31
FFN_HIDDEN = 2816
EPS = 1e-6
NEG = -1e30

LANES = 128
MXU_COLS = 256
VMEM_LIMIT_BYTES = 56 * 1024 * 1024

BF16 = jnp.bfloat16
F32 = jnp.float32

QA_OFF = 0
KA_OFF = 512
VA_OFF = 768
QB_OFF = 1024
KB_OFF = 1536
VB_OFF = 2048
PROJ0_W = 2560
NORM_CHUNKS0 = (True, True, True, False, True, True, True, True, False, False)

GB_OFF = 0
GX_OFF = 512
GLU_OFF = 1024
PROJ1_W = 1536
HALO = 16


def _cparams(sem):
    return pltpu.CompilerParams(dimension_semantics=sem, vmem_limit_bytes=VMEM_LIMIT_BYTES)


def _const_spec(shape):
    nd = len(shape)
    return pl.BlockSpec(shape, lambda *_: (0,) * nd)


def _rms(x, g):
    ms = jnp.mean(x * x, axis=-1, keepdims=True)
    return x * lax.rsqrt(ms + EPS) * g


def _attn_in_kernel(x_ref, g_ref, w_ref, gain_ref, seg_ref, o_ref):
    h = _rms(x_ref[...], g_ref[...]).astype(BF16)
    seg = seg_ref[...]
    for c, normed in enumerate(NORM_CHUNKS0):
        lo, hi = c * MXU_COLS, (c + 1) * MXU_COLS
        acc = jnp.dot(h, w_ref[:, lo:hi], preferred_element_type=F32)
        if normed:
            sq = acc * acc
            sq_hi = sq.astype(BF16)
            sq_lo = (sq - sq_hi.astype(F32)).astype(BF16)
            ms = (jnp.dot(sq_hi, seg, preferred_element_type=F32)
                  + jnp.dot(sq_lo, seg, preferred_element_type=F32))
            acc = acc * lax.rsqrt(ms + EPS) * gain_ref[:, lo:hi]
        o_ref[:, lo:hi] = acc.astype(BF16)


def _attn_in(x2d, g, w, gain, seg, tm):
    rows = x2d.shape[0]
    return pl.pallas_call(
        _attn_in_kernel,
        grid=(rows // tm,),
        in_specs=[
            pl.BlockSpec((tm, D_MODEL), lambda i: (i, 0)),
            _const_spec((1, D_MODEL)),
            _const_spec((D_MODEL, PROJ0_W)),
            _const_spec((1, PROJ0_W)),
            _const_spec((MXU_COLS, MXU_COLS)),
        ],
        out_specs=pl.BlockSpec((tm, PROJ0_W), lambda i: (i, 0)),
        out_shape=jax.ShapeDtypeStruct((rows, PROJ0_W), BF16),
        compiler_params=_cparams(("parallel",)),
        name="attn_in",
    )(x2d, g, w, gain, seg)


def _win_attn_kernel(sink_ref, q_ref, kp_ref, kc_ref, kn_ref, vp_ref, vc_ref, vn_ref, bias_ref, o_ref):
    n = pl.program_id(1)
    nb = pl.num_programs(1)
    col = lax.broadcasted_iota(jnp.int32, (1, 3 * BLOCK), 1)
    invalid = ((col < BLOCK) & (n == 0)) | ((col >= 2 * BLOCK) & (n == nb - 1))
    lane = lax.broadcasted_iota(jnp.int32, (1, LANES), 1)
    low = lane < HEAD_DIM
    row = lax.broadcasted_iota(jnp.int32, (2 * BLOCK, 1), 0)
    first = row < BLOCK
    for j in range(A_Q_HEADS // 2):
        kh = j // 2
        kcat = jnp.concatenate([kp_ref[0, :, kh * LANES:(kh + 1) * LANES],
                                kc_ref[0, :, kh * LANES:(kh + 1) * LANES],
                                kn_ref[0, :, kh * LANES:(kh + 1) * LANES]], axis=0)
        vcat = jnp.concatenate([vp_ref[0, :, kh * LANES:(kh + 1) * LANES],
                                vc_ref[0, :, kh * LANES:(kh + 1) * LANES],
                                vn_ref[0, :, kh * LANES:(kh + 1) * LANES]], axis=0)
        q = q_ref[0, :, j * LANES:(j + 1) * LANES]
        zero = jnp.zeros_like(q)
        q2 = jnp.concatenate([jnp.where(low, q, zero), jnp.where(low, zero, q)], axis=0)
        s = lax.dot_general(q2, kcat, (((1,), (1,)), ((), ())), preferred_element_type=F32)
        s = s + bias_ref[j]
        s = jnp.where(invalid, NEG, s)
        sk = jnp.where(first, sink_ref[2 * j], sink_ref[2 * j + 1])
        m = jnp.maximum(jnp.max(s, axis=-1, keepdims=True), sk)
        e = jnp.exp(s - m)
        den = jnp.sum(e, axis=-1, keepdims=True) + jnp.exp(sk - m)
        o = jnp.dot(e.astype(BF16), vcat, preferred_element_type=F32) / den
        o_ref[0, :, j * LANES:(j + 1) * LANES] = jnp.where(low, o[:BLOCK], o[BLOCK:]).astype(BF16)


def _win_attn(proj, sink, bias):
    bsz, s_len, _ = proj.shape
    nb = s_len // BLOCK
    qcol, kcol, vcol = QA_OFF // 512, KA_OFF // 256, VA_OFF // 256
    kv_specs = []
    for cb in (kcol, vcol):
        kv_specs += [
            pl.BlockSpec((1, BLOCK, 256), lambda b, n, cb=cb: (b, jnp.maximum(n - 1, 0), cb)),
            pl.BlockSpec((1, BLOCK, 256), lambda b, n, cb=cb: (b, n, cb)),
            pl.BlockSpec((1, BLOCK, 256), lambda b, n, cb=cb: (b, jnp.minimum(n + 1, nb - 1), cb)),
        ]
    return pl.pallas_call(
        _win_attn_kernel,
        grid=(bsz, nb),
        in_specs=[pl.BlockSpec(memory_space=pltpu.SMEM),
                  pl.BlockSpec((1, BLOCK, 512), lambda b, n: (b, n, qcol))] + kv_specs
                 + [_const_spec((A_Q_HEADS // 2, 2 * BLOCK, 3 * BLOCK))],
        out_specs=pl.BlockSpec((1, BLOCK, 512), lambda b, n: (b, n, 0)),
        out_shape=jax.ShapeDtypeStruct((bsz, s_len, 512), BF16),
        compiler_params=_cparams(("parallel", "parallel")),
        name="win_attn",
    )(sink, proj, proj, proj, proj, proj, proj, proj, bias)


_NT = (((1,), (1,)), ((), ()))
_TN = (((0,), (0,)), ((), ()))


def _split_maps(q, qz_ref):
    lane = lax.broadcasted_iota(jnp.int32, (1, LANES), 1)
    zero = jnp.zeros_like(q)
    qz_ref[0] = jnp.where(lane < HEAD_DIM, q, zero)
    qz_ref[1] = jnp.where(lane < HEAD_DIM, zero, q)


def _diff_finish(acc0, l0, acc1, l1, lam_ref, g_ref, o_ref, lam_init):
    lam = lam_ref[...]
    lam_full = (jnp.exp(jnp.sum(lam[0:1] * lam[1:2], axis=-1, keepdims=True))
                - jnp.exp(jnp.sum(lam[2:3] * lam[3:4], axis=-1, keepdims=True)) + lam_init)
    o = acc0 / l0 - lam_full * (acc1 / l1)
    ms = jnp.mean(o * o, axis=0, keepdims=True)
    y = o * lax.rsqrt(ms + EPS) * g_ref[...] * (1.0 - lam_init)
    o_ref[0] = y.T.astype(BF16)


def _diff_attn_bounded_kernel(shift_ref, q_ref, k_ref, v_ref, bias_ref, lam_ref, g_ref, o_ref,
                              qz_ref, l_ref, acc_ref, *, lam_init, tile):
    h = pl.program_id(1)
    qi = pl.program_id(2)
    nk = k_ref.shape[1] // tile
    _split_maps(q_ref[0], qz_ref)
    l_ref[...] = jnp.zeros(l_ref.shape, F32)
    acc_ref[...] = jnp.zeros(acc_ref.shape, F32)

    def body(ki, carry):
        start = pl.multiple_of(ki * tile, tile)
        k = k_ref[0, pl.ds(start, tile), :]
        v = v_ref[0, pl.ds(start, tile), :]
        delta = ki - qi

        def step(c, bias_tile, shift):
            s_t = lax.dot_general(k, qz_ref[c], _NT, preferred_element_type=F32)
            if bias_tile is not None:
                s_t = s_t + bias_tile
            p = jnp.exp2(s_t + shift)
            l_ref[c] += jnp.sum(p.reshape(tile // 8, 8, tile), axis=0)
            acc_ref[c] += lax.dot_general(v, p.astype(BF16), _TN, preferred_element_type=F32)

        @pl.when(jnp.abs(delta) <= 1)
        def _near():
            bias_tile = bias_ref[0, delta + 1]
            for c in range(2):
                step(c, bias_tile, shift_ref[h, 1])

        @pl.when(jnp.abs(delta) > 1)
        def _far():
            shift = jnp.where(delta < 0, shift_ref[h, 0], shift_ref[h, 2])
            for c in range(2):
                step(c, None, shift)

        return carry

    lax.fori_loop(0, nk, body, 0)
    l0 = jnp.sum(l_ref[0], axis=0, keepdims=True)
    l1 = jnp.sum(l_ref[1], axis=0, keepdims=True)
    _diff_finish(acc_ref[0], l0, acc_ref[1], l1, lam_ref, g_ref, o_ref, lam_init)


def _diff_attn_bounded(proj, shift, bias_t, lam, g_col, lam_init, tile):
    bsz, s_len, _ = proj.shape
    qcol, kcol, vcol = QB_OFF // LANES, KB_OFF // LANES, VB_OFF // LANES
    return pl.pallas_call(
        functools.partial(_diff_attn_bounded_kernel, lam_init=lam_init, tile=tile),
        grid=(bsz, B_HEADS, s_len // tile),
        in_specs=[
            pl.BlockSpec(memory_space=pltpu.SMEM),
            pl.BlockSpec((1, tile, LANES), lambda b, h, qi: (b, qi, qcol + h)),
            pl.BlockSpec((1, s_len, LANES), lambda b, h, qi: (b, 0, kcol + h)),
            pl.BlockSpec((1, s_len, LANES), lambda b, h, qi: (b, 0, vcol + h)),
            pl.BlockSpec((1, 3, tile, tile), lambda b, h, qi: (h, 0, 0, 0)),
            _const_spec((4, HEAD_DIM)),
            _const_spec((2 * HEAD_DIM, 1)),
        ],
        out_specs=pl.BlockSpec((1, tile, LANES), lambda b, h, qi: (b, qi, h)),
        out_shape=jax.ShapeDtypeStruct((bsz, s_len, B_HEADS * 2 * HEAD_DIM), BF16),
        scratch_shapes=[
            pltpu.VMEM((2, tile, LANES), BF16),
            pltpu.VMEM((2, 8, tile), F32),
            pltpu.VMEM((2, 2 * HEAD_DIM, tile), F32),
        ],
        compiler_params=_cparams(("parallel", "parallel", "parallel")),
        name="diff_attn",
    )(shift, proj, proj, proj, bias_t, lam, g_col)


def _diff_attn_online_kernel(far_ref, q_ref, k_ref, v_ref, bias_ref, lam_ref, g_ref, o_ref,
                             qz_ref, m_ref, l_ref, acc_ref, *, lam_init):
    h = pl.program_id(1)
    qi = pl.program_id(2)
    ki = pl.program_id(3)
    nk = pl.num_programs(3)

    @pl.when(ki == 0)
    def _init():
        _split_maps(q_ref[0], qz_ref)
        m_ref[...] = jnp.full(m_ref.shape, NEG, F32)
        l_ref[...] = jnp.zeros(l_ref.shape, F32)
        acc_ref[...] = jnp.zeros(acc_ref.shape, F32)

    delta = ki - qi

    def step(c, bias_tile, bias_row):
        s_t = lax.dot_general(k_ref[0], qz_ref[c], _NT, preferred_element_type=F32)
        if bias_tile is not None:
            s_t = s_t + bias_tile
        m_old = m_ref[c]
        m_new = jnp.maximum(m_old, jnp.max(s_t, axis=0, keepdims=True) + bias_row)
        alpha = jnp.exp2(m_old - m_new)
        p = jnp.exp2(s_t - (m_new - bias_row))
        l_ref[c] = alpha * l_ref[c] + jnp.sum(p, axis=0, keepdims=True)
        pv = lax.dot_general(v_ref[0], p.astype(BF16), _TN, preferred_element_type=F32)
        acc_ref[c] = alpha * acc_ref[c] + pv
        m_ref[c] = m_new

    @pl.when(jnp.abs(delta) <= 1)
    def _near():
        bias_tile = bias_ref[0, delta + 1]
        for c in range(2):
            step(c, bias_tile, 0.0)

    @pl.when(jnp.abs(delta) > 1)
    def _far():
        side = jnp.where(delta < 0, far_ref[h, 0], far_ref[h, 1])
        for c in range(2):
            step(c, None, side)

    @pl.when(ki == nk - 1)
    def _finish():
        _diff_finish(acc_ref[0], l_ref[0], acc_ref[1], l_ref[1], lam_ref, g_ref, o_ref, lam_init)


def _diff_attn_online(proj, far, bias_t, lam, g_col, lam_init, tile):
    bsz, s_len, _ = proj.shape
    nt = s_len // tile
    qcol, kcol, vcol = QB_OFF // LANES, KB_OFF // LANES, VB_OFF // LANES
    return pl.pallas_call(
        functools.partial(_diff_attn_online_kernel, lam_init=lam_init),
        grid=(bsz, B_HEADS, nt, nt),
        in_specs=[
            pl.BlockSpec(memory_space=pltpu.SMEM),
            pl.BlockSpec((1, tile, LANES), lambda b, h, qi, ki: (b, qi, qcol + h)),
            pl.BlockSpec((1, tile, LANES), lambda b, h, qi, ki: (b, ki, kcol + h)),
            pl.BlockSpec((1, tile, LANES), lambda b, h, qi, ki: (b, ki, vcol + h)),
            pl.BlockSpec((1, 3, tile, tile), lambda b, h, qi, ki: (h, 0, 0, 0)),
            _const_spec((4, HEAD_DIM)),
            _const_spec((2 * HEAD_DIM, 1)),
        ],
        out_specs=pl.BlockSpec((1, tile, LANES), lambda b, h, qi, ki: (b, qi, h)),
        out_shape=jax.ShapeDtypeStruct((bsz, s_len, B_HEADS * 2 * HEAD_DIM), BF16),
        scratch_shapes=[
            pltpu.VMEM((2, tile, LANES), BF16),
            pltpu.VMEM((2, 1, tile), F32),
            pltpu.VMEM((2, 1, tile), F32),
            pltpu.VMEM((2, 2 * HEAD_DIM, tile), F32),
        ],
        compiler_params=_cparams(("parallel", "parallel", "parallel", "arbitrary")),
        name="diff_attn_online",
    )(far, proj, proj, proj, bias_t, lam, g_col)


def _diff_attn(proj, db, lam, g_col, lam_init, tile):
    bounded = lambda: _diff_attn_bounded(proj, db["shift"], db["near"], lam, g_col, lam_init, tile)
    online = lambda: _diff_attn_online(proj, db["far"], db["near"], lam, g_col, lam_init, tile)
    return lax.cond(db["bounded_ok"], bounded, online)


FFN_CHUNKS = ((0, 1536), (1536, FFN_HIDDEN))


def _mix_ffn_kernel(x_ref, a_ref, b_ref, wa_ref, wb_ref, g_ref, wg_ref, wu_ref, wd_ref, o_ref):
    mix = (jnp.dot(a_ref[...], wa_ref[...], preferred_element_type=F32)
           + jnp.dot(b_ref[...], wb_ref[...], preferred_element_type=F32))
    x1 = x_ref[...] + mix
    h = _rms(x1, g_ref[...]).astype(BF16)
    down = None
    for lo, hi in FFN_CHUNKS:
        gate = jnp.dot(h, wg_ref[:, lo:hi], preferred_element_type=F32)
        up = jnp.dot(h, wu_ref[:, lo:hi], preferred_element_type=F32)
        act = (gate * jax.nn.sigmoid(gate) * up).astype(BF16)
        part = jnp.dot(act, wd_ref[lo:hi, :], preferred_element_type=F32)
        down = part if down is None else down + part
    o_ref[...] = x1 + down


def _mix_ffn(x2d, a2d, b2d, wa, wb, g, wg, wu, wd, tm):
    rows = x2d.shape[0]
    half = a2d.shape[1]
    row_spec = lambda w: pl.BlockSpec((tm, w), lambda i: (i, 0))
    single = lambda shape: pl.BlockSpec(memory_space=pltpu.VMEM)
    return pl.pallas_call(
        _mix_ffn_kernel,
        grid=(rows // tm,),
        in_specs=[
            row_spec(D_MODEL), row_spec(half), row_spec(half),
            single((half, D_MODEL)), single((half, D_MODEL)),
            single((1, D_MODEL)),
            single((D_MODEL, FFN_HIDDEN)), single((D_MODEL, FFN_HIDDEN)), single((FFN_HIDDEN, D_MODEL)),
        ],
        out_specs=row_spec(D_MODEL),
        out_shape=jax.ShapeDtypeStruct((rows, D_MODEL), F32),
        compiler_params=_cparams(("parallel",)),
        name="mix_ffn",
    )(x2d, a2d, b2d, wa, wb, g, wg, wu, wd)


def _conv_in_kernel(x_ref, g_ref, w_ref, o_ref):
    h = _rms(x_ref[...], g_ref[...]).astype(BF16)
    c = C_WIDTH
    dot = lambda lo, hi: jnp.dot(h, w_ref[:, lo:hi], preferred_element_type=F32)
    o_ref[:, GB_OFF:GB_OFF + c] = dot(0, c).astype(BF16)
    o_ref[:, GX_OFF:GX_OFF + c] = (dot(c, 2 * c) * dot(2 * c, 3 * c)).astype(BF16)
    a = dot(3 * c, 3 * c + D_WIDTH)
    gate = dot(3 * c + D_WIDTH, 3 * c + 2 * D_WIDTH)
    o_ref[:, GLU_OFF:GLU_OFF + D_WIDTH] = (a * jax.nn.sigmoid(gate)).astype(BF16)


def _conv_in(x2d, g, w, tm):
    rows = x2d.shape[0]
    conv_in_w = w.shape[1]
    return pl.pallas_call(
        _conv_in_kernel,
        grid=(rows // tm,),
        in_specs=[
            pl.BlockSpec((tm, D_MODEL), lambda i: (i, 0)),
            _const_spec((1, D_MODEL)),
            _const_spec((D_MODEL, conv_in_w)),
        ],
        out_specs=pl.BlockSpec((tm, PROJ1_W), lambda i: (i, 0)),
        out_shape=jax.ShapeDtypeStruct((rows, PROJ1_W), BF16),
        compiler_params=_cparams(("parallel",)),
        name="conv_in",
    )(x2d, g, w)


def _conv_mix_kernel(gb_ref, gx_ref, gxp_ref, gxn_ref, u_ref, up_ref, un_ref,
                     scw_ref, dww_ref, dwb_ref, lng_ref, lnb_ref, yc_ref, yu_ref, pad_ref, *, ts):
    t = pl.program_id(1)
    nt = pl.num_programs(1)
    has_prev = (t > 0).astype(F32)
    has_next = (t < nt - 1).astype(F32)

    def fill(cur, prev, nxt):
        pad_ref[0:HALO, :] = prev[0].astype(F32) * has_prev
        pad_ref[HALO:HALO + ts, :] = cur[0].astype(F32)
        pad_ref[HALO + ts:2 * HALO + ts, :] = nxt[0].astype(F32) * has_next

    def conv(w_ref, width):
        off = HALO - width // 2
        acc = pad_ref[off:off + ts, :] * w_ref[0:1, :]
        for j in range(1, width):
            acc = acc + pad_ref[off + j:off + j + ts, :] * w_ref[j:j + 1, :]
        return acc

    fill(gx_ref, gxp_ref, gxn_ref)
    yc_ref[0] = (gb_ref[0].astype(F32) * conv(scw_ref, SHORT_CONV)).astype(BF16)

    fill(u_ref, up_ref, un_ref)
    u = conv(dww_ref, CONF_CONV) + dwb_ref[...]
    mu = jnp.mean(u, axis=-1, keepdims=True)
    uc = u - mu
    var = jnp.mean(uc * uc, axis=-1, keepdims=True)
    y = uc * lax.rsqrt(var + EPS) * lng_ref[...] + lnb_ref[...]
    yu_ref[0] = (y * jax.nn.sigmoid(y)).astype(BF16)


def _conv_mix(proj, scw, dww, dwb, lng, lnb, ts):
    bsz, s_len, _ = proj.shape
    nt = s_len // ts
    r = ts // HALO
    nh = s_len // HALO
    cur = lambda cb: pl.BlockSpec((1, ts, 512), lambda b, t: (b, t, cb))
    prev = lambda cb: pl.BlockSpec((1, HALO, 512), lambda b, t: (b, jnp.maximum(t * r - 1, 0), cb))
    nxt = lambda cb: pl.BlockSpec((1, HALO, 512), lambda b, t: (b, jnp.minimum((t + 1) * r, nh - 1), cb))
    gb, gx, gl = GB_OFF // 512, GX_OFF // 512, GLU_OFF // 512
    out_spec = pl.BlockSpec((1, ts, 512), lambda b, t: (b, t, 0))
    return pl.pallas_call(
        functools.partial(_conv_mix_kernel, ts=ts),
        grid=(bsz, nt),
        in_specs=[cur(gb), cur(gx), prev(gx), nxt(gx), cur(gl), prev(gl), nxt(gl),
                  _const_spec((SHORT_CONV, C_WIDTH)), _const_spec((CONF_CONV, D_WIDTH)),
                  _const_spec((1, D_WIDTH)), _const_spec((1, D_WIDTH)), _const_spec((1, D_WIDTH))],
        out_specs=[out_spec, out_spec],
        out_shape=[jax.ShapeDtypeStruct((bsz, s_len, C_WIDTH), BF16),
                   jax.ShapeDtypeStruct((bsz, s_len, D_WIDTH), BF16)],
        scratch_shapes=[pltpu.VMEM((ts + 2 * HALO, 512), F32)],
        compiler_params=_cparams(("parallel", "parallel")),
        name="conv_mix",
    )(proj, proj, proj, proj, proj, proj, proj, scw, dww, dwb, lng, lnb)


def _rel_bucket(rel):
    half = NUM_BUCKETS // 2
    max_exact = half // 2
    n = jnp.abs(rel)
    large = max_exact + (jnp.log(jnp.maximum(n, 1).astype(F32) / max_exact)
                         / math.log(MAX_DISTANCE / max_exact) * (half - max_exact)).astype(jnp.int32)
    large = jnp.minimum(large, half - 1)
    return jnp.where(rel > 0, half, 0) + jnp.where(n < max_exact, n, large)


def _toeplitz(u, rows, cols):
    length = u.shape[-1]
    flat = jnp.tile(u, (1,) * (u.ndim - 1) + (rows,))[..., :rows * (length - 1)]
    return flat.reshape(u.shape[:-1] + (rows, length - 1))[..., :cols]


def _win_bias(rel_bias):
    length = 4 * BLOCK
    n = jnp.arange(length)
    rel = jnp.where(n < 3 * BLOCK, n, n - length) - BLOCK
    vec = rel_bias[:, :A_Q_HEADS][_rel_bucket(rel)].astype(F32).T
    vec = jnp.where((jnp.abs(rel) <= WINDOW)[None], vec, NEG)
    return _toeplitz(vec, BLOCK, 3 * BLOCK).reshape(A_Q_HEADS // 2, 2 * BLOCK, 3 * BLOCK)


LOG2E = math.log2(math.e)
MIN_EXP2_ARG = -120.0


def _diff_bias(rel_bias, b_qn, b_kn, tile):
    table = rel_bias[:, A_Q_HEADS:].astype(F32) * LOG2E
    length = 2 * tile
    n = jnp.arange(length)
    k_minus_q = jnp.where(n < tile, -n, length - n)
    rel = jnp.clip(k_minus_q[None, :] + jnp.array([-tile, 0, tile])[:, None], 1 - length, length - 1)
    near = _toeplitz(table[_rel_bucket(rel)].transpose(2, 0, 1), tile, tile)
    far = table[_rel_bucket(jnp.array([-length, length]))].T
    s_max = 1.02 * LOG2E * math.sqrt(HEAD_DIM) * jnp.max(jnp.abs(b_qn * b_kn))
    b_max, b_min = jnp.max(table, axis=0), jnp.min(table, axis=0)
    bound = s_max + b_max
    shift = jnp.stack([far[:, 0] - bound, -bound, far[:, 1] - bound], axis=1)
    bounded_ok = jnp.all(-2.0 * s_max - (b_max - b_min) > MIN_EXP2_ARG)
    return {"near": near, "far": far, "shift": shift, "bounded_ok": bounded_ok}


def _attn_in_params(w_in, a_qn, a_kn, b_qn, b_kn):
    d = HEAD_DIM
    k0, v0 = A_Q_HEADS * d, A_Q_HEADS * d + A_KV_HEADS * d
    sl = lambda lo: w_in[:, lo:lo + d]
    w = jnp.concatenate([
        w_in[:, :k0],
        sl(k0), sl(k0), sl(k0 + d), sl(k0 + d),
        sl(v0), sl(v0), sl(v0 + d), sl(v0 + d),
        w_in[:, v0 + A_KV_HEADS * d:],
    ], axis=1).astype(BF16)
    scale = HEAD_DIM ** -0.5
    ones = lambda n: jnp.ones((n,), F32)
    gain = jnp.concatenate([
        jnp.tile(a_qn, A_Q_HEADS) * scale, jnp.tile(a_kn, 4), ones(256),
        jnp.tile(b_qn, 2 * B_HEADS) * (scale * LOG2E), jnp.tile(b_kn, 2 * B_HEADS), ones(512),
    ]).astype(F32)[None, :]
    seg_id = jnp.arange(MXU_COLS) // d
    seg = jnp.where(seg_id[:, None] == seg_id[None, :], 1.0 / d, 0.0).astype(BF16)
    return w, gain, seg


def _trunk(x, p, *, tm, tile, ts):
    bsz, s_len, _ = x.shape
    rows = bsz * s_len
    x2d = x.reshape(rows, D_MODEL)

    proj = _attn_in(x2d, p["mix_g"][0], p["attn_w"], p["attn_gain"], p["seg"], tm)
    proj = proj.reshape(bsz, s_len, PROJ0_W)
    ya = _win_attn(proj, p["sink"], p["win_bias"])
    yb = _diff_attn(proj, p["diff_bias"][tile], p["lam"], p["subln"], p["lam_init"], tile)
    x2d = _mix_ffn(x2d, ya.reshape(rows, -1), yb.reshape(rows, -1), p["attn_wo_a"], p["attn_wo_b"],
                   p["ffn_g"][0], p["wg"][0], p["wu"][0], p["wd"][0], tm)

    proj = _conv_in(x2d, p["mix_g"][1], p["conv_w"], tm).reshape(bsz, s_len, PROJ1_W)
    yc, yu = _conv_mix(proj, p["scw"], p["dww"], p["dwb"], p["lng"], p["lnb"], ts)
    x2d = _mix_ffn(x2d, yc.reshape(rows, -1), yu.reshape(rows, -1), p["conv_wo_a"], p["conv_wo_b"],
                   p["ffn_g"][1], p["wg"][1], p["wu"][1], p["wd"][1], tm)
    return x2d.reshape(bsz, s_len, D_MODEL)


def _prepare(rel_bias, mix_norm, ffn_norm, w_gate, w_up, w_down, attn_w_in, attn_w_out, a_q_norm, a_k_norm,
             a_sink, b_q_norm, b_k_norm, b_lambda, b_subln, conv_w_in, conv_w_out, short_conv_w, conf_dw_w,
             conf_dw_b, conf_ln_g, conf_ln_b, tiles):
    attn_w, attn_gain, seg = _attn_in_params(attn_w_in[0], a_q_norm[0], a_k_norm[0], b_q_norm[0], b_k_norm[0])
    half = A_Q_HEADS * HEAD_DIM
    return {
        "mix_g": [mix_norm[l][None, :].astype(F32) for l in range(2)],
        "ffn_g": [ffn_norm[l][None, :].astype(F32) for l in range(2)],
        "wg": [w_gate[l].astype(BF16) for l in range(2)],
        "wu": [w_up[l].astype(BF16) for l in range(2)],
        "wd": [w_down[l].astype(BF16) for l in range(2)],
        "attn_w": attn_w, "attn_gain": attn_gain, "seg": seg,
        "attn_wo_a": attn_w_out[0][:half].astype(BF16), "attn_wo_b": attn_w_out[0][half:].astype(BF16),
        "sink": a_sink[0].astype(F32),
        "win_bias": _win_bias(rel_bias),
        "diff_bias": {t: _diff_bias(rel_bias, b_q_norm[0], b_k_norm[0], t) for t in tiles},
        "lam": b_lambda[0].astype(F32),
        "subln": b_subln[0].astype(F32)[:, None],
        "lam_init": 0.8 - 0.6 * math.exp(-0.3 * 0),
        "conv_w": conv_w_in[0].astype(BF16),
        "conv_wo_a": conv_w_out[0][:C_WIDTH].astype(BF16), "conv_wo_b": conv_w_out[0][C_WIDTH:].astype(BF16),
        "scw": short_conv_w[0].astype(F32), "dww": conf_dw_w[0].astype(F32),
        "dwb": conf_dw_b[0][None, :].astype(F32),
        "lng": conf_ln_g[0][None, :].astype(F32), "lnb": conf_ln_b[0][None, :].astype(F32),
    }


def _tiling(s_len):
    return dict(tm=512, tile=min(512, s_len), ts=min(512, s_len))


def kernel(x_prompt, x_sample, rel_bias, mix_norm, ffn_norm, w_gate, w_up, w_down, attn_w_in, attn_w_out,
           a_q_norm, a_k_norm, a_sink, b_q_norm, b_k_norm, b_lambda, b_subln, conv_w_in, conv_w_out,
           short_conv_w, conf_dw_w, conf_dw_b, conf_ln_g, conf_ln_b):
    tp, tsm = _tiling(x_prompt.shape[1]), _tiling(x_sample.shape[1])
    p = _prepare(rel_bias, mix_norm, ffn_norm, w_gate, w_up, w_down, attn_w_in, attn_w_out, a_q_norm,
                 a_k_norm, a_sink, b_q_norm, b_k_norm, b_lambda, b_subln, conv_w_in, conv_w_out,
                 short_conv_w, conf_dw_w, conf_dw_b, conf_ln_g, conf_ln_b, {tp["tile"], tsm["tile"]})
    return (_trunk(x_prompt, p, **tp), _trunk(x_sample, p, **tsm))
```

```python
import functools
import math

import jax
import jax.numpy as jnp
from jax import lax
from jax.experimental import pallas as pl
from jax.experimental.pallas import tpu as pltpu

D_MODEL = 1024
HEAD_DIM = 64
A_Q_HEADS = 8
A_KV_HEADS = 2
WINDOW = 128
BLOCK = 128
B_HEADS = 4
NUM_BUCKETS = 32
MAX_DISTANCE = 128
C_WIDTH = 512
D_WIDTH = 512
SHORT_CONV = 3
CONF_CONV = 31
FFN_HIDDEN = 2816
EPS = 1e-6
NEG = -1e30

LANES = 128
MXU_COLS = 256
VMEM_LIMIT_BYTES = 56 * 1024 * 1024

BF16 = jnp.bfloat16
F32 = jnp.float32

QA_OFF = 0
KA_OFF = 512
VA_OFF = 768
QB_OFF = 1024
KB_OFF = 1536
VB_OFF = 2048
PROJ0_W = 2560
NORM_CHUNKS0 = (True, True, True, False, True, True, True, True, False, False)

GB_OFF = 0
GX_OFF = 512
GLU_OFF = 1024
PROJ1_W = 1536
HALO = 16


def _cparams(sem):
    return pltpu.CompilerParams(dimension_semantics=sem, vmem_limit_bytes=VMEM_LIMIT_BYTES)


def _const_spec(shape):
    nd = len(shape)
    return pl.BlockSpec(shape, lambda *_: (0,) * nd)


def _rms(x, g):
    ms = jnp.mean(x * x, axis=-1, keepdims=True)
    return x * lax.rsqrt(ms + EPS) * g


def _attn_in_kernel(x_ref, g_ref, w_ref, gain_ref, seg_ref, o_ref):
    h = _rms(x_ref[...], g_ref[...]).astype(BF16)
    seg = seg_ref[...]
    for c, normed in enumerate(NORM_CHUNKS0):
        lo, hi = c * MXU_COLS, (c + 1) * MXU_COLS
        acc = jnp.dot(h, w_ref[:, lo:hi], preferred_element_type=F32)
        if normed:
            sq = acc * acc
            sq_hi = sq.astype(BF16)
            sq_lo = (sq - sq_hi.astype(F32)).astype(BF16)
            ms = (jnp.dot(sq_hi, seg, preferred_element_type=F32)
                  + jnp.dot(sq_lo, seg, preferred_element_type=F32))
            acc = acc * lax.rsqrt(ms + EPS) * gain_ref[:, lo:hi]
        o_ref[:, lo:hi] = acc.astype(BF16)


def _attn_in(x2d, g, w, gain, seg, tm):
    rows = x2d.shape[0]
    return pl.pallas_call(
        _attn_in_kernel,
        grid=(rows // tm,),
        in_specs=[
            pl.BlockSpec((tm, D_MODEL), lambda i: (i, 0)),
            _const_spec((1, D_MODEL)),
            _const_spec((D_MODEL, PROJ0_W)),
            _const_spec((1, PROJ0_W)),
            _const_spec((MXU_COLS, MXU_COLS)),
        ],
        out_specs=pl.BlockSpec((tm, PROJ0_W), lambda i: (i, 0)),
        out_shape=jax.ShapeDtypeStruct((rows, PROJ0_W), BF16),
        compiler_params=_cparams(("parallel",)),
        name="attn_in",
    )(x2d, g, w, gain, seg)


def _win_attn_kernel(sink_ref, q_ref, kp_ref, kc_ref, kn_ref, vp_ref, vc_ref, vn_ref, bias_ref, o_ref):
    n = pl.program_id(1)
    nb = pl.num_programs(1)
    col = lax.broadcasted_iota(jnp.int32, (1, 3 * BLOCK), 1)
    invalid = ((col < BLOCK) & (n == 0)) | ((col >= 2 * BLOCK) & (n == nb - 1))
    lane = lax.broadcasted_iota(jnp.int32, (1, LANES), 1)
    low = lane < HEAD_DIM
    row = lax.broadcasted_iota(jnp.int32, (2 * BLOCK, 1), 0)
    first = row < BLOCK
    for j in range(A_Q_HEADS // 2):
        kh = j // 2
        kcat = jnp.concatenate([kp_ref[0, :, kh * LANES:(kh + 1) * LANES],
                                kc_ref[0, :, kh * LANES:(kh + 1) * LANES],
                                kn_ref[0, :, kh * LANES:(kh + 1) * LANES]], axis=0)
        vcat = jnp.concatenate([vp_ref[0, :, kh * LANES:(kh + 1) * LANES],
                                vc_ref[0, :, kh * LANES:(kh + 1) * LANES],
                                vn_ref[0, :, kh * LANES:(kh + 1) * LANES]], axis=0)
        q = q_ref[0, :, j * LANES:(j + 1) * LANES]
        zero = jnp.zeros_like(q)
        q2 = jnp.concatenate([jnp.where(low, q, zero), jnp.where(low, zero, q)], axis=0)
        s = lax.dot_general(q2, kcat, (((1,), (1,)), ((), ())), preferred_element_type=F32)
        s = s + bias_ref[j]
        s = jnp.where(invalid, NEG, s)
        sk = jnp.where(first, sink_ref[2 * j], sink_ref[2 * j + 1])
        m = jnp.maximum(jnp.max(s, axis=-1, keepdims=True), sk)
        e = jnp.exp(s - m)
        den = jnp.sum(e, axis=-1, keepdims=True) + jnp.exp(sk - m)
        o = jnp.dot(e.astype(BF16), vcat, preferred_element_type=F32) / den
        o_ref[0, :, j * LANES:(j + 1) * LANES] = jnp.where(low, o[:BLOCK], o[BLOCK:]).astype(BF16)


def _win_attn(proj, sink, bias):
    bsz, s_len, _ = proj.shape
    nb = s_len // BLOCK
    qcol, kcol, vcol = QA_OFF // 512, KA_OFF // 256, VA_OFF // 256
    kv_specs = []
    for cb in (kcol, vcol):
        kv_specs += [
            pl.BlockSpec((1, BLOCK, 256), lambda b, n, cb=cb: (b, jnp.maximum(n - 1, 0), cb)),
            pl.BlockSpec((1, BLOCK, 256), lambda b, n, cb=cb: (b, n, cb)),
            pl.BlockSpec((1, BLOCK, 256), lambda b, n, cb=cb: (b, jnp.minimum(n + 1, nb - 1), cb)),
        ]
    return pl.pallas_call(
        _win_attn_kernel,
        grid=(bsz, nb),
        in_specs=[pl.BlockSpec(memory_space=pltpu.SMEM),
                  pl.BlockSpec((1, BLOCK, 512), lambda b, n: (b, n, qcol))] + kv_specs
                 + [_const_spec((A_Q_HEADS // 2, 2 * BLOCK, 3 * BLOCK))],
        out_specs=pl.BlockSpec((1, BLOCK, 512), lambda b, n: (b, n, 0)),
        out_shape=jax.ShapeDtypeStruct((bsz, s_len, 512), BF16),
        compiler_params=_cparams(("parallel", "parallel")),
        name="win_attn",
    )(sink, proj, proj, proj, proj, proj, proj, proj, bias)


_NT = (((1,), (1,)), ((), ()))
_TN = (((0,), (0,)), ((), ()))


def _split_maps(q, qz_ref):
    lane = lax.broadcasted_iota(jnp.int32, (1, LANES), 1)
    zero = jnp.zeros_like(q)
    qz_ref[0] = jnp.where(lane < HEAD_DIM, q, zero)
    qz_ref[1] = jnp.where(lane < HEAD_DIM, zero, q)


def _diff_finish(acc0, l0, acc1, l1, lam_ref, g_ref, o_ref, lam_init):
    lam = lam_ref[...]
    lam_full = (jnp.exp(jnp.sum(lam[0:1] * lam[1:2], axis=-1, keepdims=True))
                - jnp.exp(jnp.sum(lam[2:3] * lam[3:4], axis=-1, keepdims=True)) + lam_init)
    o = acc0 / l0 - lam_full * (acc1 / l1)
    ms = jnp.mean(o * o, axis=0, keepdims=True)
    y = o * lax.rsqrt(ms + EPS) * g_ref[...] * (1.0 - lam_init)
    o_ref[0] = y.T.astype(BF16)


def _diff_attn_bounded_kernel(q_ref, k_ref, v_ref, bias_ref, lam_ref, g_ref, o_ref,
                              qz_ref, s_ref, l_ref, acc_ref, *, lam_init, tile, unroll):
    qi = pl.program_id(2)
    nk = k_ref.shape[1] // tile
    _split_maps(q_ref[0], qz_ref)
    l_ref[...] = jnp.zeros(l_ref.shape, F32)
    acc_ref[...] = jnp.zeros(acc_ref.shape, F32)

    def rows(ki):
        return pl.ds(pl.multiple_of(ki * tile, tile), tile)

    def produce(ki, buf):
        k = k_ref[0, rows(ki), :]
        for c in range(2):
            s_ref[buf, c] = lax.dot_general(k, qz_ref[c], _NT, preferred_element_type=F32)

    def consume(ki, buf):
        v = v_ref[0, rows(ki), :]
        bias = bias_ref[0, jnp.clip(ki - qi + 2, 0, 4)]
        for c in range(2):
            p = jnp.exp2(s_ref[buf, c] + bias)
            l_ref[c] += jnp.sum(p.reshape(tile // 8, 8, tile), axis=0)
            acc_ref[c] += lax.dot_general(v, p.astype(BF16), _TN, preferred_element_type=F32)

    def group(first, last_group):
        for t in range(unroll):
            if not (last_group and t == unroll - 1):
                produce(first + t + 1, (t + 1) % 2)
            consume(first + t, t % 2)

    def body(j, carry):
        group(j * unroll, False)
        return carry

    produce(0, 0)
    lax.fori_loop(0, nk // unroll - 1, body, 0)
    group(nk - unroll, True)
    l0 = jnp.sum(l_ref[0], axis=0, keepdims=True)
    l1 = jnp.sum(l_ref[1], axis=0, keepdims=True)
    _diff_finish(acc_ref[0], l0, acc_ref[1], l1, lam_ref, g_ref, o_ref, lam_init)


def _diff_attn_bounded(proj, bias_t, lam, g_col, lam_init, tile):
    bsz, s_len, _ = proj.shape
    nk = s_len // tile
    unroll = 4 if nk % 4 == 0 else 2
    assert nk % unroll == 0
    qcol, kcol, vcol = QB_OFF // LANES, KB_OFF // LANES, VB_OFF // LANES
    return pl.pallas_call(
        functools.partial(_diff_attn_bounded_kernel, lam_init=lam_init, tile=tile, unroll=unroll),
        grid=(bsz, B_HEADS, s_len // tile),
        in_specs=[
            pl.BlockSpec((1, tile, LANES), lambda b, h, qi: (b, qi, qcol + h)),
            pl.BlockSpec((1, s_len, LANES), lambda b, h, qi: (b, 0, kcol + h)),
            pl.BlockSpec((1, s_len, LANES), lambda b, h, qi: (b, 0, vcol + h)),
            pl.BlockSpec((1, 5, tile, tile), lambda b, h, qi: (h, 0, 0, 0)),
            _const_spec((4, HEAD_DIM)),
            _const_spec((2 * HEAD_DIM, 1)),
        ],
        out_specs=pl.BlockSpec((1, tile, LANES), lambda b, h, qi: (b, qi, h)),
        out_shape=jax.ShapeDtypeStruct((bsz, s_len, B_HEADS * 2 * HEAD_DIM), BF16),
        scratch_shapes=[
            pltpu.VMEM((2, tile, LANES), BF16),
            pltpu.VMEM((2, 2, tile, tile), F32),
            pltpu.VMEM((2, 8, tile), F32),
            pltpu.VMEM((2, 2 * HEAD_DIM, tile), F32),
        ],
        compiler_params=_cparams(("parallel", "parallel", "parallel")),
        name="diff_attn",
    )(proj, proj, proj, bias_t, lam, g_col)


def _diff_attn_online_kernel(far_ref, q_ref, k_ref, v_ref, bias_ref, lam_ref, g_ref, o_ref,
                             qz_ref, m_ref, l_ref, acc_ref, *, lam_init):
    h = pl.program_id(1)
    qi = pl.program_id(2)
    ki = pl.program_id(3)
    nk = pl.num_programs(3)

    @pl.when(ki == 0)
    def _init():
        _split_maps(q_ref[0], qz_ref)
        m_ref[...] = jnp.full(m_ref.shape, NEG, F32)
        l_ref[...] = jnp.zeros(l_ref.shape, F32)
        acc_ref[...] = jnp.zeros(acc_ref.shape, F32)

    delta = ki - qi

    def step(c, bias_tile, bias_row):
        s_t = lax.dot_general(k_ref[0], qz_ref[c], _NT, preferred_element_type=F32)
        if bias_tile is not None:
            s_t = s_t + bias_tile
        m_old = m_ref[c]
        m_new = jnp.maximum(m_old, jnp.max(s_t, axis=0, keepdims=True) + bias_row)
        alpha = jnp.exp2(m_old - m_new)
        p = jnp.exp2(s_t - (m_new - bias_row))
        l_ref[c] = alpha * l_ref[c] + jnp.sum(p, axis=0, keepdims=True)
        pv = lax.dot_general(v_ref[0], p.astype(BF16), _TN, preferred_element_type=F32)
        acc_ref[c] = alpha * acc_ref[c] + pv
        m_ref[c] = m_new

    @pl.when(jnp.abs(delta) <= 1)
    def _near():
        bias_tile = bias_ref[0, delta + 1]
        for c in range(2):
            step(c, bias_tile, 0.0)

    @pl.when(jnp.abs(delta) > 1)
    def _far():
        side = jnp.where(delta < 0, far_ref[h, 0], far_ref[h, 1])
        for c in range(2):
            step(c, None, side)

    @pl.when(ki == nk - 1)
    def _finish():
        _diff_finish(acc_ref[0], l_ref[0], acc_ref[1], l_ref[1], lam_ref, g_ref, o_ref, lam_init)


def _diff_attn_online(proj, far, bias_t, lam, g_col, lam_init, tile):
    bsz, s_len, _ = proj.shape
    nt = s_len // tile
    qcol, kcol, vcol = QB_OFF // LANES, KB_OFF // LANES, VB_OFF // LANES
    return pl.pallas_call(
        functools.partial(_diff_attn_online_kernel, lam_init=lam_init),
        grid=(bsz, B_HEADS, nt, nt),
        in_specs=[
            pl.BlockSpec(memory_space=pltpu.SMEM),
            pl.BlockSpec((1, tile, LANES), lambda b, h, qi, ki: (b, qi, qcol + h)),
            pl.BlockSpec((1, tile, LANES), lambda b, h, qi, ki: (b, ki, kcol + h)),
            pl.BlockSpec((1, tile, LANES), lambda b, h, qi, ki: (b, ki, vcol + h)),
            pl.BlockSpec((1, 3, tile, tile), lambda b, h, qi, ki: (h, 0, 0, 0)),
            _const_spec((4, HEAD_DIM)),
            _const_spec((2 * HEAD_DIM, 1)),
        ],
        out_specs=pl.BlockSpec((1, tile, LANES), lambda b, h, qi, ki: (b, qi, h)),
        out_shape=jax.ShapeDtypeStruct((bsz, s_len, B_HEADS * 2 * HEAD_DIM), BF16),
        scratch_shapes=[
            pltpu.VMEM((2, tile, LANES), BF16),
            pltpu.VMEM((2, 1, tile), F32),
            pltpu.VMEM((2, 1, tile), F32),
            pltpu.VMEM((2, 2 * HEAD_DIM, tile), F32),
        ],
        compiler_params=_cparams(("parallel", "parallel", "parallel", "arbitrary")),
        name="diff_attn_online",
    )(far, proj, proj, proj, bias_t, lam, g_col)


def _diff_attn(proj, db, lam, g_col, lam_init, tile):
    bounded = lambda: _diff_attn_bounded(proj, db["shifted"], lam, g_col, lam_init, tile)
    online = lambda: _diff_attn_online(proj, db["far"], db["near"], lam, g_col, lam_init, tile)
    return lax.cond(db["bounded_ok"], bounded, online)


FFN_CHUNKS = ((0, 1536), (1536, FFN_HIDDEN))


def _mix_ffn_kernel(x_ref, a_ref, b_ref, wa_ref, wb_ref, g_ref, wg_ref, wu_ref, wd_ref, o_ref):
    mix = (jnp.dot(a_ref[...], wa_ref[...], preferred_element_type=F32)
           + jnp.dot(b_ref[...], wb_ref[...], preferred_element_type=F32))
    x1 = x_ref[...] + mix
    h = _rms(x1, g_ref[...]).astype(BF16)
    down = None
    for lo, hi in FFN_CHUNKS:
        gate = jnp.dot(h, wg_ref[:, lo:hi], preferred_element_type=F32)
        up = jnp.dot(h, wu_ref[:, lo:hi], preferred_element_type=F32)
        act = (gate * jax.nn.sigmoid(gate) * up).astype(BF16)
        part = jnp.dot(act, wd_ref[lo:hi, :], preferred_element_type=F32)
        down = part if down is None else down + part
    o_ref[...] = x1 + down


def _mix_ffn(x2d, a2d, b2d, wa, wb, g, wg, wu, wd, tm):
    rows = x2d.shape[0]
    half = a2d.shape[1]
    row_spec = lambda w: pl.BlockSpec((tm, w), lambda i: (i, 0))
    single = lambda shape: pl.BlockSpec(memory_space=pltpu.VMEM)
    return pl.pallas_call(
        _mix_ffn_kernel,
        grid=(rows // tm,),
        in_specs=[
            row_spec(D_MODEL), row_spec(half), row_spec(half),
            single((half, D_MODEL)), single((half, D_MODEL)),
            single((1, D_MODEL)),
            single((D_MODEL, FFN_HIDDEN)), single((D_MODEL, FFN_HIDDEN)), single((FFN_HIDDEN, D_MODEL)),
        ],
        out_specs=row_spec(D_MODEL),
        out_shape=jax.ShapeDtypeStruct((rows, D_MODEL), F32),
        compiler_params=_cparams(("parallel",)),
        name="mix_ffn",
    )(x2d, a2d, b2d, wa, wb, g, wg, wu, wd)


def _conv_in_kernel(x_ref, g_ref, w_ref, o_ref):
    h = _rms(x_ref[...], g_ref[...]).astype(BF16)
    c = C_WIDTH
    dot = lambda lo, hi: jnp.dot(h, w_ref[:, lo:hi], preferred_element_type=F32)
    o_ref[:, GB_OFF:GB_OFF + c] = dot(0, c).astype(BF16)
    o_ref[:, GX_OFF:GX_OFF + c] = (dot(c, 2 * c) * dot(2 * c, 3 * c)).astype(BF16)
    a = dot(3 * c, 3 * c + D_WIDTH)
    gate = dot(3 * c + D_WIDTH, 3 * c + 2 * D_WIDTH)
    o_ref[:, GLU_OFF:GLU_OFF + D_WIDTH] = (a * jax.nn.sigmoid(gate)).astype(BF16)


def _conv_in(x2d, g, w, tm):
    rows = x2d.shape[0]
    conv_in_w = w.shape[1]
    return pl.pallas_call(
        _conv_in_kernel,
        grid=(rows // tm,),
        in_specs=[
            pl.BlockSpec((tm, D_MODEL), lambda i: (i, 0)),
            _const_spec((1, D_MODEL)),
            _const_spec((D_MODEL, conv_in_w)),
        ],
        out_specs=pl.BlockSpec((tm, PROJ1_W), lambda i: (i, 0)),
        out_shape=jax.ShapeDtypeStruct((rows, PROJ1_W), BF16),
        compiler_params=_cparams(("parallel",)),
        name="conv_in",
    )(x2d, g, w)


def _conv_mix_kernel(gb_ref, gx_ref, gxp_ref, gxn_ref, u_ref, up_ref, un_ref,
                     scw_ref, dww_ref, dwb_ref, lng_ref, lnb_ref, yc_ref, yu_ref, pad_ref, *, ts):
    t = pl.program_id(1)
    nt = pl.num_programs(1)
    has_prev = (t > 0).astype(F32)
    has_next = (t < nt - 1).astype(F32)

    def fill(cur, prev, nxt):
        pad_ref[0:HALO, :] = prev[0].astype(F32) * has_prev
        pad_ref[HALO:HALO + ts, :] = cur[0].astype(F32)
        pad_ref[HALO + ts:2 * HALO + ts, :] = nxt[0].astype(F32) * has_next

    def conv(w_ref, width):
        off = HALO - width // 2
        acc = pad_ref[off:off + ts, :] * w_ref[0:1, :]
        for j in range(1, width):
            acc = acc + pad_ref[off + j:off + j + ts, :] * w_ref[j:j + 1, :]
        return acc

    fill(gx_ref, gxp_ref, gxn_ref)
    yc_ref[0] = (gb_ref[0].astype(F32) * conv(scw_ref, SHORT_CONV)).astype(BF16)

    fill(u_ref, up_ref, un_ref)
    u = conv(dww_ref, CONF_CONV) + dwb_ref[...]
    mu = jnp.mean(u, axis=-1, keepdims=True)
    uc = u - mu
    var = jnp.mean(uc * uc, axis=-1, keepdims=True)
    y = uc * lax.rsqrt(var + EPS) * lng_ref[...] + lnb_ref[...]
    yu_ref[0] = (y * jax.nn.sigmoid(y)).astype(BF16)


def _conv_mix(proj, scw, dww, dwb, lng, lnb, ts):
    bsz, s_len, _ = proj.shape
    nt = s_len // ts
    r = ts // HALO
    nh = s_len // HALO
    cur = lambda cb: pl.BlockSpec((1, ts, 512), lambda b, t: (b, t, cb))
    prev = lambda cb: pl.BlockSpec((1, HALO, 512), lambda b, t: (b, jnp.maximum(t * r - 1, 0), cb))
    nxt = lambda cb: pl.BlockSpec((1, HALO, 512), lambda b, t: (b, jnp.minimum((t + 1) * r, nh - 1), cb))
    gb, gx, gl = GB_OFF // 512, GX_OFF // 512, GLU_OFF // 512
    out_spec = pl.BlockSpec((1, ts, 512), lambda b, t: (b, t, 0))
    return pl.pallas_call(
        functools.partial(_conv_mix_kernel, ts=ts),
        grid=(bsz, nt),
        in_specs=[cur(gb), cur(gx), prev(gx), nxt(gx), cur(gl), prev(gl), nxt(gl),
                  _const_spec((SHORT_CONV, C_WIDTH)), _const_spec((CONF_CONV, D_WIDTH)),
                  _const_spec((1, D_WIDTH)), _const_spec((1, D_WIDTH)), _const_spec((1, D_WIDTH))],
        out_specs=[out_spec, out_spec],
        out_shape=[jax.ShapeDtypeStruct((bsz, s_len, C_WIDTH), BF16),
                   jax.ShapeDtypeStruct((bsz, s_len, D_WIDTH), BF16)],
        scratch_shapes=[pltpu.VMEM((ts + 2 * HALO, 512), F32)],
        compiler_params=_cparams(("parallel", "parallel")),
        name="conv_mix",
    )(proj, proj, proj, proj, proj, proj, proj, scw, dww, dwb, lng, lnb)


def _rel_bucket(rel):
    half = NUM_BUCKETS // 2
    max_exact = half // 2
    n = jnp.abs(rel)
    large = max_exact + (jnp.log(jnp.maximum(n, 1).astype(F32) / max_exact)
                         / math.log(MAX_DISTANCE / max_exact) * (half - max_exact)).astype(jnp.int32)
    large = jnp.minimum(large, half - 1)
    return jnp.where(rel > 0, half, 0) + jnp.where(n < max_exact, n, large)


def _toeplitz(u, rows, cols):
    length = u.shape[-1]
    flat = jnp.tile(u, (1,) * (u.ndim - 1) + (rows,))[..., :rows * (length - 1)]
    return flat.reshape(u.shape[:-1] + (rows, length - 1))[..., :cols]


def _win_bias(rel_bias):
    length = 4 * BLOCK
    n = jnp.arange(length)
    rel = jnp.where(n < 3 * BLOCK, n, n - length) - BLOCK
    vec = rel_bias[:, :A_Q_HEADS][_rel_bucket(rel)].astype(F32).T
    vec = jnp.where((jnp.abs(rel) <= WINDOW)[None], vec, NEG)
    return _toeplitz(vec, BLOCK, 3 * BLOCK).reshape(A_Q_HEADS // 2, 2 * BLOCK, 3 * BLOCK)


LOG2E = math.log2(math.e)
MIN_EXP2_ARG = -120.0


def _diff_bias(rel_bias, b_qn, b_kn, tile):
    table = rel_bias[:, A_Q_HEADS:].astype(F32) * LOG2E
    length = 2 * tile
    n = jnp.arange(length)
    k_minus_q = jnp.where(n < tile, -n, length - n)
    rel = jnp.clip(k_minus_q[None, :] + jnp.array([-tile, 0, tile])[:, None], 1 - length, length - 1)
    near = _toeplitz(table[_rel_bucket(rel)].transpose(2, 0, 1), tile, tile)
    far = table[_rel_bucket(jnp.array([-length, length]))].T
    s_max = 1.02 * LOG2E * math.sqrt(HEAD_DIM) * jnp.max(jnp.abs(b_qn * b_kn))
    b_max, b_min = jnp.max(table, axis=0), jnp.min(table, axis=0)
    bound = s_max + b_max
    const = lambda col: jnp.broadcast_to((far[:, col] - bound)[:, None, None, None], (B_HEADS, 1, tile, tile))
    shifted = jnp.concatenate([const(0), near - bound[:, None, None, None], const(1)], axis=1)
    bounded_ok = jnp.all(-2.0 * s_max - (b_max - b_min) > MIN_EXP2_ARG)
    return {"near": near, "far": far, "shifted": shifted, "bounded_ok": bounded_ok}


def _attn_in_params(w_in, a_qn, a_kn, b_qn, b_kn):
    d = HEAD_DIM
    k0, v0 = A_Q_HEADS * d, A_Q_HEADS * d + A_KV_HEADS * d
    sl = lambda lo: w_in[:, lo:lo + d]
    w = jnp.concatenate([
        w_in[:, :k0],
        sl(k0), sl(k0), sl(k0 + d), sl(k0 + d),
        sl(v0), sl(v0), sl(v0 + d), sl(v0 + d),
        w_in[:, v0 + A_KV_HEADS * d:],
    ], axis=1).astype(BF16)
    scale = HEAD_DIM ** -0.5
    ones = lambda n: jnp.ones((n,), F32)
    gain = jnp.concatenate([
        jnp.tile(a_qn, A_Q_HEADS) * scale, jnp.tile(a_kn, 4), ones(256),
        jnp.tile(b_qn, 2 * B_HEADS) * (scale * LOG2E), jnp.tile(b_kn, 2 * B_HEADS), ones(512),
    ]).astype(F32)[None, :]
    seg_id = jnp.arange(MXU_COLS) // d
    seg = jnp.where(seg_id[:, None] == seg_id[None, :], 1.0 / d, 0.0).astype(BF16)
    return w, gain, seg


def _trunk(x, p, *, tm, tile, ts):
    bsz, s_len, _ = x.shape
    rows = bsz * s_len
    x2d = x.reshape(rows, D_MODEL)

    proj = _attn_in(x2d, p["mix_g"][0], p["attn_w"], p["attn_gain"], p["seg"], tm)
    proj = proj.reshape(bsz, s_len, PROJ0_W)
    ya = _win_attn(proj, p["sink"], p["win_bias"])
    yb = _diff_attn(proj, p["diff_bias"][tile], p["lam"], p["subln"], p["lam_init"], tile)
    x2d = _mix_ffn(x2d, ya.reshape(rows, -1), yb.reshape(rows, -1), p["attn_wo_a"], p["attn_wo_b"],
                   p["ffn_g"][0], p["wg"][0], p["wu"][0], p["wd"][0], tm)

    proj = _conv_in(x2d, p["mix_g"][1], p["conv_w"], tm).reshape(bsz, s_len, PROJ1_W)
    yc, yu = _conv_mix(proj, p["scw"], p["dww"], p["dwb"], p["lng"], p["lnb"], ts)
    x2d = _mix_ffn(x2d, yc.reshape(rows, -1), yu.reshape(rows, -1), p["conv_wo_a"], p["conv_wo_b"],
                   p["ffn_g"][1], p["wg"][1], p["wu"][1], p["wd"][1], tm)
    return x2d.reshape(bsz, s_len, D_MODEL)


def _prepare(rel_bias, mix_norm, ffn_norm, w_gate, w_up, w_down, attn_w_in, attn_w_out, a_q_norm, a_k_norm,
             a_sink, b_q_norm, b_k_norm, b_lambda, b_subln, conv_w_in, conv_w_out, short_conv_w, conf_dw_w,
             conf_dw_b, conf_ln_g, conf_ln_b, tiles):
    attn_w, attn_gain, seg = _attn_in_params(attn_w_in[0], a_q_norm[0], a_k_norm[0], b_q_norm[0], b_k_norm[0])
    half = A_Q_HEADS * HEAD_DIM
    return {
        "mix_g": [mix_norm[l][None, :].astype(F32) for l in range(2)],
        "ffn_g": [ffn_norm[l][None, :].astype(F32) for l in range(2)],
        "wg": [w_gate[l].astype(BF16) for l in range(2)],
        "wu": [w_up[l].astype(BF16) for l in range(2)],
        "wd": [w_down[l].astype(BF16) for l in range(2)],
        "attn_w": attn_w, "attn_gain": attn_gain, "seg": seg,
        "attn_wo_a": attn_w_out[0][:half].astype(BF16), "attn_wo_b": attn_w_out[0][half:].astype(BF16),
        "sink": a_sink[0].astype(F32),
        "win_bias": _win_bias(rel_bias),
        "diff_bias": {t: _diff_bias(rel_bias, b_q_norm[0], b_k_norm[0], t) for t in tiles},
        "lam": b_lambda[0].astype(F32),
        "subln": b_subln[0].astype(F32)[:, None],
        "lam_init": 0.8 - 0.6 * math.exp(-0.3 * 0),
        "conv_w": conv_w_in[0].astype(BF16),
        "conv_wo_a": conv_w_out[0][:C_WIDTH].astype(BF16), "conv_wo_b": conv_w_out[0][C_WIDTH:].astype(BF16),
        "scw": short_conv_w[0].astype(F32), "dww": conf_dw_w[0].astype(F32),
        "dwb": conf_dw_b[0][None, :].astype(F32),
        "lng": conf_ln_g[0][None, :].astype(F32), "lnb": conf_ln_b[0][None, :].astype(F32),
    }


def _tiling(s_len):
    return dict(tm=512, tile=min(512, s_len), ts=min(512, s_len))


def kernel(x_prompt, x_sample, rel_bias, mix_norm, ffn_norm, w_gate, w_up, w_down, attn_w_in, attn_w_out,
           a_q_norm, a_k_norm, a_sink, b_q_norm, b_k_norm, b_lambda, b_subln, conv_w_in, conv_w_out,
           short_conv_w, conf_dw_w, conf_dw_b, conf_ln_g, conf_ln_b):
    tp, tsm = _tiling(x_prompt.shape[1]), _tiling(x_sample.shape[1])
    p = _prepare(rel_bias, mix_norm, ffn_norm, w_gate, w_up, w_down, attn_w_in, attn_w_out, a_q_norm,
                 a_k_norm, a_sink, b_q_norm, b_k_norm, b_lambda, b_subln, conv_w_in, conv_w_out,
                 short_conv_w, conf_dw_w, conf_dw_b, conf_ln_g, conf_ln_b, {tp["tile"], tsm["tile"]})
    return (_trunk(x_prompt, p, **tp), _trunk(x_sample, p, **tsm))
```

```python
import functools
import math

import jax
import jax.numpy as jnp
from jax import lax
from jax.experimental import pallas as pl
from jax.experimental.pallas import tpu as pltpu

D_MODEL = 1024
HEAD_DIM = 64
A_Q_HEADS = 8
A_KV_HEADS = 2
WINDOW = 128
BLOCK = 128
B_HEADS = 4
NUM_BUCKETS = 32
MAX_DISTANCE = 128
C_WIDTH = 512
D_WIDTH = 512
SHORT_CONV = 3
CONF_CONV = 31
FFN_HIDDEN = 2816
EPS = 1e-6
NEG = -1e30

LANES = 128
MXU_COLS = 256
VMEM_LIMIT_BYTES = 56 * 1024 * 1024

BF16 = jnp.bfloat16
F32 = jnp.float32

QA_OFF = 0
KA_OFF = 512
VA_OFF = 640
QB_OFF = 768
KB_OFF = 1280
VB_OFF = 1792
PROJ0_W = 2304
NORM_CHUNKS0 = ("all", "all", "low", "all", "all", "all", "all", None, None)
WIN_HEAD_ORDER = (0, 4, 1, 5, 2, 6, 3, 7)

GB_OFF = 0
GX_OFF = 512
GLU_OFF = 1024
PROJ1_W = 1536
HALO = 16


def _cparams(sem):
    return pltpu.CompilerParams(dimension_semantics=sem, vmem_limit_bytes=VMEM_LIMIT_BYTES)


def _const_spec(shape):
    nd = len(shape)
    return pl.BlockSpec(shape, lambda *_: (0,) * nd)


def _rms(x, g):
    ms = jnp.mean(x * x, axis=-1, keepdims=True)
    return x * lax.rsqrt(ms + EPS) * g


def _attn_in_kernel(x_ref, g_ref, w_ref, gain_ref, seg_ref, o_ref):
    h = _rms(x_ref[...], g_ref[...]).astype(BF16)
    seg = seg_ref[...]
    low = lax.broadcasted_iota(jnp.int32, (1, MXU_COLS), 1) < MXU_COLS // 2
    chunk = lambda c: slice(c * MXU_COLS, (c + 1) * MXU_COLS)
    accs = [jnp.dot(h, w_ref[:, chunk(c)], preferred_element_type=F32) for c in range(len(NORM_CHUNKS0))]
    for c, normed in enumerate(NORM_CHUNKS0):
        acc = accs[c]
        if normed is not None:
            sq = acc * acc
            sq_hi = sq.astype(BF16)
            sq_lo = (sq - sq_hi.astype(F32)).astype(BF16)
            ms = (jnp.dot(sq_hi, seg, preferred_element_type=F32)
                  + jnp.dot(sq_lo, seg, preferred_element_type=F32))
            scale = lax.rsqrt(ms + EPS) * gain_ref[:, chunk(c)]
            acc = acc * (scale if normed == "all" else jnp.where(low, scale, 1.0))
        o_ref[:, chunk(c)] = acc.astype(BF16)


def _attn_in(x2d, g, w, gain, seg, tm):
    rows = x2d.shape[0]
    return pl.pallas_call(
        _attn_in_kernel,
        grid=(rows // tm,),
        in_specs=[
            pl.BlockSpec((tm, D_MODEL), lambda i: (i, 0)),
            _const_spec((1, D_MODEL)),
            _const_spec((D_MODEL, PROJ0_W)),
            _const_spec((1, PROJ0_W)),
            _const_spec((MXU_COLS, MXU_COLS)),
        ],
        out_specs=pl.BlockSpec((tm, PROJ0_W), lambda i: (i, 0)),
        out_shape=jax.ShapeDtypeStruct((rows, PROJ0_W), BF16),
        compiler_params=_cparams(("parallel",)),
        name="attn_in",
    )(x2d, g, w, gain, seg)


WIN_COLS = A_Q_HEADS * BLOCK


def _win_attn_kernel(q_ref, kp_ref, kc_ref, kn_ref, vp_ref, vc_ref, vn_ref, bias_ref, sink_ref, o_ref, *, qb):
    step = pl.program_id(1)
    nb = pl.num_programs(1) * qb
    low = lax.broadcasted_iota(jnp.int32, (1, LANES), 1) < HEAD_DIM
    top = lax.broadcasted_iota(jnp.int32, (LANES, 1), 0) < HEAD_DIM
    sink = sink_ref[...]

    def piece(prev_ref, cur_ref, next_ref, i):
        if i == 0:
            return prev_ref[0]
        if i == qb + 1:
            return next_ref[0]
        return cur_ref[0, (i - 1) * BLOCK:i * BLOCK, :]

    def scores(b):
        kcat = jnp.concatenate([piece(kp_ref, kc_ref, kn_ref, b + i) for i in range(3)], axis=0)
        q = q_ref[0, b * BLOCK:(b + 1) * BLOCK, :]
        zero = jnp.zeros((BLOCK, LANES), BF16)
        halves = []
        for j in range(A_Q_HEADS // 2):
            slab = q[:, j * LANES:(j + 1) * LANES]
            halves += [jnp.where(low, slab, zero), jnp.where(low, zero, slab)]
        qcat = jnp.concatenate(halves, axis=0)
        return lax.dot_general(kcat, qcat, _NT, preferred_element_type=F32)

    def weights(b, s_t):
        n = step * qb + b
        edge = jnp.where(n == 0, 0, jnp.where(n == nb - 1, 2, 1))
        s_t = s_t + bias_ref[edge]
        m = jnp.maximum(jnp.max(s_t, axis=0, keepdims=True), sink)
        p = jnp.exp2(s_t - m)
        den = jnp.sum(p, axis=0, keepdims=True) + jnp.exp2(sink - m)
        return p.astype(BF16), den

    def values(b, p, den):
        vcat = jnp.concatenate([piece(vp_ref, vc_ref, vn_ref, b + i) for i in range(3)], axis=0)
        return lax.dot_general(vcat, p, _TN, preferred_element_type=F32) / den

    s_all = [scores(b) for b in range(qb)]
    w_all = [weights(b, s_all[b]) for b in range(qb)]
    o_all = [values(b, *w_all[b]) for b in range(qb)]
    for b in range(qb):
        for j in range(A_Q_HEADS // 2):
            c0 = 2 * j * BLOCK
            slab_t = jnp.where(top, o_all[b][:, c0:c0 + BLOCK], o_all[b][:, c0 + BLOCK:c0 + 2 * BLOCK])
            o_ref[0, b * BLOCK:(b + 1) * BLOCK, j * LANES:(j + 1) * LANES] = slab_t.T.astype(BF16)


def _win_attn(proj, sink_row, bias, qb):
    bsz, s_len, _ = proj.shape
    nb = s_len // BLOCK
    assert nb % qb == 0 and nb >= 2
    kcol, vcol = KA_OFF // LANES, VA_OFF // LANES
    kv_specs = []
    for cb in (kcol, vcol):
        kv_specs += [
            pl.BlockSpec((1, BLOCK, LANES), lambda b, t, cb=cb: (b, jnp.maximum(t * qb - 1, 0), cb)),
            pl.BlockSpec((1, qb * BLOCK, LANES), lambda b, t, cb=cb: (b, t, cb)),
            pl.BlockSpec((1, BLOCK, LANES), lambda b, t, cb=cb: (b, jnp.minimum((t + 1) * qb, nb - 1), cb)),
        ]
    return pl.pallas_call(
        functools.partial(_win_attn_kernel, qb=qb),
        grid=(bsz, nb // qb),
        in_specs=[pl.BlockSpec((1, qb * BLOCK, 512), lambda b, t: (b, t, QA_OFF // 512))] + kv_specs
                 + [pl.BlockSpec(memory_space=pltpu.VMEM), pl.BlockSpec(memory_space=pltpu.VMEM)],
        out_specs=pl.BlockSpec((1, qb * BLOCK, 512), lambda b, t: (b, t, 0)),
        out_shape=jax.ShapeDtypeStruct((bsz, s_len, 512), BF16),
        compiler_params=_cparams(("parallel", "parallel")),
        name="win_attn",
    )(proj, proj, proj, proj, proj, proj, proj, bias, sink_row)


_NT = (((1,), (1,)), ((), ()))
_TN = (((0,), (0,)), ((), ()))


def _split_maps(q, qz_ref):
    lane = lax.broadcasted_iota(jnp.int32, (1, LANES), 1)
    zero = jnp.zeros_like(q)
    qz_ref[0] = jnp.where(lane < HEAD_DIM, q, zero)
    qz_ref[1] = jnp.where(lane < HEAD_DIM, zero, q)


def _diff_finish(acc0, l0, acc1, l1, lam_ref, g_ref, o_ref, lam_init):
    lam = lam_ref[...]
    lam_full = (jnp.exp(jnp.sum(lam[0:1] * lam[1:2], axis=-1, keepdims=True))
                - jnp.exp(jnp.sum(lam[2:3] * lam[3:4], axis=-1, keepdims=True)) + lam_init)
    o = acc0 / l0 - lam_full * (acc1 / l1)
    ms = jnp.mean(o * o, axis=0, keepdims=True)
    y = o * lax.rsqrt(ms + EPS) * g_ref[...] * (1.0 - lam_init)
    o_ref[0] = y.T.astype(BF16)


def _diff_attn_bounded_kernel(q_ref, k_ref, v_ref, bias_ref, lam_ref, g_ref, o_ref,
                              qz_ref, s_ref, l_ref, acc_ref, *, lam_init, tile, unroll):
    qi = pl.program_id(2)
    nk = k_ref.shape[1] // tile
    _split_maps(q_ref[0], qz_ref)
    l_ref[...] = jnp.zeros(l_ref.shape, F32)
    acc_ref[...] = jnp.zeros(acc_ref.shape, F32)

    def rows(ki):
        return pl.ds(pl.multiple_of(ki * tile, tile), tile)

    def produce(ki, buf):
        k = k_ref[0, rows(ki), :]
        for c in range(2):
            s_ref[buf, c] = lax.dot_general(k, qz_ref[c], _NT, preferred_element_type=F32)

    def consume(ki, buf):
        v = v_ref[0, rows(ki), :]
        bias = bias_ref[0, jnp.clip(ki - qi + 2, 0, 4)]
        for c in range(2):
            p = jnp.exp2(s_ref[buf, c] + bias)
            l_ref[c] += jnp.sum(p.reshape(tile // 8, 8, tile), axis=0)
            acc_ref[c] += lax.dot_general(v, p.astype(BF16), _TN, preferred_element_type=F32)

    def group(first, last_group):
        for t in range(unroll):
            if not (last_group and t == unroll - 1):
                produce(first + t + 1, (t + 1) % 2)
            consume(first + t, t % 2)

    def body(j, carry):
        group(j * unroll, False)
        return carry

    produce(0, 0)
    lax.fori_loop(0, nk // unroll - 1, body, 0)
    group(nk - unroll, True)
    l0 = jnp.sum(l_ref[0], axis=0, keepdims=True)
    l1 = jnp.sum(l_ref[1], axis=0, keepdims=True)
    _diff_finish(acc_ref[0], l0, acc_ref[1], l1, lam_ref, g_ref, o_ref, lam_init)


def _diff_attn_bounded(proj, bias_t, lam, g_col, lam_init, tile):
    bsz, s_len, _ = proj.shape
    nk = s_len // tile
    unroll = 4 if nk % 4 == 0 else 2
    assert nk % unroll == 0
    qcol, kcol, vcol = QB_OFF // LANES, KB_OFF // LANES, VB_OFF // LANES
    return pl.pallas_call(
        functools.partial(_diff_attn_bounded_kernel, lam_init=lam_init, tile=tile, unroll=unroll),
        grid=(bsz, B_HEADS, s_len // tile),
        in_specs=[
            pl.BlockSpec((1, tile, LANES), lambda b, h, qi: (b, qi, qcol + h)),
            pl.BlockSpec((1, s_len, LANES), lambda b, h, qi: (b, 0, kcol + h)),
            pl.BlockSpec((1, s_len, LANES), lambda b, h, qi: (b, 0, vcol + h)),
            pl.BlockSpec((1, 5, tile, tile), lambda b, h, qi: (h, 0, 0, 0)),
            _const_spec((4, HEAD_DIM)),
            _const_spec((2 * HEAD_DIM, 1)),
        ],
        out_specs=pl.BlockSpec((1, tile, LANES), lambda b, h, qi: (b, qi, h)),
        out_shape=jax.ShapeDtypeStruct((bsz, s_len, B_HEADS * 2 * HEAD_DIM), BF16),
        scratch_shapes=[
            pltpu.VMEM((2, tile, LANES), BF16),
            pltpu.VMEM((2, 2, tile, tile), F32),
            pltpu.VMEM((2, 8, tile), F32),
            pltpu.VMEM((2, 2 * HEAD_DIM, tile), F32),
        ],
        compiler_params=_cparams(("parallel", "parallel", "parallel")),
        name="diff_attn",
    )(proj, proj, proj, bias_t, lam, g_col)


def _diff_attn_online_kernel(far_ref, q_ref, k_ref, v_ref, bias_ref, lam_ref, g_ref, o_ref,
                             qz_ref, m_ref, l_ref, acc_ref, *, lam_init):
    h = pl.program_id(1)
    qi = pl.program_id(2)
    ki = pl.program_id(3)
    nk = pl.num_programs(3)

    @pl.when(ki == 0)
    def _init():
        _split_maps(q_ref[0], qz_ref)
        m_ref[...] = jnp.full(m_ref.shape, NEG, F32)
        l_ref[...] = jnp.zeros(l_ref.shape, F32)
        acc_ref[...] = jnp.zeros(acc_ref.shape, F32)

    delta = ki - qi

    def step(c, bias_tile, bias_row):
        s_t = lax.dot_general(k_ref[0], qz_ref[c], _NT, preferred_element_type=F32)
        if bias_tile is not None:
            s_t = s_t + bias_tile
        m_old = m_ref[c]
        m_new = jnp.maximum(m_old, jnp.max(s_t, axis=0, keepdims=True) + bias_row)
        alpha = jnp.exp2(m_old - m_new)
        p = jnp.exp2(s_t - (m_new - bias_row))
        l_ref[c] = alpha * l_ref[c] + jnp.sum(p, axis=0, keepdims=True)
        pv = lax.dot_general(v_ref[0], p.astype(BF16), _TN, preferred_element_type=F32)
        acc_ref[c] = alpha * acc_ref[c] + pv
        m_ref[c] = m_new

    @pl.when(jnp.abs(delta) <= 1)
    def _near():
        bias_tile = bias_ref[0, delta + 1]
        for c in range(2):
            step(c, bias_tile, 0.0)

    @pl.when(jnp.abs(delta) > 1)
    def _far():
        side = jnp.where(delta < 0, far_ref[h, 0], far_ref[h, 1])
        for c in range(2):
            step(c, None, side)

    @pl.when(ki == nk - 1)
    def _finish():
        _diff_finish(acc_ref[0], l_ref[0], acc_ref[1], l_ref[1], lam_ref, g_ref, o_ref, lam_init)


def _diff_attn_online(proj, far, bias_t, lam, g_col, lam_init, tile):
    bsz, s_len, _ = proj.shape
    nt = s_len // tile
    qcol, kcol, vcol = QB_OFF // LANES, KB_OFF // LANES, VB_OFF // LANES
    return pl.pallas_call(
        functools.partial(_diff_attn_online_kernel, lam_init=lam_init),
        grid=(bsz, B_HEADS, nt, nt),
        in_specs=[
            pl.BlockSpec(memory_space=pltpu.SMEM),
            pl.BlockSpec((1, tile, LANES), lambda b, h, qi, ki: (b, qi, qcol + h)),
            pl.BlockSpec((1, tile, LANES), lambda b, h, qi, ki: (b, ki, kcol + h)),
            pl.BlockSpec((1, tile, LANES), lambda b, h, qi, ki: (b, ki, vcol + h)),
            pl.BlockSpec((1, 3, tile, tile), lambda b, h, qi, ki: (h, 0, 0, 0)),
            _const_spec((4, HEAD_DIM)),
            _const_spec((2 * HEAD_DIM, 1)),
        ],
        out_specs=pl.BlockSpec((1, tile, LANES), lambda b, h, qi, ki: (b, qi, h)),
        out_shape=jax.ShapeDtypeStruct((bsz, s_len, B_HEADS * 2 * HEAD_DIM), BF16),
        scratch_shapes=[
            pltpu.VMEM((2, tile, LANES), BF16),
            pltpu.VMEM((2, 1, tile), F32),
            pltpu.VMEM((2, 1, tile), F32),
            pltpu.VMEM((2, 2 * HEAD_DIM, tile), F32),
        ],
        compiler_params=_cparams(("parallel", "parallel", "parallel", "arbitrary")),
        name="diff_attn_online",
    )(far, proj, proj, proj, bias_t, lam, g_col)


def _diff_attn(proj, db, lam, g_col, lam_init, tile):
    bounded = lambda: _diff_attn_bounded(proj, db["shifted"], lam, g_col, lam_init, tile)
    online = lambda: _diff_attn_online(proj, db["far"], db["near"], lam, g_col, lam_init, tile)
    return lax.cond(db["bounded_ok"], bounded, online)


FFN_CHUNKS = ((0, 1536), (1536, FFN_HIDDEN))


def _mix_ffn_kernel(x_ref, a_ref, b_ref, wa_ref, wb_ref, g_ref, wg_ref, wu_ref, wd_ref, o_ref):
    mix = (jnp.dot(a_ref[...], wa_ref[...], preferred_element_type=F32)
           + jnp.dot(b_ref[...], wb_ref[...], preferred_element_type=F32))
    x1 = x_ref[...] + mix
    h = _rms(x1, g_ref[...]).astype(BF16)
    gates = [jnp.dot(h, wg_ref[:, lo:hi], preferred_element_type=F32) for lo, hi in FFN_CHUNKS]
    ups = [jnp.dot(h, wu_ref[:, lo:hi], preferred_element_type=F32) for lo, hi in FFN_CHUNKS]
    down = None
    for (lo, hi), gate, up in zip(FFN_CHUNKS, gates, ups):
        act = (gate * jax.nn.sigmoid(gate) * up).astype(BF16)
        part = jnp.dot(act, wd_ref[lo:hi, :], preferred_element_type=F32)
        down = part if down is None else down + part
    o_ref[...] = x1 + down


def _mix_ffn(x2d, a2d, b2d, wa, wb, g, wg, wu, wd, tm):
    rows = x2d.shape[0]
    half = a2d.shape[1]
    row_spec = lambda w: pl.BlockSpec((tm, w), lambda i: (i, 0))
    single = lambda shape: pl.BlockSpec(memory_space=pltpu.VMEM)
    return pl.pallas_call(
        _mix_ffn_kernel,
        grid=(rows // tm,),
        in_specs=[
            row_spec(D_MODEL), row_spec(half), row_spec(half),
            single((half, D_MODEL)), single((half, D_MODEL)),
            single((1, D_MODEL)),
            single((D_MODEL, FFN_HIDDEN)), single((D_MODEL, FFN_HIDDEN)), single((FFN_HIDDEN, D_MODEL)),
        ],
        out_specs=row_spec(D_MODEL),
        out_shape=jax.ShapeDtypeStruct((rows, D_MODEL), F32),
        compiler_params=_cparams(("parallel",)),
        name="mix_ffn",
    )(x2d, a2d, b2d, wa, wb, g, wg, wu, wd)


def _conv_in_kernel(x_ref, g_ref, w_ref, o_ref):
    h = _rms(x_ref[...], g_ref[...]).astype(BF16)
    c = C_WIDTH
    gb, gc, xc, a, gate = [jnp.dot(h, w_ref[:, i * c:(i + 1) * c], preferred_element_type=F32) for i in range(5)]
    o_ref[:, GB_OFF:GB_OFF + c] = gb.astype(BF16)
    o_ref[:, GX_OFF:GX_OFF + c] = (gc * xc).astype(BF16)
    o_ref[:, GLU_OFF:GLU_OFF + D_WIDTH] = (a * jax.nn.sigmoid(gate)).astype(BF16)


def _conv_in(x2d, g, w, tm):
    rows = x2d.shape[0]
    conv_in_w = w.shape[1]
    return pl.pallas_call(
        _conv_in_kernel,
        grid=(rows // tm,),
        in_specs=[
            pl.BlockSpec((tm, D_MODEL), lambda i: (i, 0)),
            _const_spec((1, D_MODEL)),
            _const_spec((D_MODEL, conv_in_w)),
        ],
        out_specs=pl.BlockSpec((tm, PROJ1_W), lambda i: (i, 0)),
        out_shape=jax.ShapeDtypeStruct((rows, PROJ1_W), BF16),
        compiler_params=_cparams(("parallel",)),
        name="conv_in",
    )(x2d, g, w)


CONV_CHUNK = 32
SUBLANES = 8


def _conv_mix_kernel(gb_ref, gx_ref, gxp_ref, gxn_ref, u_ref, up_ref, un_ref,
                     scw_ref, dww_ref, dwb_ref, lng_ref, lnb_ref, yc_ref, yu_ref, xs_ref, u_ref_f32, *, ts):
    t = pl.program_id(1)
    nt = pl.num_programs(1)
    has_prev = (t > 0).astype(F32)
    has_next = (t < nt - 1).astype(F32)
    n_shifted = ts + 2 * HALO - SUBLANES

    def taps(width):
        return [(j,) + divmod(HALO - width // 2 + j, SUBLANES)[::-1] for j in range(width)]

    def fill(cur, prev, nxt, width):
        xs_ref[0, 0:HALO, :] = prev[0].astype(F32) * has_prev
        xs_ref[0, HALO:HALO + ts, :] = cur[0].astype(F32)
        xs_ref[0, HALO + ts:2 * HALO + ts, :] = nxt[0].astype(F32) * has_next
        for r in sorted({r for _, r, _ in taps(width)} - {0}):
            xs_ref[r, 0:n_shifted, :] = xs_ref[0, r:r + n_shifted, :]

    def conv(w_ref, width, s0):
        acc = None
        for j, r, q in taps(width):
            window = xs_ref[r, pl.ds(s0 + SUBLANES * q, CONV_CHUNK), :]
            term = window.reshape(CONV_CHUNK // SUBLANES, SUBLANES, -1) * w_ref[j]
            acc = term if acc is None else acc + term
        return acc.reshape(CONV_CHUNK, -1)

    def chunks(body):
        def step(c, carry):
            body(pl.multiple_of(c * CONV_CHUNK, CONV_CHUNK))
            return carry
        lax.fori_loop(0, ts // CONV_CHUNK, step, 0)

    fill(gx_ref, gxp_ref, gxn_ref, SHORT_CONV)

    def short(s0):
        out = pl.ds(s0, CONV_CHUNK)
        yc_ref[0, out, :] = (gb_ref[0, out, :].astype(F32) * conv(scw_ref, SHORT_CONV, s0)).astype(BF16)
    chunks(short)

    fill(u_ref, up_ref, un_ref, CONF_CONV)

    def conf(s0):
        u_ref_f32[pl.ds(s0, CONV_CHUNK), :] = conv(dww_ref, CONF_CONV, s0) + dwb_ref[...]
    chunks(conf)

    u = u_ref_f32[...]
    mu = jnp.mean(u, axis=-1, keepdims=True)
    uc = u - mu
    var = jnp.mean(uc * uc, axis=-1, keepdims=True)
    y = uc * lax.rsqrt(var + EPS) * lng_ref[...] + lnb_ref[...]
    yu_ref[0] = (y * jax.nn.sigmoid(y)).astype(BF16)


def _conv_mix(proj, scw, dww, dwb, lng, lnb, ts):
    bsz, s_len, _ = proj.shape
    nt = s_len // ts
    r = ts // HALO
    nh = s_len // HALO
    cur = lambda cb: pl.BlockSpec((1, ts, 512), lambda b, t: (b, t, cb))
    prev = lambda cb: pl.BlockSpec((1, HALO, 512), lambda b, t: (b, jnp.maximum(t * r - 1, 0), cb))
    nxt = lambda cb: pl.BlockSpec((1, HALO, 512), lambda b, t: (b, jnp.minimum((t + 1) * r, nh - 1), cb))
    gb, gx, gl = GB_OFF // 512, GX_OFF // 512, GLU_OFF // 512
    out_spec = pl.BlockSpec((1, ts, 512), lambda b, t: (b, t, 0))
    return pl.pallas_call(
        functools.partial(_conv_mix_kernel, ts=ts),
        grid=(bsz, nt),
        in_specs=[cur(gb), cur(gx), prev(gx), nxt(gx), cur(gl), prev(gl), nxt(gl),
                  _const_spec((SHORT_CONV, SUBLANES, C_WIDTH)), _const_spec((CONF_CONV, SUBLANES, D_WIDTH)),
                  _const_spec((1, D_WIDTH)), _const_spec((1, D_WIDTH)), _const_spec((1, D_WIDTH))],
        out_specs=[out_spec, out_spec],
        out_shape=[jax.ShapeDtypeStruct((bsz, s_len, C_WIDTH), BF16),
                   jax.ShapeDtypeStruct((bsz, s_len, D_WIDTH), BF16)],
        scratch_shapes=[pltpu.VMEM((SUBLANES, ts + 2 * HALO, 512), F32), pltpu.VMEM((ts, D_WIDTH), F32)],
        compiler_params=_cparams(("parallel", "parallel")),
        name="conv_mix",
    )(proj, proj, proj, proj, proj, proj, proj, scw, dww, dwb, lng, lnb)


def _rel_bucket(rel):
    half = NUM_BUCKETS // 2
    max_exact = half // 2
    n = jnp.abs(rel)
    large = max_exact + (jnp.log(jnp.maximum(n, 1).astype(F32) / max_exact)
                         / math.log(MAX_DISTANCE / max_exact) * (half - max_exact)).astype(jnp.int32)
    large = jnp.minimum(large, half - 1)
    return jnp.where(rel > 0, half, 0) + jnp.where(n < max_exact, n, large)


def _toeplitz(u, rows, cols):
    length = u.shape[-1]
    flat = jnp.tile(u, (1,) * (u.ndim - 1) + (rows,))[..., :rows * (length - 1)]
    return flat.reshape(u.shape[:-1] + (rows, length - 1))[..., :cols]


LOG2E = math.log2(math.e)


def _win_bias(rel_bias):
    length = 4 * BLOCK
    n = jnp.arange(length)
    rel = jnp.where(n < BLOCK, -n, length - n) - BLOCK
    vec = rel_bias[:, :A_Q_HEADS][_rel_bucket(rel)].astype(F32).T * LOG2E
    vec = jnp.where((jnp.abs(rel) <= WINDOW)[None], vec, NEG)
    base = _toeplitz(vec, 3 * BLOCK, BLOCK)[jnp.array(WIN_HEAD_ORDER)]
    mid = base.transpose(1, 0, 2).reshape(3 * BLOCK, A_Q_HEADS * BLOCK)
    key_block = (jnp.arange(3 * BLOCK) // BLOCK)[:, None]
    return jnp.stack([jnp.where(key_block == 0, NEG, mid), mid, jnp.where(key_block == 2, NEG, mid)], axis=0)
MIN_EXP2_ARG = -120.0


def _diff_bias(rel_bias, b_qn, b_kn, tile):
    table = rel_bias[:, A_Q_HEADS:].astype(F32) * LOG2E
    length = 2 * tile
    n = jnp.arange(length)
    k_minus_q = jnp.where(n < tile, -n, length - n)
    rel = jnp.clip(k_minus_q[None, :] + jnp.array([-tile, 0, tile])[:, None], 1 - length, length - 1)
    near = _toeplitz(table[_rel_bucket(rel)].transpose(2, 0, 1), tile, tile)
    far = table[_rel_bucket(jnp.array([-length, length]))].T
    s_max = 1.02 * LOG2E * math.sqrt(HEAD_DIM) * jnp.max(jnp.abs(b_qn * b_kn))
    b_max, b_min = jnp.max(table, axis=0), jnp.min(table, axis=0)
    bound = s_max + b_max
    const = lambda col: jnp.broadcast_to((far[:, col] - bound)[:, None, None, None], (B_HEADS, 1, tile, tile))
    shifted = jnp.concatenate([const(0), near - bound[:, None, None, None], const(1)], axis=1)
    bounded_ok = jnp.all(-2.0 * s_max - (b_max - b_min) > MIN_EXP2_ARG)
    return {"near": near, "far": far, "shifted": shifted, "bounded_ok": bounded_ok}


def _by_win_head(m, axis):
    take = lambda h: lax.slice_in_dim(m, h * HEAD_DIM, (h + 1) * HEAD_DIM, axis=axis)
    return jnp.concatenate([take(h) for h in WIN_HEAD_ORDER], axis=axis)


def _attn_in_params(w_in, a_qn, a_kn, b_qn, b_kn):
    d = HEAD_DIM
    k0 = A_Q_HEADS * d
    w = jnp.concatenate([_by_win_head(w_in[:, :k0], 1), w_in[:, k0:]], axis=1).astype(BF16)
    scale = HEAD_DIM ** -0.5 * LOG2E
    ones = lambda n: jnp.ones((n,), F32)
    gain = jnp.concatenate([
        jnp.tile(a_qn, A_Q_HEADS) * scale, jnp.tile(a_kn, A_KV_HEADS), ones(A_KV_HEADS * d),
        jnp.tile(b_qn, 2 * B_HEADS) * scale, jnp.tile(b_kn, 2 * B_HEADS), ones(2 * B_HEADS * d),
    ]).astype(F32)[None, :]
    seg_id = jnp.arange(MXU_COLS) // d
    seg = jnp.where(seg_id[:, None] == seg_id[None, :], 1.0 / d, 0.0).astype(BF16)
    return w, gain, seg


def _trunk(x, p, *, tm, tile, ts, qb):
    bsz, s_len, _ = x.shape
    rows = bsz * s_len
    x2d = x.reshape(rows, D_MODEL)

    proj = _attn_in(x2d, p["mix_g"][0], p["attn_w"], p["attn_gain"], p["seg"], tm)
    proj = proj.reshape(bsz, s_len, PROJ0_W)
    ya = _win_attn(proj, p["sink_row"], p["win_bias"], qb)
    yb = _diff_attn(proj, p["diff_bias"][tile], p["lam"], p["subln"], p["lam_init"], tile)
    x2d = _mix_ffn(x2d, ya.reshape(rows, -1), yb.reshape(rows, -1), p["attn_wo_a"], p["attn_wo_b"],
                   p["ffn_g"][0], p["wg"][0], p["wu"][0], p["wd"][0], tm)

    proj = _conv_in(x2d, p["mix_g"][1], p["conv_w"], tm).reshape(bsz, s_len, PROJ1_W)
    yc, yu = _conv_mix(proj, p["scw"], p["dww"], p["dwb"], p["lng"], p["lnb"], ts)
    x2d = _mix_ffn(x2d, yc.reshape(rows, -1), yu.reshape(rows, -1), p["conv_wo_a"], p["conv_wo_b"],
                   p["ffn_g"][1], p["wg"][1], p["wu"][1], p["wd"][1], tm)
    return x2d.reshape(bsz, s_len, D_MODEL)


def _prepare(rel_bias, mix_norm, ffn_norm, w_gate, w_up, w_down, attn_w_in, attn_w_out, a_q_norm, a_k_norm,
             a_sink, b_q_norm, b_k_norm, b_lambda, b_subln, conv_w_in, conv_w_out, short_conv_w, conf_dw_w,
             conf_dw_b, conf_ln_g, conf_ln_b, tiles):
    attn_w, attn_gain, seg = _attn_in_params(attn_w_in[0], a_q_norm[0], a_k_norm[0], b_q_norm[0], b_k_norm[0])
    half = A_Q_HEADS * HEAD_DIM
    return {
        "mix_g": [mix_norm[l][None, :].astype(F32) for l in range(2)],
        "ffn_g": [ffn_norm[l][None, :].astype(F32) for l in range(2)],
        "wg": [w_gate[l].astype(BF16) for l in range(2)],
        "wu": [w_up[l].astype(BF16) for l in range(2)],
        "wd": [w_down[l].astype(BF16) for l in range(2)],
        "attn_w": attn_w, "attn_gain": attn_gain, "seg": seg,
        "attn_wo_a": _by_win_head(attn_w_out[0][:half], 0).astype(BF16),
        "attn_wo_b": attn_w_out[0][half:].astype(BF16),
        "sink_row": jnp.repeat(a_sink[0].astype(F32)[jnp.array(WIN_HEAD_ORDER)] * LOG2E, BLOCK)[None, :],
        "win_bias": _win_bias(rel_bias),
        "diff_bias": {t: _diff_bias(rel_bias, b_q_norm[0], b_k_norm[0], t) for t in tiles},
        "lam": b_lambda[0].astype(F32),
        "subln": b_subln[0].astype(F32)[:, None],
        "lam_init": 0.8 - 0.6 * math.exp(-0.3 * 0),
        "conv_w": conv_w_in[0].astype(BF16),
        "conv_wo_a": conv_w_out[0][:C_WIDTH].astype(BF16), "conv_wo_b": conv_w_out[0][C_WIDTH:].astype(BF16),
        "scw": jnp.broadcast_to(short_conv_w[0].astype(F32)[:, None, :], (SHORT_CONV, SUBLANES, C_WIDTH)),
        "dww": jnp.broadcast_to(conf_dw_w[0].astype(F32)[:, None, :], (CONF_CONV, SUBLANES, D_WIDTH)),
        "dwb": conf_dw_b[0][None, :].astype(F32),
        "lng": conf_ln_g[0][None, :].astype(F32), "lnb": conf_ln_b[0][None, :].astype(F32),
    }


def _tiling(s_len):
    return dict(tm=512, tile=min(512, s_len), ts=min(512, s_len), qb=4)


def kernel(x_prompt, x_sample, rel_bias, mix_norm, ffn_norm, w_gate, w_up, w_down, attn_w_in, attn_w_out,
           a_q_norm, a_k_norm, a_sink, b_q_norm, b_k_norm, b_lambda, b_subln, conv_w_in, conv_w_out,
           short_conv_w, conf_dw_w, conf_dw_b, conf_ln_g, conf_ln_b):
    tp, tsm = _tiling(x_prompt.shape[1]), _tiling(x_sample.shape[1])
    p = _prepare(rel_bias, mix_norm, ffn_norm, w_gate, w_up, w_down, attn_w_in, attn_w_out, a_q_norm,
                 a_k_norm, a_sink, b_q_norm, b_k_norm, b_lambda, b_subln, conv_w_in, conv_w_out,
                 short_conv_w, conf_dw_w, conf_dw_b, conf_ln_g, conf_ln_b, {tp["tile"], tsm["tile"]})
    return (_trunk(x_prompt, p, **tp), _trunk(x_sample, p, **tsm))
```

```python
import functools
import math

import jax
import jax.numpy as jnp
from jax import lax
from jax.experimental import pallas as pl
from jax.experimental.pallas import tpu as pltpu

D_MODEL = 1024
HEAD_DIM = 64
A_Q_HEADS = 8
A_KV_HEADS = 2
WINDOW = 128
BLOCK = 128
B_HEADS = 4
NUM_BUCKETS = 32
MAX_DISTANCE = 128
C_WIDTH = 512
D_WIDTH = 512
SHORT_CONV = 3
CONF_CONV = 31
FFN_HIDDEN = 2816
EPS = 1e-6
NEG = -1e30

LANES = 128
MXU_COLS = 256
VMEM_LIMIT_BYTES = 56 * 1024 * 1024

BF16 = jnp.bfloat16
F32 = jnp.float32

PROJ0_W = 2304
NORM_CHUNKS0 = ("all", "all", "low", "all", "all", "all", "all", None, None)
WIN_HEAD_ORDER = (0, 4, 1, 5, 2, 6, 3, 7)

HALO = 16


def _cparams(sem):
    return pltpu.CompilerParams(dimension_semantics=sem, vmem_limit_bytes=VMEM_LIMIT_BYTES)


def _const_spec(shape):
    nd = len(shape)
    return pl.BlockSpec(shape, lambda *_: (0,) * nd)


def _rms(x, g):
    ms = jnp.mean(x * x, axis=-1, keepdims=True)
    return x * lax.rsqrt(ms + EPS) * g


def _attn_in_kernel(x_ref, g_ref, w_ref, gain_ref, seg_ref, qa_ref, ka_ref, va_ref, qb_ref, kb_ref, vb_ref):
    half = MXU_COLS // 2

    def store(c, y):
        if c < 2:
            qa_ref[:, c * MXU_COLS:(c + 1) * MXU_COLS] = y
        elif c == 2:
            ka_ref[...] = y[:, :half]
            va_ref[...] = y[:, half:]
        else:
            ref, first = ((qb_ref, 3), (kb_ref, 5), (vb_ref, 7))[(c - 3) // 2]
            ref[0, 2 * (c - first)] = y[:, :half]
            ref[0, 2 * (c - first) + 1] = y[:, half:]

    h = _rms(x_ref[...], g_ref[...]).astype(BF16)
    seg = seg_ref[...]
    low = lax.broadcasted_iota(jnp.int32, (1, MXU_COLS), 1) < MXU_COLS // 2
    chunk = lambda c: slice(c * MXU_COLS, (c + 1) * MXU_COLS)
    accs = [jnp.dot(h, w_ref[:, chunk(c)], preferred_element_type=F32) for c in range(len(NORM_CHUNKS0))]
    for c, normed in enumerate(NORM_CHUNKS0):
        acc = accs[c]
        if normed is not None:
            sq = acc * acc
            sq_hi = sq.astype(BF16)
            sq_lo = (sq - sq_hi.astype(F32)).astype(BF16)
            ms = (jnp.dot(sq_hi, seg, preferred_element_type=F32)
                  + jnp.dot(sq_lo, seg, preferred_element_type=F32))
            scale = lax.rsqrt(ms + EPS) * gain_ref[:, chunk(c)]
            acc = acc * (scale if normed == "all" else jnp.where(low, scale, 1.0))
        store(c, acc.astype(BF16))


def _attn_in(x, g, w, gain, seg, tm):
    bsz, s_len, _ = x.shape
    spt = s_len // tm
    flat = lambda width: pl.BlockSpec((1, tm, width), lambda i: (i // spt, i % spt, 0))
    heads = pl.BlockSpec((1, B_HEADS, tm, LANES), lambda i: (i // spt, 0, i % spt, 0))
    sds = lambda *shape: jax.ShapeDtypeStruct(shape, BF16)
    head_major = sds(bsz, B_HEADS, s_len, LANES)

    def body(x_ref, g_ref, w_ref, gain_ref, seg_ref, qa_ref, ka_ref, va_ref, qb_ref, kb_ref, vb_ref):
        _attn_in_kernel(x_ref.at[0], g_ref, w_ref, gain_ref, seg_ref,
                        qa_ref.at[0], ka_ref.at[0], va_ref.at[0], qb_ref, kb_ref, vb_ref)

    return pl.pallas_call(
        body,
        grid=(bsz * spt,),
        in_specs=[
            flat(D_MODEL),
            _const_spec((1, D_MODEL)),
            _const_spec((D_MODEL, PROJ0_W)),
            _const_spec((1, PROJ0_W)),
            _const_spec((MXU_COLS, MXU_COLS)),
        ],
        out_specs=[flat(512), flat(LANES), flat(LANES), heads, heads, heads],
        out_shape=[sds(bsz, s_len, 512), sds(bsz, s_len, LANES), sds(bsz, s_len, LANES),
                   head_major, head_major, head_major],
        compiler_params=_cparams(("parallel",)),
        name="attn_in",
    )(x, g, w, gain, seg)


WIN_COLS = A_Q_HEADS * BLOCK


def _win_attn_kernel(q_ref, kp_ref, kc_ref, kn_ref, vp_ref, vc_ref, vn_ref, bias_ref, sink_ref, o_ref, *, qb):
    step = pl.program_id(1)
    nb = pl.num_programs(1) * qb
    low = lax.broadcasted_iota(jnp.int32, (1, LANES), 1) < HEAD_DIM
    top = lax.broadcasted_iota(jnp.int32, (LANES, 1), 0) < HEAD_DIM
    sink = sink_ref[...]

    def piece(prev_ref, cur_ref, next_ref, i):
        if i == 0:
            return prev_ref[0]
        if i == qb + 1:
            return next_ref[0]
        return cur_ref[0, (i - 1) * BLOCK:i * BLOCK, :]

    def scores(b):
        kcat = jnp.concatenate([piece(kp_ref, kc_ref, kn_ref, b + i) for i in range(3)], axis=0)
        q = q_ref[0, b * BLOCK:(b + 1) * BLOCK, :]
        zero = jnp.zeros((BLOCK, LANES), BF16)
        halves = []
        for j in range(A_Q_HEADS // 2):
            slab = q[:, j * LANES:(j + 1) * LANES]
            halves += [jnp.where(low, slab, zero), jnp.where(low, zero, slab)]
        qcat = jnp.concatenate(halves, axis=0)
        return lax.dot_general(kcat, qcat, _NT, preferred_element_type=F32)

    def weights(b, s_t):
        n = step * qb + b
        edge = jnp.where(n == 0, 0, jnp.where(n == nb - 1, 2, 1))
        s_t = s_t + bias_ref[edge]
        m = jnp.maximum(jnp.max(s_t, axis=0, keepdims=True), sink)
        p = jnp.exp2(s_t - m)
        den = jnp.sum(p, axis=0, keepdims=True) + jnp.exp2(sink - m)
        return p.astype(BF16), den

    def values(b, p, den):
        vcat = jnp.concatenate([piece(vp_ref, vc_ref, vn_ref, b + i) for i in range(3)], axis=0)
        return lax.dot_general(vcat, p, _TN, preferred_element_type=F32) / den

    s_all = [scores(b) for b in range(qb)]
    w_all = [weights(b, s_all[b]) for b in range(qb)]
    o_all = [values(b, *w_all[b]) for b in range(qb)]
    for b in range(qb):
        for j in range(A_Q_HEADS // 2):
            c0 = 2 * j * BLOCK
            slab_t = jnp.where(top, o_all[b][:, c0:c0 + BLOCK], o_all[b][:, c0 + BLOCK:c0 + 2 * BLOCK])
            o_ref[0, b * BLOCK:(b + 1) * BLOCK, j * LANES:(j + 1) * LANES] = slab_t.T.astype(BF16)


def _win_attn(qa, ka, va, sink_row, bias, qb):
    bsz, s_len, _ = qa.shape
    nb = s_len // BLOCK
    assert nb % qb == 0 and nb >= 2
    kv_specs = [
        pl.BlockSpec((1, BLOCK, LANES), lambda b, t: (b, jnp.maximum(t * qb - 1, 0), 0)),
        pl.BlockSpec((1, qb * BLOCK, LANES), lambda b, t: (b, t, 0)),
        pl.BlockSpec((1, BLOCK, LANES), lambda b, t: (b, jnp.minimum((t + 1) * qb, nb - 1), 0)),
    ]
    return pl.pallas_call(
        functools.partial(_win_attn_kernel, qb=qb),
        grid=(bsz, nb // qb),
        in_specs=[pl.BlockSpec((1, qb * BLOCK, 512), lambda b, t: (b, t, 0))] + kv_specs + kv_specs
                 + [pl.BlockSpec(memory_space=pltpu.VMEM), pl.BlockSpec(memory_space=pltpu.VMEM)],
        out_specs=pl.BlockSpec((1, qb * BLOCK, 512), lambda b, t: (b, t, 0)),
        out_shape=jax.ShapeDtypeStruct((bsz, s_len, 512), BF16),
        compiler_params=_cparams(("parallel", "parallel")),
        name="win_attn",
    )(qa, ka, ka, ka, va, va, va, bias, sink_row)


_NT = (((1,), (1,)), ((), ()))
_TN = (((0,), (0,)), ((), ()))


def _split_maps(q, qz_ref):
    lane = lax.broadcasted_iota(jnp.int32, (1, LANES), 1)
    zero = jnp.zeros_like(q)
    qz_ref[0] = jnp.where(lane < HEAD_DIM, q, zero)
    qz_ref[1] = jnp.where(lane < HEAD_DIM, zero, q)


def _diff_finish(acc0, l0, acc1, l1, lam_ref, g_ref, o_ref, lam_init):
    lam = lam_ref[...]
    lam_full = (jnp.exp(jnp.sum(lam[0:1] * lam[1:2], axis=-1, keepdims=True))
                - jnp.exp(jnp.sum(lam[2:3] * lam[3:4], axis=-1, keepdims=True)) + lam_init)
    o = acc0 / l0 - lam_full * (acc1 / l1)
    ms = jnp.mean(o * o, axis=0, keepdims=True)
    y = o * lax.rsqrt(ms + EPS) * g_ref[...] * (1.0 - lam_init)
    o_ref[0] = y.T.astype(BF16)


def _diff_attn_bounded_kernel(q_ref, k_ref, v_ref, bias_ref, lam_ref, g_ref, o_ref,
                              qz_ref, s_ref, l_ref, acc_ref, *, lam_init, tile, unroll):
    qi = pl.program_id(2)
    nk = k_ref.shape[1] // tile
    _split_maps(q_ref[0], qz_ref)
    l_ref[...] = jnp.zeros(l_ref.shape, F32)
    acc_ref[...] = jnp.zeros(acc_ref.shape, F32)

    def rows(ki):
        return pl.ds(pl.multiple_of(ki * tile, tile), tile)

    def produce(ki, buf):
        k = k_ref[0, rows(ki), :]
        for c in range(2):
            s_ref[buf, c] = lax.dot_general(k, qz_ref[c], _NT, preferred_element_type=F32)

    def consume(ki, buf):
        v = v_ref[0, rows(ki), :]
        bias = bias_ref[0, jnp.clip(ki - qi + 2, 0, 4)]
        for c in range(2):
            p = jnp.exp2(s_ref[buf, c] + bias)
            l_ref[c] += jnp.sum(p.reshape(tile // 8, 8, tile), axis=0)
            acc_ref[c] += lax.dot_general(v, p.astype(BF16), _TN, preferred_element_type=F32)

    def group(first, last_group):
        for t in range(unroll):
            if not (last_group and t == unroll - 1):
                produce(first + t + 1, (t + 1) % 2)
            consume(first + t, t % 2)

    def body(j, carry):
        group(j * unroll, False)
        return carry

    produce(0, 0)
    lax.fori_loop(0, nk // unroll - 1, body, 0)
    group(nk - unroll, True)
    l0 = jnp.sum(l_ref[0], axis=0, keepdims=True)
    l1 = jnp.sum(l_ref[1], axis=0, keepdims=True)
    _diff_finish(acc_ref[0], l0, acc_ref[1], l1, lam_ref, g_ref, o_ref, lam_init)


def _diff_attn_bounded(qb, kb, vb, bias_t, lam, g_col, lam_init, tile):
    bsz, _, s_len, _ = qb.shape
    nk = s_len // tile
    unroll = 4 if nk % 4 == 0 else 2
    assert nk % unroll == 0
    kernel = functools.partial(_diff_attn_bounded_kernel, lam_init=lam_init, tile=tile, unroll=unroll)

    def body(q_ref, k_ref, v_ref, bias_ref, lam_ref, g_ref, o_ref, *scratch):
        kernel(q_ref.at[0], k_ref.at[0], v_ref.at[0], bias_ref, lam_ref, g_ref, o_ref.at[0], *scratch)

    return pl.pallas_call(
        body,
        grid=(bsz, B_HEADS, s_len // tile),
        in_specs=[
            pl.BlockSpec((1, 1, tile, LANES), lambda b, h, qi: (b, h, qi, 0)),
            pl.BlockSpec((1, 1, s_len, LANES), lambda b, h, qi: (b, h, 0, 0)),
            pl.BlockSpec((1, 1, s_len, LANES), lambda b, h, qi: (b, h, 0, 0)),
            pl.BlockSpec((1, 5, tile, tile), lambda b, h, qi: (h, 0, 0, 0)),
            _const_spec((4, HEAD_DIM)),
            _const_spec((2 * HEAD_DIM, 1)),
        ],
        out_specs=pl.BlockSpec((1, 1, tile, LANES), lambda b, h, qi: (b, h, qi, 0)),
        out_shape=jax.ShapeDtypeStruct((bsz, B_HEADS, s_len, LANES), BF16),
        scratch_shapes=[
            pltpu.VMEM((2, tile, LANES), BF16),
            pltpu.VMEM((2, 2, tile, tile), F32),
            pltpu.VMEM((2, 8, tile), F32),
            pltpu.VMEM((2, 2 * HEAD_DIM, tile), F32),
        ],
        compiler_params=_cparams(("parallel", "parallel", "parallel")),
        name="diff_attn",
    )(qb, kb, vb, bias_t, lam, g_col)


def _diff_attn_online_kernel(far_ref, q_ref, k_ref, v_ref, bias_ref, lam_ref, g_ref, o_ref,
                             qz_ref, m_ref, l_ref, acc_ref, *, lam_init):
    h = pl.program_id(1)
    qi = pl.program_id(2)
    ki = pl.program_id(3)
    nk = pl.num_programs(3)

    @pl.when(ki == 0)
    def _init():
        _split_maps(q_ref[0], qz_ref)
        m_ref[...] = jnp.full(m_ref.shape, NEG, F32)
        l_ref[...] = jnp.zeros(l_ref.shape, F32)
        acc_ref[...] = jnp.zeros(acc_ref.shape, F32)

    delta = ki - qi

    def step(c, bias_tile, bias_row):
        s_t = lax.dot_general(k_ref[0], qz_ref[c], _NT, preferred_element_type=F32)
        if bias_tile is not None:
            s_t = s_t + bias_tile
        m_old = m_ref[c]
        m_new = jnp.maximum(m_old, jnp.max(s_t, axis=0, keepdims=True) + bias_row)
        alpha = jnp.exp2(m_old - m_new)
        p = jnp.exp2(s_t - (m_new - bias_row))
        l_ref[c] = alpha * l_ref[c] + jnp.sum(p, axis=0, keepdims=True)
        pv = lax.dot_general(v_ref[0], p.astype(BF16), _TN, preferred_element_type=F32)
        acc_ref[c] = alpha * acc_ref[c] + pv
        m_ref[c] = m_new

    @pl.when(jnp.abs(delta) <= 1)
    def _near():
        bias_tile = bias_ref[0, delta + 1]
        for c in range(2):
            step(c, bias_tile, 0.0)

    @pl.when(jnp.abs(delta) > 1)
    def _far():
        side = jnp.where(delta < 0, far_ref[h, 0], far_ref[h, 1])
        for c in range(2):
            step(c, None, side)

    @pl.when(ki == nk - 1)
    def _finish():
        _diff_finish(acc_ref[0], l_ref[0], acc_ref[1], l_ref[1], lam_ref, g_ref, o_ref, lam_init)


def _diff_attn_online(qb, kb, vb, far, bias_t, lam, g_col, lam_init, tile):
    bsz, _, s_len, _ = qb.shape
    nt = s_len // tile
    kernel = functools.partial(_diff_attn_online_kernel, lam_init=lam_init)

    def body(far_ref, q_ref, k_ref, v_ref, bias_ref, lam_ref, g_ref, o_ref, *scratch):
        kernel(far_ref, q_ref.at[0], k_ref.at[0], v_ref.at[0], bias_ref, lam_ref, g_ref, o_ref.at[0], *scratch)

    return pl.pallas_call(
        body,
        grid=(bsz, B_HEADS, nt, nt),
        in_specs=[
            pl.BlockSpec(memory_space=pltpu.SMEM),
            pl.BlockSpec((1, 1, tile, LANES), lambda b, h, qi, ki: (b, h, qi, 0)),
            pl.BlockSpec((1, 1, tile, LANES), lambda b, h, qi, ki: (b, h, ki, 0)),
            pl.BlockSpec((1, 1, tile, LANES), lambda b, h, qi, ki: (b, h, ki, 0)),
            pl.BlockSpec((1, 3, tile, tile), lambda b, h, qi, ki: (h, 0, 0, 0)),
            _const_spec((4, HEAD_DIM)),
            _const_spec((2 * HEAD_DIM, 1)),
        ],
        out_specs=pl.BlockSpec((1, 1, tile, LANES), lambda b, h, qi, ki: (b, h, qi, 0)),
        out_shape=jax.ShapeDtypeStruct((bsz, B_HEADS, s_len, LANES), BF16),
        scratch_shapes=[
            pltpu.VMEM((2, tile, LANES), BF16),
            pltpu.VMEM((2, 1, tile), F32),
            pltpu.VMEM((2, 1, tile), F32),
            pltpu.VMEM((2, 2 * HEAD_DIM, tile), F32),
        ],
        compiler_params=_cparams(("parallel", "parallel", "parallel", "arbitrary")),
        name="diff_attn_online",
    )(far, qb, kb, vb, bias_t, lam, g_col)


def _diff_attn(qb, kb, vb, db, lam, g_col, lam_init, tile):
    bounded = lambda: _diff_attn_bounded(qb, kb, vb, db["shifted"], lam, g_col, lam_init, tile)
    online = lambda: _diff_attn_online(qb, kb, vb, db["far"], db["near"], lam, g_col, lam_init, tile)
    return lax.cond(db["bounded_ok"], bounded, online)


FFN_CHUNKS = ((0, 1536), (1536, FFN_HIDDEN))


def _mix_ffn_kernel(x_ref, a_ref, b_ref, wa_ref, wb_ref, g_ref, wg_ref, wu_ref, wd_ref, o_ref):
    if len(b_ref.shape) == 4:
        b = jnp.concatenate([b_ref[0, hd] for hd in range(b_ref.shape[1])], axis=-1)
    else:
        b = b_ref[0]
    mix = (jnp.dot(a_ref[0], wa_ref[...], preferred_element_type=F32)
           + jnp.dot(b, wb_ref[...], preferred_element_type=F32))
    x1 = x_ref[0] + mix
    h = _rms(x1, g_ref[...]).astype(BF16)
    gates = [jnp.dot(h, wg_ref[:, lo:hi], preferred_element_type=F32) for lo, hi in FFN_CHUNKS]
    ups = [jnp.dot(h, wu_ref[:, lo:hi], preferred_element_type=F32) for lo, hi in FFN_CHUNKS]
    down = None
    for (lo, hi), gate, up in zip(FFN_CHUNKS, gates, ups):
        act = (gate * jax.nn.sigmoid(gate) * up).astype(BF16)
        part = jnp.dot(act, wd_ref[lo:hi, :], preferred_element_type=F32)
        down = part if down is None else down + part
    o_ref[0] = x1 + down


def _mix_ffn(x, a, b, wa, wb, g, wg, wu, wd, tm):
    bsz, s_len, _ = x.shape
    spt = s_len // tm
    row_spec = lambda w: pl.BlockSpec((1, tm, w), lambda i: (i // spt, i % spt, 0))
    if b.ndim == 4:
        b_spec = pl.BlockSpec((1, b.shape[1], tm, LANES), lambda i: (i // spt, 0, i % spt, 0))
    else:
        b_spec = row_spec(b.shape[-1])
    resident = pl.BlockSpec(memory_space=pltpu.VMEM)
    return pl.pallas_call(
        _mix_ffn_kernel,
        grid=(bsz * spt,),
        in_specs=[row_spec(D_MODEL), row_spec(a.shape[-1]), b_spec] + [resident] * 6,
        out_specs=row_spec(D_MODEL),
        out_shape=jax.ShapeDtypeStruct(x.shape, F32),
        compiler_params=_cparams(("parallel",)),
        name="mix_ffn",
    )(x, a, b, wa, wb, g, wg, wu, wd)


def _conv_in_kernel(x_ref, g_ref, w_ref, gb_ref, gx_ref, glu_ref):
    h = _rms(x_ref[0], g_ref[...]).astype(BF16)
    c = C_WIDTH
    gb, gc, xc, a, gate = [jnp.dot(h, w_ref[:, i * c:(i + 1) * c], preferred_element_type=F32) for i in range(5)]
    gb_ref[0] = gb.astype(BF16)
    gx_ref[0] = (gc * xc).astype(BF16)
    glu_ref[0] = (a * jax.nn.sigmoid(gate)).astype(BF16)


def _conv_in(x, g, w, tm):
    bsz, s_len, _ = x.shape
    spt = s_len // tm
    row_spec = lambda width: pl.BlockSpec((1, tm, width), lambda i: (i // spt, i % spt, 0))
    out = jax.ShapeDtypeStruct((bsz, s_len, C_WIDTH), BF16)
    return pl.pallas_call(
        _conv_in_kernel,
        grid=(bsz * spt,),
        in_specs=[row_spec(D_MODEL), _const_spec((1, D_MODEL)), _const_spec((D_MODEL, w.shape[1]))],
        out_specs=[row_spec(C_WIDTH)] * 3,
        out_shape=[out] * 3,
        compiler_params=_cparams(("parallel",)),
        name="conv_in",
    )(x, g, w)


CONV_CHUNK = 32
SUBLANES = 8


def _conv_mix_kernel(gb_ref, gx_ref, gxp_ref, gxn_ref, u_ref, up_ref, un_ref,
                     scw_ref, dww_ref, dwb_ref, lng_ref, lnb_ref, yc_ref, yu_ref, xs_ref, u_ref_f32, *, ts):
    t = pl.program_id(1)
    nt = pl.num_programs(1)
    has_prev = (t > 0).astype(F32)
    has_next = (t < nt - 1).astype(F32)
    n_shifted = ts + 2 * HALO - SUBLANES

    def taps(width):
        return [(j,) + divmod(HALO - width // 2 + j, SUBLANES)[::-1] for j in range(width)]

    def fill(cur, prev, nxt, width):
        xs_ref[0, 0:HALO, :] = prev[0].astype(F32) * has_prev
        xs_ref[0, HALO:HALO + ts, :] = cur[0].astype(F32)
        xs_ref[0, HALO + ts:2 * HALO + ts, :] = nxt[0].astype(F32) * has_next
        for r in sorted({r for _, r, _ in taps(width)} - {0}):
            xs_ref[r, 0:n_shifted, :] = xs_ref[0, r:r + n_shifted, :]

    def conv(w_ref, width, s0):
        acc = None
        for j, r, q in taps(width):
            window = xs_ref[r, pl.ds(s0 + SUBLANES * q, CONV_CHUNK), :]
            term = window.reshape(CONV_CHUNK // SUBLANES, SUBLANES, -1) * w_ref[j]
            acc = term if acc is None else acc + term
        return acc.reshape(CONV_CHUNK, -1)

    def chunks(body):
        def step(c, carry):
            body(pl.multiple_of(c * CONV_CHUNK, CONV_CHUNK))
            return carry
        lax.fori_loop(0, ts // CONV_CHUNK, step, 0)

    fill(gx_ref, gxp_ref, gxn_ref, SHORT_CONV)

    def short(s0):
        out = pl.ds(s0, CONV_CHUNK)
        yc_ref[0, out, :] = (gb_ref[0, out, :].astype(F32) * conv(scw_ref, SHORT_CONV, s0)).astype(BF16)
    chunks(short)

    fill(u_ref, up_ref, un_ref, CONF_CONV)

    def conf(s0):
        u_ref_f32[pl.ds(s0, CONV_CHUNK), :] = conv(dww_ref, CONF_CONV, s0) + dwb_ref[...]
    chunks(conf)

    u = u_ref_f32[...]
    mu = jnp.mean(u, axis=-1, keepdims=True)
    uc = u - mu
    var = jnp.mean(uc * uc, axis=-1, keepdims=True)
    y = uc * lax.rsqrt(var + EPS) * lng_ref[...] + lnb_ref[...]
    yu_ref[0] = (y * jax.nn.sigmoid(y)).astype(BF16)


def _conv_mix(gb, gx, glu, scw, dww, dwb, lng, lnb, ts):
    bsz, s_len, _ = gb.shape
    nt = s_len // ts
    r = ts // HALO
    nh = s_len // HALO
    cur = pl.BlockSpec((1, ts, 512), lambda b, t: (b, t, 0))
    prev = pl.BlockSpec((1, HALO, 512), lambda b, t: (b, jnp.maximum(t * r - 1, 0), 0))
    nxt = pl.BlockSpec((1, HALO, 512), lambda b, t: (b, jnp.minimum((t + 1) * r, nh - 1), 0))
    out_spec = pl.BlockSpec((1, ts, 512), lambda b, t: (b, t, 0))
    return pl.pallas_call(
        functools.partial(_conv_mix_kernel, ts=ts),
        grid=(bsz, nt),
        in_specs=[cur, cur, prev, nxt, cur, prev, nxt,
                  _const_spec((SHORT_CONV, SUBLANES, C_WIDTH)), _const_spec((CONF_CONV, SUBLANES, D_WIDTH)),
                  _const_spec((1, D_WIDTH)), _const_spec((1, D_WIDTH)), _const_spec((1, D_WIDTH))],
        out_specs=[out_spec, out_spec],
        out_shape=[jax.ShapeDtypeStruct((bsz, s_len, C_WIDTH), BF16),
                   jax.ShapeDtypeStruct((bsz, s_len, D_WIDTH), BF16)],
        scratch_shapes=[pltpu.VMEM((SUBLANES, ts + 2 * HALO, 512), F32), pltpu.VMEM((ts, D_WIDTH), F32)],
        compiler_params=_cparams(("parallel", "parallel")),
        name="conv_mix",
    )(gb, gx, gx, gx, glu, glu, glu, scw, dww, dwb, lng, lnb)


def _rel_bucket(rel):
    half = NUM_BUCKETS // 2
    max_exact = half // 2
    n = jnp.abs(rel)
    large = max_exact + (jnp.log(jnp.maximum(n, 1).astype(F32) / max_exact)
                         / math.log(MAX_DISTANCE / max_exact) * (half - max_exact)).astype(jnp.int32)
    large = jnp.minimum(large, half - 1)
    return jnp.where(rel > 0, half, 0) + jnp.where(n < max_exact, n, large)


def _toeplitz(u, rows, cols):
    length = u.shape[-1]
    flat = jnp.tile(u, (1,) * (u.ndim - 1) + (rows,))[..., :rows * (length - 1)]
    return flat.reshape(u.shape[:-1] + (rows, length - 1))[..., :cols]


LOG2E = math.log2(math.e)


def _win_bias(rel_bias):
    length = 4 * BLOCK
    n = jnp.arange(length)
    rel = jnp.where(n < BLOCK, -n, length - n) - BLOCK
    vec = rel_bias[:, :A_Q_HEADS][_rel_bucket(rel)].astype(F32).T * LOG2E
    vec = jnp.where((jnp.abs(rel) <= WINDOW)[None], vec, NEG)
    base = _toeplitz(vec, 3 * BLOCK, BLOCK)[jnp.array(WIN_HEAD_ORDER)]
    mid = base.transpose(1, 0, 2).reshape(3 * BLOCK, A_Q_HEADS * BLOCK)
    key_block = (jnp.arange(3 * BLOCK) // BLOCK)[:, None]
    return jnp.stack([jnp.where(key_block == 0, NEG, mid), mid, jnp.where(key_block == 2, NEG, mid)], axis=0)
MIN_EXP2_ARG = -120.0


def _diff_bias(rel_bias, b_qn, b_kn, tile):
    table = rel_bias[:, A_Q_HEADS:].astype(F32) * LOG2E
    length = 2 * tile
    n = jnp.arange(length)
    k_minus_q = jnp.where(n < tile, -n, length - n)
    rel = jnp.clip(k_minus_q[None, :] + jnp.array([-tile, 0, tile])[:, None], 1 - length, length - 1)
    near = _toeplitz(table[_rel_bucket(rel)].transpose(2, 0, 1), tile, tile)
    far = table[_rel_bucket(jnp.array([-length, length]))].T
    s_max = 1.02 * LOG2E * math.sqrt(HEAD_DIM) * jnp.max(jnp.abs(b_qn * b_kn))
    b_max, b_min = jnp.max(table, axis=0), jnp.min(table, axis=0)
    bound = s_max + b_max
    const = lambda col: jnp.broadcast_to((far[:, col] - bound)[:, None, None, None], (B_HEADS, 1, tile, tile))
    shifted = jnp.concatenate([const(0), near - bound[:, None, None, None], const(1)], axis=1)
    bounded_ok = jnp.all(-2.0 * s_max - (b_max - b_min) > MIN_EXP2_ARG)
    return {"near": near, "far": far, "shifted": shifted, "bounded_ok": bounded_ok}


def _by_win_head(m, axis):
    take = lambda h: lax.slice_in_dim(m, h * HEAD_DIM, (h + 1) * HEAD_DIM, axis=axis)
    return jnp.concatenate([take(h) for h in WIN_HEAD_ORDER], axis=axis)


def _attn_in_params(w_in, a_qn, a_kn, b_qn, b_kn):
    d = HEAD_DIM
    k0 = A_Q_HEADS * d
    w = jnp.concatenate([_by_win_head(w_in[:, :k0], 1), w_in[:, k0:]], axis=1).astype(BF16)
    scale = HEAD_DIM ** -0.5 * LOG2E
    ones = lambda n: jnp.ones((n,), F32)
    gain = jnp.concatenate([
        jnp.tile(a_qn, A_Q_HEADS) * scale, jnp.tile(a_kn, A_KV_HEADS), ones(A_KV_HEADS * d),
        jnp.tile(b_qn, 2 * B_HEADS) * scale, jnp.tile(b_kn, 2 * B_HEADS), ones(2 * B_HEADS * d),
    ]).astype(F32)[None, :]
    seg_id = jnp.arange(MXU_COLS) // d
    seg = jnp.where(seg_id[:, None] == seg_id[None, :], 1.0 / d, 0.0).astype(BF16)
    return w, gain, seg


def _trunk(x, p, *, tm, tile, ts, qb):
    qa, ka, va, qb_, kb, vb = _attn_in(x, p["mix_g"][0], p["attn_w"], p["attn_gain"], p["seg"], tm)
    ya = _win_attn(qa, ka, va, p["sink_row"], p["win_bias"], qb)
    yb = _diff_attn(qb_, kb, vb, p["diff_bias"][tile], p["lam"], p["subln"], p["lam_init"], tile)
    x = _mix_ffn(x, ya, yb, p["attn_wo_a"], p["attn_wo_b"],
                 p["ffn_g"][0], p["wg"][0], p["wu"][0], p["wd"][0], tm)

    gb, gx, glu = _conv_in(x, p["mix_g"][1], p["conv_w"], tm)
    yc, yu = _conv_mix(gb, gx, glu, p["scw"], p["dww"], p["dwb"], p["lng"], p["lnb"], ts)
    return _mix_ffn(x, yc, yu, p["conv_wo_a"], p["conv_wo_b"],
                    p["ffn_g"][1], p["wg"][1], p["wu"][1], p["wd"][1], tm)


def _prepare(rel_bias, mix_norm, ffn_norm, w_gate, w_up, w_down, attn_w_in, attn_w_out, a_q_norm, a_k_norm,
             a_sink, b_q_norm, b_k_norm, b_lambda, b_subln, conv_w_in, conv_w_out, short_conv_w, conf_dw_w,
             conf_dw_b, conf_ln_g, conf_ln_b, tiles):
    attn_w, attn_gain, seg = _attn_in_params(attn_w_in[0], a_q_norm[0], a_k_norm[0], b_q_norm[0], b_k_norm[0])
    half = A_Q_HEADS * HEAD_DIM
    return {
        "mix_g": [mix_norm[l][None, :].astype(F32) for l in range(2)],
        "ffn_g": [ffn_norm[l][None, :].astype(F32) for l in range(2)],
        "wg": [w_gate[l].astype(BF16) for l in range(2)],
        "wu": [w_up[l].astype(BF16) for l in range(2)],
        "wd": [w_down[l].astype(BF16) for l in range(2)],
        "attn_w": attn_w, "attn_gain": attn_gain, "seg": seg,
        "attn_wo_a": _by_win_head(attn_w_out[0][:half], 0).astype(BF16),
        "attn_wo_b": attn_w_out[0][half:].astype(BF16),
        "sink_row": jnp.repeat(a_sink[0].astype(F32)[jnp.array(WIN_HEAD_ORDER)] * LOG2E, BLOCK)[None, :],
        "win_bias": _win_bias(rel_bias),
        "diff_bias": {t: _diff_bias(rel_bias, b_q_norm[0], b_k_norm[0], t) for t in tiles},
        "lam": b_lambda[0].astype(F32),
        "subln": b_subln[0].astype(F32)[:, None],
        "lam_init": 0.8 - 0.6 * math.exp(-0.3 * 0),
        "conv_w": conv_w_in[0].astype(BF16),
        "conv_wo_a": conv_w_out[0][:C_WIDTH].astype(BF16), "conv_wo_b": conv_w_out[0][C_WIDTH:].astype(BF16),
        "scw": jnp.broadcast_to(short_conv_w[0].astype(F32)[:, None, :], (SHORT_CONV, SUBLANES, C_WIDTH)),
        "dww": jnp.broadcast_to(conf_dw_w[0].astype(F32)[:, None, :], (CONF_CONV, SUBLANES, D_WIDTH)),
        "dwb": conf_dw_b[0][None, :].astype(F32),
        "lng": conf_ln_g[0][None, :].astype(F32), "lnb": conf_ln_b[0][None, :].astype(F32),
    }


def _tiling(s_len):
    return dict(tm=512, tile=min(512, s_len), ts=min(512, s_len), qb=4)


def kernel(x_prompt, x_sample, rel_bias, mix_norm, ffn_norm, w_gate, w_up, w_down, attn_w_in, attn_w_out,
           a_q_norm, a_k_norm, a_sink, b_q_norm, b_k_norm, b_lambda, b_subln, conv_w_in, conv_w_out,
           short_conv_w, conf_dw_w, conf_dw_b, conf_ln_g, conf_ln_b):
    tp, tsm = _tiling(x_prompt.shape[1]), _tiling(x_sample.shape[1])
    p = _prepare(rel_bias, mix_norm, ffn_norm, w_gate, w_up, w_down, attn_w_in, attn_w_out, a_q_norm,
                 a_k_norm, a_sink, b_q_norm, b_k_norm, b_lambda, b_subln, conv_w_in, conv_w_out,
                 short_conv_w, conf_dw_w, conf_dw_b, conf_ln_g, conf_ln_b, {tp["tile"], tsm["tile"]})
    return (_trunk(x_prompt, p, **tp), _trunk(x_sample, p, **tsm))
```

```python
import functools
import math

import jax
import jax.numpy as jnp
from jax import lax
from jax.experimental import pallas as pl
from jax.experimental.pallas import tpu as pltpu

D_MODEL = 1024
HEAD_DIM = 64
A_Q_HEADS = 8
A_KV_HEADS = 2
WINDOW = 128
BLOCK = 128
B_HEADS = 4
NUM_BUCKETS = 32
MAX_DISTANCE = 128
C_WIDTH = 512
D_WIDTH = 512
SHORT_CONV = 3
CONF_CONV = 31
FFN_HIDDEN = 2816
EPS = 1e-6
NEG = -1e30

LANES = 128
MXU_COLS = 256
VMEM_LIMIT_BYTES = 56 * 1024 * 1024

BF16 = jnp.bfloat16
F32 = jnp.float32

PROJ0_W = 2304
NORM_CHUNKS0 = ("all", "all", "low", "all", "all", "all", "all", None, None)
WIN_HEAD_ORDER = (0, 4, 1, 5, 2, 6, 3, 7)

HALO = 16


def _cparams(sem):
    return pltpu.CompilerParams(dimension_semantics=sem, vmem_limit_bytes=VMEM_LIMIT_BYTES)


def _const_spec(shape):
    nd = len(shape)
    return pl.BlockSpec(shape, lambda *_: (0,) * nd)


def _rms(x, g):
    ms = jnp.mean(x * x, axis=-1, keepdims=True)
    return x * lax.rsqrt(ms + EPS) * g


def _attn_in_kernel(x_ref, g_ref, w_ref, gain_ref, seg_ref, qa_ref, ka_ref, va_ref, qb_ref, kb_ref, vb_ref):
    half = MXU_COLS // 2

    def store(c, y):
        if c < 2:
            qa_ref[:, c * MXU_COLS:(c + 1) * MXU_COLS] = y
        elif c == 2:
            ka_ref[...] = y[:, :half]
            va_ref[...] = y[:, half:]
        else:
            ref, first = ((qb_ref, 3), (kb_ref, 5), (vb_ref, 7))[(c - 3) // 2]
            ref[0, 2 * (c - first)] = y[:, :half]
            ref[0, 2 * (c - first) + 1] = y[:, half:]

    h = _rms(x_ref[...], g_ref[...]).astype(BF16)
    seg = seg_ref[...]
    low = lax.broadcasted_iota(jnp.int32, (1, MXU_COLS), 1) < MXU_COLS // 2
    chunk = lambda c: slice(c * MXU_COLS, (c + 1) * MXU_COLS)
    accs = [jnp.dot(h, w_ref[:, chunk(c)], preferred_element_type=F32) for c in range(len(NORM_CHUNKS0))]
    for c, normed in enumerate(NORM_CHUNKS0):
        acc = accs[c]
        if normed is not None:
            sq = acc * acc
            sq_hi = sq.astype(BF16)
            sq_lo = (sq - sq_hi.astype(F32)).astype(BF16)
            ms = (jnp.dot(sq_hi, seg, preferred_element_type=F32)
                  + jnp.dot(sq_lo, seg, preferred_element_type=F32))
            scale = lax.rsqrt(ms + EPS) * gain_ref[:, chunk(c)]
            acc = acc * (scale if normed == "all" else jnp.where(low, scale, 1.0))
        store(c, acc.astype(BF16))


def _attn_in(x, g, w, gain, seg, tm):
    bsz, s_len, _ = x.shape
    spt = s_len // tm
    flat = lambda width: pl.BlockSpec((1, tm, width), lambda i: (i // spt, i % spt, 0))
    heads = pl.BlockSpec((1, B_HEADS, tm, LANES), lambda i: (i // spt, 0, i % spt, 0))
    sds = lambda *shape: jax.ShapeDtypeStruct(shape, BF16)
    head_major = sds(bsz, B_HEADS, s_len, LANES)

    def body(x_ref, g_ref, w_ref, gain_ref, seg_ref, qa_ref, ka_ref, va_ref, qb_ref, kb_ref, vb_ref):
        _attn_in_kernel(x_ref.at[0], g_ref, w_ref, gain_ref, seg_ref,
                        qa_ref.at[0], ka_ref.at[0], va_ref.at[0], qb_ref, kb_ref, vb_ref)

    return pl.pallas_call(
        body,
        grid=(bsz * spt,),
        in_specs=[
            flat(D_MODEL),
            _const_spec((1, D_MODEL)),
            _const_spec((D_MODEL, PROJ0_W)),
            _const_spec((1, PROJ0_W)),
            _const_spec((MXU_COLS, MXU_COLS)),
        ],
        out_specs=[flat(512), flat(LANES), flat(LANES), heads, heads, heads],
        out_shape=[sds(bsz, s_len, 512), sds(bsz, s_len, LANES), sds(bsz, s_len, LANES),
                   head_major, head_major, head_major],
        compiler_params=_cparams(("parallel",)),
        name="attn_in",
    )(x, g, w, gain, seg)


WIN_COLS = A_Q_HEADS * BLOCK


def _win_attn_kernel(q_ref, kp_ref, kc_ref, kn_ref, vp_ref, vc_ref, vn_ref, bias_ref, sink_ref, o_ref, *, qb):
    step = pl.program_id(1)
    nb = pl.num_programs(1) * qb
    low = lax.broadcasted_iota(jnp.int32, (1, LANES), 1) < HEAD_DIM
    top = lax.broadcasted_iota(jnp.int32, (LANES, 1), 0) < HEAD_DIM
    sink = sink_ref[...]

    def piece(prev_ref, cur_ref, next_ref, i):
        if i == 0:
            return prev_ref[0]
        if i == qb + 1:
            return next_ref[0]
        return cur_ref[0, (i - 1) * BLOCK:i * BLOCK, :]

    def scores(b):
        kcat = jnp.concatenate([piece(kp_ref, kc_ref, kn_ref, b + i) for i in range(3)], axis=0)
        q = q_ref[0, b * BLOCK:(b + 1) * BLOCK, :]
        zero = jnp.zeros((BLOCK, LANES), BF16)
        halves = []
        for j in range(A_Q_HEADS // 2):
            slab = q[:, j * LANES:(j + 1) * LANES]
            halves += [jnp.where(low, slab, zero), jnp.where(low, zero, slab)]
        qcat = jnp.concatenate(halves, axis=0)
        return lax.dot_general(kcat, qcat, _NT, preferred_element_type=F32)

    def weights(b, s_t):
        n = step * qb + b
        edge = jnp.where(n == 0, 0, jnp.where(n == nb - 1, 2, 1))
        s_t = s_t + bias_ref[edge]
        m = jnp.maximum(jnp.max(s_t, axis=0, keepdims=True), sink)
        p = jnp.exp2(s_t - m)
        den = jnp.sum(p, axis=0, keepdims=True) + jnp.exp2(sink - m)
        return p.astype(BF16), den

    def values(b, p, den):
        vcat = jnp.concatenate([piece(vp_ref, vc_ref, vn_ref, b + i) for i in range(3)], axis=0)
        return lax.dot_general(vcat, p, _TN, preferred_element_type=F32) / den

    s_all = [scores(b) for b in range(qb)]
    w_all = [weights(b, s_all[b]) for b in range(qb)]
    o_all = [values(b, *w_all[b]) for b in range(qb)]
    for b in range(qb):
        for j in range(A_Q_HEADS // 2):
            c0 = 2 * j * BLOCK
            slab_t = jnp.where(top, o_all[b][:, c0:c0 + BLOCK], o_all[b][:, c0 + BLOCK:c0 + 2 * BLOCK])
            o_ref[0, b * BLOCK:(b + 1) * BLOCK, j * LANES:(j + 1) * LANES] = slab_t.T.astype(BF16)


def _win_attn(qa, ka, va, sink_row, bias, qb):
    bsz, s_len, _ = qa.shape
    nb = s_len // BLOCK
    assert nb % qb == 0 and nb >= 2
    kv_specs = [
        pl.BlockSpec((1, BLOCK, LANES), lambda b, t: (b, jnp.maximum(t * qb - 1, 0), 0)),
        pl.BlockSpec((1, qb * BLOCK, LANES), lambda b, t: (b, t, 0)),
        pl.BlockSpec((1, BLOCK, LANES), lambda b, t: (b, jnp.minimum((t + 1) * qb, nb - 1), 0)),
    ]
    return pl.pallas_call(
        functools.partial(_win_attn_kernel, qb=qb),
        grid=(bsz, nb // qb),
        in_specs=[pl.BlockSpec((1, qb * BLOCK, 512), lambda b, t: (b, t, 0))] + kv_specs + kv_specs
                 + [pl.BlockSpec(memory_space=pltpu.VMEM), pl.BlockSpec(memory_space=pltpu.VMEM)],
        out_specs=pl.BlockSpec((1, qb * BLOCK, 512), lambda b, t: (b, t, 0)),
        out_shape=jax.ShapeDtypeStruct((bsz, s_len, 512), BF16),
        compiler_params=_cparams(("parallel", "parallel")),
        name="win_attn",
    )(qa, ka, ka, ka, va, va, va, bias, sink_row)


_NT = (((1,), (1,)), ((), ()))
_TN = (((0,), (0,)), ((), ()))


def _split_maps(q, qz_ref):
    lane = lax.broadcasted_iota(jnp.int32, (1, LANES), 1)
    zero = jnp.zeros_like(q)
    qz_ref[0] = jnp.where(lane < HEAD_DIM, q, zero)
    qz_ref[1] = jnp.where(lane < HEAD_DIM, zero, q)


def _diff_finish(acc0, l0, acc1, l1, lam_ref, g_ref, o_ref, lam_init):
    lam = lam_ref[...]
    lam_full = (jnp.exp(jnp.sum(lam[0:1] * lam[1:2], axis=-1, keepdims=True))
                - jnp.exp(jnp.sum(lam[2:3] * lam[3:4], axis=-1, keepdims=True)) + lam_init)
    o = acc0 / l0 - lam_full * (acc1 / l1)
    ms = jnp.mean(o * o, axis=0, keepdims=True)
    y = o * lax.rsqrt(ms + EPS) * g_ref[...] * (1.0 - lam_init)
    o_ref[0] = y.T.astype(BF16)


def _diff_attn_bounded_kernel(q_ref, k_ref, v_ref, bias_ref, lam_ref, g_ref, o_ref,
                              qz_ref, s_ref, l_ref, acc_ref, *, lam_init, tile, unroll):
    qi = pl.program_id(2)
    nk = k_ref.shape[1] // tile
    _split_maps(q_ref[0], qz_ref)
    l_ref[...] = jnp.zeros(l_ref.shape, F32)
    acc_ref[...] = jnp.zeros(acc_ref.shape, F32)

    def rows(ki):
        return pl.ds(pl.multiple_of(ki * tile, tile), tile)

    def produce(ki, buf):
        k = k_ref[0, rows(ki), :]
        for c in range(2):
            s_ref[buf, c] = lax.dot_general(k, qz_ref[c], _NT, preferred_element_type=F32)

    def consume(ki, buf):
        v = v_ref[0, rows(ki), :]
        bias = bias_ref[0, jnp.clip(ki - qi + 2, 0, 4)]
        for c in range(2):
            p = jnp.exp2(s_ref[buf, c] + bias)
            l_ref[c] += jnp.sum(p.reshape(tile // 8, 8, tile), axis=0)
            acc_ref[c] += lax.dot_general(v, p.astype(BF16), _TN, preferred_element_type=F32)

    def group(first, last_group):
        for t in range(unroll):
            if not (last_group and t == unroll - 1):
                produce(first + t + 1, (t + 1) % 2)
            consume(first + t, t % 2)

    def body(j, carry):
        group(j * unroll, False)
        return carry

    produce(0, 0)
    lax.fori_loop(0, nk // unroll - 1, body, 0)
    group(nk - unroll, True)
    l0 = jnp.sum(l_ref[0], axis=0, keepdims=True)
    l1 = jnp.sum(l_ref[1], axis=0, keepdims=True)
    _diff_finish(acc_ref[0], l0, acc_ref[1], l1, lam_ref, g_ref, o_ref, lam_init)


def _diff_attn_bounded(qb, kb, vb, bias_t, lam, g_col, lam_init, tile):
    bsz, _, s_len, _ = qb.shape
    nk = s_len // tile
    unroll = next(u for u in (8, 4, 2) if nk % u == 0)
    kernel = functools.partial(_diff_attn_bounded_kernel, lam_init=lam_init, tile=tile, unroll=unroll)

    def body(q_ref, k_ref, v_ref, bias_ref, lam_ref, g_ref, o_ref, *scratch):
        kernel(q_ref.at[0], k_ref.at[0], v_ref.at[0], bias_ref, lam_ref, g_ref, o_ref.at[0], *scratch)

    return pl.pallas_call(
        body,
        grid=(bsz, B_HEADS, s_len // tile),
        in_specs=[
            pl.BlockSpec((1, 1, tile, LANES), lambda b, h, qi: (b, h, qi, 0)),
            pl.BlockSpec((1, 1, s_len, LANES), lambda b, h, qi: (b, h, 0, 0)),
            pl.BlockSpec((1, 1, s_len, LANES), lambda b, h, qi: (b, h, 0, 0)),
            pl.BlockSpec((1, 5, tile, tile), lambda b, h, qi: (h, 0, 0, 0)),
            _const_spec((4, HEAD_DIM)),
            _const_spec((2 * HEAD_DIM, 1)),
        ],
        out_specs=pl.BlockSpec((1, 1, tile, LANES), lambda b, h, qi: (b, h, qi, 0)),
        out_shape=jax.ShapeDtypeStruct((bsz, B_HEADS, s_len, LANES), BF16),
        scratch_shapes=[
            pltpu.VMEM((2, tile, LANES), BF16),
            pltpu.VMEM((2, 2, tile, tile), F32),
            pltpu.VMEM((2, 8, tile), F32),
            pltpu.VMEM((2, 2 * HEAD_DIM, tile), F32),
        ],
        compiler_params=_cparams(("parallel", "parallel", "parallel")),
        name="diff_attn",
    )(qb, kb, vb, bias_t, lam, g_col)


def _diff_attn_online_kernel(far_ref, q_ref, k_ref, v_ref, bias_ref, lam_ref, g_ref, o_ref,
                             qz_ref, m_ref, l_ref, acc_ref, *, lam_init):
    h = pl.program_id(1)
    qi = pl.program_id(2)
    ki = pl.program_id(3)
    nk = pl.num_programs(3)

    @pl.when(ki == 0)
    def _init():
        _split_maps(q_ref[0], qz_ref)
        m_ref[...] = jnp.full(m_ref.shape, NEG, F32)
        l_ref[...] = jnp.zeros(l_ref.shape, F32)
        acc_ref[...] = jnp.zeros(acc_ref.shape, F32)

    delta = ki - qi

    def step(c, bias_tile, bias_row):
        s_t = lax.dot_general(k_ref[0], qz_ref[c], _NT, preferred_element_type=F32)
        if bias_tile is not None:
            s_t = s_t + bias_tile
        m_old = m_ref[c]
        m_new = jnp.maximum(m_old, jnp.max(s_t, axis=0, keepdims=True) + bias_row)
        alpha = jnp.exp2(m_old - m_new)
        p = jnp.exp2(s_t - (m_new - bias_row))
        l_ref[c] = alpha * l_ref[c] + jnp.sum(p, axis=0, keepdims=True)
        pv = lax.dot_general(v_ref[0], p.astype(BF16), _TN, preferred_element_type=F32)
        acc_ref[c] = alpha * acc_ref[c] + pv
        m_ref[c] = m_new

    @pl.when(jnp.abs(delta) <= 1)
    def _near():
        bias_tile = bias_ref[0, delta + 1]
        for c in range(2):
            step(c, bias_tile, 0.0)

    @pl.when(jnp.abs(delta) > 1)
    def _far():
        side = jnp.where(delta < 0, far_ref[h, 0], far_ref[h, 1])
        for c in range(2):
            step(c, None, side)

    @pl.when(ki == nk - 1)
    def _finish():
        _diff_finish(acc_ref[0], l_ref[0], acc_ref[1], l_ref[1], lam_ref, g_ref, o_ref, lam_init)


def _diff_attn_online(qb, kb, vb, far, bias_t, lam, g_col, lam_init, tile):
    bsz, _, s_len, _ = qb.shape
    nt = s_len // tile
    kernel = functools.partial(_diff_attn_online_kernel, lam_init=lam_init)

    def body(far_ref, q_ref, k_ref, v_ref, bias_ref, lam_ref, g_ref, o_ref, *scratch):
        kernel(far_ref, q_ref.at[0], k_ref.at[0], v_ref.at[0], bias_ref, lam_ref, g_ref, o_ref.at[0], *scratch)

    return pl.pallas_call(
        body,
        grid=(bsz, B_HEADS, nt, nt),
        in_specs=[
            pl.BlockSpec(memory_space=pltpu.SMEM),
            pl.BlockSpec((1, 1, tile, LANES), lambda b, h, qi, ki: (b, h, qi, 0)),
            pl.BlockSpec((1, 1, tile, LANES), lambda b, h, qi, ki: (b, h, ki, 0)),
            pl.BlockSpec((1, 1, tile, LANES), lambda b, h, qi, ki: (b, h, ki, 0)),
            pl.BlockSpec((1, 3, tile, tile), lambda b, h, qi, ki: (h, 0, 0, 0)),
            _const_spec((4, HEAD_DIM)),
            _const_spec((2 * HEAD_DIM, 1)),
        ],
        out_specs=pl.BlockSpec((1, 1, tile, LANES), lambda b, h, qi, ki: (b, h, qi, 0)),
        out_shape=jax.ShapeDtypeStruct((bsz, B_HEADS, s_len, LANES), BF16),
        scratch_shapes=[
            pltpu.VMEM((2, tile, LANES), BF16),
            pltpu.VMEM((2, 1, tile), F32),
            pltpu.VMEM((2, 1, tile), F32),
            pltpu.VMEM((2, 2 * HEAD_DIM, tile), F32),
        ],
        compiler_params=_cparams(("parallel", "parallel", "parallel", "arbitrary")),
        name="diff_attn_online",
    )(far, qb, kb, vb, bias_t, lam, g_col)


def _diff_attn(qb, kb, vb, db, lam, g_col, lam_init, tile):
    bounded = lambda: _diff_attn_bounded(qb, kb, vb, db["shifted"], lam, g_col, lam_init, tile)
    online = lambda: _diff_attn_online(qb, kb, vb, db["far"], db["near"], lam, g_col, lam_init, tile)
    return lax.cond(db["bounded_ok"], bounded, online)


FFN_CHUNKS = ((0, 1536), (1536, FFN_HIDDEN))


def _mix_ffn_kernel(x_ref, a_ref, b_ref, wa_ref, wb_ref, g_ref, wg_ref, wu_ref, wd_ref, o_ref):
    if len(b_ref.shape) == 4:
        b = jnp.concatenate([b_ref[0, hd] for hd in range(b_ref.shape[1])], axis=-1)
    else:
        b = b_ref[0]
    mix = (jnp.dot(a_ref[0], wa_ref[...], preferred_element_type=F32)
           + jnp.dot(b, wb_ref[...], preferred_element_type=F32))
    x1 = x_ref[0] + mix
    h = _rms(x1, g_ref[...]).astype(BF16)
    gates = [jnp.dot(h, wg_ref[:, lo:hi], preferred_element_type=F32) for lo, hi in FFN_CHUNKS]
    ups = [jnp.dot(h, wu_ref[:, lo:hi], preferred_element_type=F32) for lo, hi in FFN_CHUNKS]
    down = None
    for (lo, hi), gate, up in zip(FFN_CHUNKS, gates, ups):
        act = (gate * jax.nn.sigmoid(gate) * up).astype(BF16)
        part = jnp.dot(act, wd_ref[lo:hi, :], preferred_element_type=F32)
        down = part if down is None else down + part
    o_ref[0] = x1 + down


def _mix_ffn(x, a, b, wa, wb, g, wg, wu, wd, tm):
    bsz, s_len, _ = x.shape
    spt = s_len // tm
    row_spec = lambda w: pl.BlockSpec((1, tm, w), lambda i: (i // spt, i % spt, 0))
    if b.ndim == 4:
        b_spec = pl.BlockSpec((1, b.shape[1], tm, LANES), lambda i: (i // spt, 0, i % spt, 0))
    else:
        b_spec = row_spec(b.shape[-1])
    resident = pl.BlockSpec(memory_space=pltpu.VMEM)
    return pl.pallas_call(
        _mix_ffn_kernel,
        grid=(bsz * spt,),
        in_specs=[row_spec(D_MODEL), row_spec(a.shape[-1]), b_spec] + [resident] * 6,
        out_specs=row_spec(D_MODEL),
        out_shape=jax.ShapeDtypeStruct(x.shape, F32),
        compiler_params=_cparams(("parallel",)),
        name="mix_ffn",
    )(x, a, b, wa, wb, g, wg, wu, wd)


def _conv_in_kernel(x_ref, g_ref, w_ref, gb_ref, gx_ref, glu_ref):
    h = _rms(x_ref[0], g_ref[...]).astype(BF16)
    c = C_WIDTH
    gb, gc, xc, a, gate = [jnp.dot(h, w_ref[:, i * c:(i + 1) * c], preferred_element_type=F32) for i in range(5)]
    gb_ref[0] = gb.astype(BF16)
    gx_ref[0] = (gc * xc).astype(BF16)
    glu_ref[0] = (a * jax.nn.sigmoid(gate)).astype(BF16)


def _conv_in(x, g, w, tm):
    bsz, s_len, _ = x.shape
    spt = s_len // tm
    row_spec = lambda width: pl.BlockSpec((1, tm, width), lambda i: (i // spt, i % spt, 0))
    out = jax.ShapeDtypeStruct((bsz, s_len, C_WIDTH), BF16)
    return pl.pallas_call(
        _conv_in_kernel,
        grid=(bsz * spt,),
        in_specs=[row_spec(D_MODEL), _const_spec((1, D_MODEL)), _const_spec((D_MODEL, w.shape[1]))],
        out_specs=[row_spec(C_WIDTH)] * 3,
        out_shape=[out] * 3,
        compiler_params=_cparams(("parallel",)),
        name="conv_in",
    )(x, g, w)


CONV_CHUNK = 32
SUBLANES = 8


def _conv_mix_kernel(gb_ref, gx_ref, gxp_ref, gxn_ref, u_ref, up_ref, un_ref,
                     scw_ref, dww_ref, dwb_ref, lng_ref, lnb_ref, yc_ref, yu_ref, xs_ref, u_ref_f32, *, ts):
    t = pl.program_id(1)
    nt = pl.num_programs(1)
    has_prev = (t > 0).astype(F32)
    has_next = (t < nt - 1).astype(F32)
    n_shifted = ts + 2 * HALO - SUBLANES

    def taps(width):
        return [(j,) + divmod(HALO - width // 2 + j, SUBLANES)[::-1] for j in range(width)]

    def fill(cur, prev, nxt, width):
        xs_ref[0, 0:HALO, :] = prev[0].astype(F32) * has_prev
        xs_ref[0, HALO:HALO + ts, :] = cur[0].astype(F32)
        xs_ref[0, HALO + ts:2 * HALO + ts, :] = nxt[0].astype(F32) * has_next
        for r in sorted({r for _, r, _ in taps(width)} - {0}):
            xs_ref[r, 0:n_shifted, :] = xs_ref[0, r:r + n_shifted, :]

    def conv(w_ref, width, s0):
        acc = None
        for j, r, q in taps(width):
            window = xs_ref[r, pl.ds(s0 + SUBLANES * q, CONV_CHUNK), :]
            term = window.reshape(CONV_CHUNK // SUBLANES, SUBLANES, -1) * w_ref[j]
            acc = term if acc is None else acc + term
        return acc.reshape(CONV_CHUNK, -1)

    def chunks(body):
        def step(c, carry):
            body(pl.multiple_of(c * CONV_CHUNK, CONV_CHUNK))
            return carry
        lax.fori_loop(0, ts // CONV_CHUNK, step, 0, unroll=True)

    fill(gx_ref, gxp_ref, gxn_ref, SHORT_CONV)

    def short(s0):
        out = pl.ds(s0, CONV_CHUNK)
        yc_ref[0, out, :] = (gb_ref[0, out, :].astype(F32) * conv(scw_ref, SHORT_CONV, s0)).astype(BF16)
    chunks(short)

    fill(u_ref, up_ref, un_ref, CONF_CONV)

    def conf(s0):
        u_ref_f32[pl.ds(s0, CONV_CHUNK), :] = conv(dww_ref, CONF_CONV, s0) + dwb_ref[...]
    chunks(conf)

    u = u_ref_f32[...]
    mu = jnp.mean(u, axis=-1, keepdims=True)
    uc = u - mu
    var = jnp.mean(uc * uc, axis=-1, keepdims=True)
    y = uc * lax.rsqrt(var + EPS) * lng_ref[...] + lnb_ref[...]
    yu_ref[0] = (y * jax.nn.sigmoid(y)).astype(BF16)


def _conv_mix(gb, gx, glu, scw, dww, dwb, lng, lnb, ts):
    bsz, s_len, _ = gb.shape
    nt = s_len // ts
    r = ts // HALO
    nh = s_len // HALO
    cur = pl.BlockSpec((1, ts, 512), lambda b, t: (b, t, 0))
    prev = pl.BlockSpec((1, HALO, 512), lambda b, t: (b, jnp.maximum(t * r - 1, 0), 0))
    nxt = pl.BlockSpec((1, HALO, 512), lambda b, t: (b, jnp.minimum((t + 1) * r, nh - 1), 0))
    out_spec = pl.BlockSpec((1, ts, 512), lambda b, t: (b, t, 0))
    return pl.pallas_call(
        functools.partial(_conv_mix_kernel, ts=ts),
        grid=(bsz, nt),
        in_specs=[cur, cur, prev, nxt, cur, prev, nxt,
                  _const_spec((SHORT_CONV, SUBLANES, C_WIDTH)), _const_spec((CONF_CONV, SUBLANES, D_WIDTH)),
                  _const_spec((1, D_WIDTH)), _const_spec((1, D_WIDTH)), _const_spec((1, D_WIDTH))],
        out_specs=[out_spec, out_spec],
        out_shape=[jax.ShapeDtypeStruct((bsz, s_len, C_WIDTH), BF16),
                   jax.ShapeDtypeStruct((bsz, s_len, D_WIDTH), BF16)],
        scratch_shapes=[pltpu.VMEM((SUBLANES, ts + 2 * HALO, 512), F32), pltpu.VMEM((ts, D_WIDTH), F32)],
        compiler_params=_cparams(("parallel", "parallel")),
        name="conv_mix",
    )(gb, gx, gx, gx, glu, glu, glu, scw, dww, dwb, lng, lnb)


def _rel_bucket(rel):
    half = NUM_BUCKETS // 2
    max_exact = half // 2
    n = jnp.abs(rel)
    large = max_exact + (jnp.log(jnp.maximum(n, 1).astype(F32) / max_exact)
                         / math.log(MAX_DISTANCE / max_exact) * (half - max_exact)).astype(jnp.int32)
    large = jnp.minimum(large, half - 1)
    return jnp.where(rel > 0, half, 0) + jnp.where(n < max_exact, n, large)


def _toeplitz(u, rows, cols):
    length = u.shape[-1]
    flat = jnp.tile(u, (1,) * (u.ndim - 1) + (rows,))[..., :rows * (length - 1)]
    return flat.reshape(u.shape[:-1] + (rows, length - 1))[..., :cols]


LOG2E = math.log2(math.e)


def _win_bias(rel_bias):
    length = 4 * BLOCK
    n = jnp.arange(length)
    rel = jnp.where(n < BLOCK, -n, length - n) - BLOCK
    vec = rel_bias[:, :A_Q_HEADS][_rel_bucket(rel)].astype(F32).T * LOG2E
    vec = jnp.where((jnp.abs(rel) <= WINDOW)[None], vec, NEG)
    base = _toeplitz(vec, 3 * BLOCK, BLOCK)[jnp.array(WIN_HEAD_ORDER)]
    mid = base.transpose(1, 0, 2).reshape(3 * BLOCK, A_Q_HEADS * BLOCK)
    key_block = (jnp.arange(3 * BLOCK) // BLOCK)[:, None]
    return jnp.stack([jnp.where(key_block == 0, NEG, mid), mid, jnp.where(key_block == 2, NEG, mid)], axis=0)
MIN_EXP2_ARG = -120.0


def _diff_bias(rel_bias, b_qn, b_kn, tile):
    table = rel_bias[:, A_Q_HEADS:].astype(F32) * LOG2E
    length = 2 * tile
    n = jnp.arange(length)
    k_minus_q = jnp.where(n < tile, -n, length - n)
    rel = jnp.clip(k_minus_q[None, :] + jnp.array([-tile, 0, tile])[:, None], 1 - length, length - 1)
    near = _toeplitz(table[_rel_bucket(rel)].transpose(2, 0, 1), tile, tile)
    far = table[_rel_bucket(jnp.array([-length, length]))].T
    s_max = 1.02 * LOG2E * math.sqrt(HEAD_DIM) * jnp.max(jnp.abs(b_qn * b_kn))
    b_max, b_min = jnp.max(table, axis=0), jnp.min(table, axis=0)
    bound = s_max + b_max
    const = lambda col: jnp.broadcast_to((far[:, col] - bound)[:, None, None, None], (B_HEADS, 1, tile, tile))
    shifted = jnp.concatenate([const(0), near - bound[:, None, None, None], const(1)], axis=1)
    bounded_ok = jnp.all(-2.0 * s_max - (b_max - b_min) > MIN_EXP2_ARG)
    return {"near": near, "far": far, "shifted": shifted, "bounded_ok": bounded_ok}


def _by_win_head(m, axis):
    take = lambda h: lax.slice_in_dim(m, h * HEAD_DIM, (h + 1) * HEAD_DIM, axis=axis)
    return jnp.concatenate([take(h) for h in WIN_HEAD_ORDER], axis=axis)


def _attn_in_params(w_in, a_qn, a_kn, b_qn, b_kn):
    d = HEAD_DIM
    k0 = A_Q_HEADS * d
    w = jnp.concatenate([_by_win_head(w_in[:, :k0], 1), w_in[:, k0:]], axis=1).astype(BF16)
    scale = HEAD_DIM ** -0.5 * LOG2E
    ones = lambda n: jnp.ones((n,), F32)
    gain = jnp.concatenate([
        jnp.tile(a_qn, A_Q_HEADS) * scale, jnp.tile(a_kn, A_KV_HEADS), ones(A_KV_HEADS * d),
        jnp.tile(b_qn, 2 * B_HEADS) * scale, jnp.tile(b_kn, 2 * B_HEADS), ones(2 * B_HEADS * d),
    ]).astype(F32)[None, :]
    seg_id = jnp.arange(MXU_COLS) // d
    seg = jnp.where(seg_id[:, None] == seg_id[None, :], 1.0 / d, 0.0).astype(BF16)
    return w, gain, seg


def _trunk(x, p, *, tm, tile, ts, qb):
    qa, ka, va, qb_, kb, vb = _attn_in(x, p["mix_g"][0], p["attn_w"], p["attn_gain"], p["seg"], tm)
    ya = _win_attn(qa, ka, va, p["sink_row"], p["win_bias"], qb)
    yb = _diff_attn(qb_, kb, vb, p["diff_bias"][tile], p["lam"], p["subln"], p["lam_init"], tile)
    x = _mix_ffn(x, ya, yb, p["attn_wo_a"], p["attn_wo_b"],
                 p["ffn_g"][0], p["wg"][0], p["wu"][0], p["wd"][0], tm)

    gb, gx, glu = _conv_in(x, p["mix_g"][1], p["conv_w"], tm)
    yc, yu = _conv_mix(gb, gx, glu, p["scw"], p["dww"], p["dwb"], p["lng"], p["lnb"], ts)
    return _mix_ffn(x, yc, yu, p["conv_wo_a"], p["conv_wo_b"],
                    p["ffn_g"][1], p["wg"][1], p["wu"][1], p["wd"][1], tm)


def _prepare(rel_bias, mix_norm, ffn_norm, w_gate, w_up, w_down, attn_w_in, attn_w_out, a_q_norm, a_k_norm,
             a_sink, b_q_norm, b_k_norm, b_lambda, b_subln, conv_w_in, conv_w_out, short_conv_w, conf_dw_w,
             conf_dw_b, conf_ln_g, conf_ln_b, tiles):
    attn_w, attn_gain, seg = _attn_in_params(attn_w_in[0], a_q_norm[0], a_k_norm[0], b_q_norm[0], b_k_norm[0])
    half = A_Q_HEADS * HEAD_DIM
    return {
        "mix_g": [mix_norm[l][None, :].astype(F32) for l in range(2)],
        "ffn_g": [ffn_norm[l][None, :].astype(F32) for l in range(2)],
        "wg": [w_gate[l].astype(BF16) for l in range(2)],
        "wu": [w_up[l].astype(BF16) for l in range(2)],
        "wd": [w_down[l].astype(BF16) for l in range(2)],
        "attn_w": attn_w, "attn_gain": attn_gain, "seg": seg,
        "attn_wo_a": _by_win_head(attn_w_out[0][:half], 0).astype(BF16),
        "attn_wo_b": attn_w_out[0][half:].astype(BF16),
        "sink_row": jnp.repeat(a_sink[0].astype(F32)[jnp.array(WIN_HEAD_ORDER)] * LOG2E, BLOCK)[None, :],
        "win_bias": _win_bias(rel_bias),
        "diff_bias": {t: _diff_bias(rel_bias, b_q_norm[0], b_k_norm[0], t) for t in tiles},
        "lam": b_lambda[0].astype(F32),
        "subln": b_subln[0].astype(F32)[:, None],
        "lam_init": 0.8 - 0.6 * math.exp(-0.3 * 0),
        "conv_w": conv_w_in[0].astype(BF16),
        "conv_wo_a": conv_w_out[0][:C_WIDTH].astype(BF16), "conv_wo_b": conv_w_out[0][C_WIDTH:].astype(BF16),
        "scw": jnp.broadcast_to(short_conv_w[0].astype(F32)[:, None, :], (SHORT_CONV, SUBLANES, C_WIDTH)),
        "dww": jnp.broadcast_to(conf_dw_w[0].astype(F32)[:, None, :], (CONF_CONV, SUBLANES, D_WIDTH)),
        "dwb": conf_dw_b[0][None, :].astype(F32),
        "lng": conf_ln_g[0][None, :].astype(F32), "lnb": conf_ln_b[0][None, :].astype(F32),
    }


def _tiling(s_len):
    return dict(tm=min(1024, s_len), tile=min(512, s_len), ts=min(512, s_len), qb=8)


def kernel(x_prompt, x_sample, rel_bias, mix_norm, ffn_norm, w_gate, w_up, w_down, attn_w_in, attn_w_out,
           a_q_norm, a_k_norm, a_sink, b_q_norm, b_k_norm, b_lambda, b_subln, conv_w_in, conv_w_out,
           short_conv_w, conf_dw_w, conf_dw_b, conf_ln_g, conf_ln_b):
    tp, tsm = _tiling(x_prompt.shape[1]), _tiling(x_sample.shape[1])
    p = _prepare(rel_bias, mix_norm, ffn_norm, w_gate, w_up, w_down, attn_w_in, attn_w_out, a_q_norm,
                 a_k_norm, a_sink, b_q_norm, b_k_norm, b_lambda, b_subln, conv_w_in, conv_w_out,
                 short_conv_w, conf_dw_w, conf_dw_b, conf_ln_g, conf_ln_b, {tp["tile"], tsm["tile"]})
    return (_trunk(x_prompt, p, **tp), _trunk(x_sample, p, **tsm))
```

```python
import functools
import math

import jax
import jax.numpy as jnp
from jax import lax
from jax.experimental import pallas as pl
from jax.experimental.pallas import tpu as pltpu

D_MODEL = 1024
HEAD_DIM = 64
A_Q_HEADS = 8
A_KV_HEADS = 2
WINDOW = 128
BLOCK = 128
B_HEADS = 4
NUM_BUCKETS = 32
MAX_DISTANCE = 128
C_WIDTH = 512
D_WIDTH = 512
SHORT_CONV = 3
CONF_CONV = 31
FFN_HIDDEN = 2816
EPS = 1e-6
NEG = -1e30

LANES = 128
MXU_COLS = 256
VMEM_LIMIT_BYTES = 56 * 1024 * 1024

BF16 = jnp.bfloat16
F32 = jnp.float32

PROJ0_W = 2304
NORM_CHUNKS0 = ("all", "all", "low", "all", "all", "all", "all", None, None)
WIN_HEAD_ORDER = (0, 4, 1, 5, 2, 6, 3, 7)

HALO = 16


def _cparams(sem):
    return pltpu.CompilerParams(dimension_semantics=sem, vmem_limit_bytes=VMEM_LIMIT_BYTES)


def _const_spec(shape):
    nd = len(shape)
    return pl.BlockSpec(shape, lambda *_: (0,) * nd)


def _rms(x, g):
    ms = jnp.mean(x * x, axis=-1, keepdims=True)
    return x * lax.rsqrt(ms + EPS) * g


def _attn_in_kernel(x_ref, g_ref, w_ref, gain_ref, seg_ref, qa_ref, ka_ref, va_ref, qb_ref, kb_ref, vb_ref):
    half = MXU_COLS // 2

    def store(c, y):
        if c < 2:
            qa_ref[:, c * MXU_COLS:(c + 1) * MXU_COLS] = y
        elif c == 2:
            ka_ref[...] = y[:, :half]
            va_ref[...] = y[:, half:]
        else:
            ref, first = ((qb_ref, 3), (kb_ref, 5), (vb_ref, 7))[(c - 3) // 2]
            ref[0, 2 * (c - first)] = y[:, :half]
            ref[0, 2 * (c - first) + 1] = y[:, half:]

    h = _rms(x_ref[...], g_ref[...]).astype(BF16)
    seg = seg_ref[...]
    low = lax.broadcasted_iota(jnp.int32, (1, MXU_COLS), 1) < MXU_COLS // 2
    chunk = lambda c: slice(c * MXU_COLS, (c + 1) * MXU_COLS)
    accs = [jnp.dot(h, w_ref[:, chunk(c)], preferred_element_type=F32) for c in range(len(NORM_CHUNKS0))]
    for c, normed in enumerate(NORM_CHUNKS0):
        acc = accs[c]
        if normed is not None:
            sq = acc * acc
            sq_hi = sq.astype(BF16)
            sq_lo = (sq - sq_hi.astype(F32)).astype(BF16)
            ms = (jnp.dot(sq_hi, seg, preferred_element_type=F32)
                  + jnp.dot(sq_lo, seg, preferred_element_type=F32))
            scale = lax.rsqrt(ms + EPS) * gain_ref[:, chunk(c)]
            acc = acc * (scale if normed == "all" else jnp.where(low, scale, 1.0))
        store(c, acc.astype(BF16))


def _attn_in(x, g, w, gain, seg, tm):
    bsz, s_len, _ = x.shape
    spt = s_len // tm
    flat = lambda width: pl.BlockSpec((1, tm, width), lambda i: (i // spt, i % spt, 0))
    heads = pl.BlockSpec((1, B_HEADS, tm, LANES), lambda i: (i // spt, 0, i % spt, 0))
    sds = lambda *shape: jax.ShapeDtypeStruct(shape, BF16)
    head_major = sds(bsz, B_HEADS, s_len, LANES)

    def body(x_ref, g_ref, w_ref, gain_ref, seg_ref, qa_ref, ka_ref, va_ref, qb_ref, kb_ref, vb_ref):
        _attn_in_kernel(x_ref.at[0], g_ref, w_ref, gain_ref, seg_ref,
                        qa_ref.at[0], ka_ref.at[0], va_ref.at[0], qb_ref, kb_ref, vb_ref)

    return pl.pallas_call(
        body,
        grid=(bsz * spt,),
        in_specs=[
            flat(D_MODEL),
            _const_spec((1, D_MODEL)),
            _const_spec((D_MODEL, PROJ0_W)),
            _const_spec((1, PROJ0_W)),
            _const_spec((MXU_COLS, MXU_COLS)),
        ],
        out_specs=[flat(512), flat(LANES), flat(LANES), heads, heads, heads],
        out_shape=[sds(bsz, s_len, 512), sds(bsz, s_len, LANES), sds(bsz, s_len, LANES),
                   head_major, head_major, head_major],
        compiler_params=_cparams(("parallel",)),
        name="attn_in",
    )(x, g, w, gain, seg)


WIN_COLS = A_Q_HEADS * BLOCK


def _win_attn_kernel(q_ref, kp_ref, kc_ref, kn_ref, vp_ref, vc_ref, vn_ref, bias_ref, sink_ref, o_ref, *, qb):
    step = pl.program_id(1)
    nb = pl.num_programs(1) * qb
    low = lax.broadcasted_iota(jnp.int32, (1, LANES), 1) < HEAD_DIM
    top = lax.broadcasted_iota(jnp.int32, (LANES, 1), 0) < HEAD_DIM
    sink = sink_ref[...]

    def piece(prev_ref, cur_ref, next_ref, i):
        if i == 0:
            return prev_ref[0]
        if i == qb + 1:
            return next_ref[0]
        return cur_ref[0, (i - 1) * BLOCK:i * BLOCK, :]

    def scores(b):
        kcat = jnp.concatenate([piece(kp_ref, kc_ref, kn_ref, b + i) for i in range(3)], axis=0)
        q = q_ref[0, b * BLOCK:(b + 1) * BLOCK, :]
        zero = jnp.zeros((BLOCK, LANES), BF16)
        halves = []
        for j in range(A_Q_HEADS // 2):
            slab = q[:, j * LANES:(j + 1) * LANES]
            halves += [jnp.where(low, slab, zero), jnp.where(low, zero, slab)]
        qcat = jnp.concatenate(halves, axis=0)
        return lax.dot_general(kcat, qcat, _NT, preferred_element_type=F32)

    def weights(b, s_t):
        n = step * qb + b
        edge = jnp.where(n == 0, 0, jnp.where(n == nb - 1, 2, 1))
        s_t = s_t + bias_ref[edge]
        m = jnp.maximum(jnp.max(s_t, axis=0, keepdims=True), sink)
        p = jnp.exp2(s_t - m)
        den = jnp.sum(p, axis=0, keepdims=True) + jnp.exp2(sink - m)
        return p.astype(BF16), den

    def values(b, p, den):
        vcat = jnp.concatenate([piece(vp_ref, vc_ref, vn_ref, b + i) for i in range(3)], axis=0)
        return lax.dot_general(vcat, p, _TN, preferred_element_type=F32) / den

    s_all = [scores(b) for b in range(qb)]
    w_all = [weights(b, s_all[b]) for b in range(qb)]
    o_all = [values(b, *w_all[b]) for b in range(qb)]
    for b in range(qb):
        for j in range(A_Q_HEADS // 2):
            c0 = 2 * j * BLOCK
            slab_t = jnp.where(top, o_all[b][:, c0:c0 + BLOCK], o_all[b][:, c0 + BLOCK:c0 + 2 * BLOCK])
            o_ref[0, b * BLOCK:(b + 1) * BLOCK, j * LANES:(j + 1) * LANES] = slab_t.T.astype(BF16)


def _win_attn(qa, ka, va, sink_row, bias, qb):
    bsz, s_len, _ = qa.shape
    nb = s_len // BLOCK
    assert nb % qb == 0 and nb >= 2
    kv_specs = [
        pl.BlockSpec((1, BLOCK, LANES), lambda b, t: (b, jnp.maximum(t * qb - 1, 0), 0)),
        pl.BlockSpec((1, qb * BLOCK, LANES), lambda b, t: (b, t, 0)),
        pl.BlockSpec((1, BLOCK, LANES), lambda b, t: (b, jnp.minimum((t + 1) * qb, nb - 1), 0)),
    ]
    return pl.pallas_call(
        functools.partial(_win_attn_kernel, qb=qb),
        grid=(bsz, nb // qb),
        in_specs=[pl.BlockSpec((1, qb * BLOCK, 512), lambda b, t: (b, t, 0))] + kv_specs + kv_specs
                 + [pl.BlockSpec(memory_space=pltpu.VMEM), pl.BlockSpec(memory_space=pltpu.VMEM)],
        out_specs=pl.BlockSpec((1, qb * BLOCK, 512), lambda b, t: (b, t, 0)),
        out_shape=jax.ShapeDtypeStruct((bsz, s_len, 512), BF16),
        compiler_params=_cparams(("parallel", "parallel")),
        name="win_attn",
    )(qa, ka, ka, ka, va, va, va, bias, sink_row)


_NT = (((1,), (1,)), ((), ()))
_TN = (((0,), (0,)), ((), ()))


def _split_maps(q, qz_ref):
    lane = lax.broadcasted_iota(jnp.int32, (1, LANES), 1)
    zero = jnp.zeros_like(q)
    qz_ref[0] = jnp.where(lane < HEAD_DIM, q, zero)
    qz_ref[1] = jnp.where(lane < HEAD_DIM, zero, q)


def _diff_out(acc0, l0, acc1, l1, lam_ref, g_ref, lam_init):
    lam = lam_ref[...]
    lam_full = (jnp.exp(jnp.sum(lam[0:1] * lam[1:2], axis=-1, keepdims=True))
                - jnp.exp(jnp.sum(lam[2:3] * lam[3:4], axis=-1, keepdims=True)) + lam_init)
    o = acc0 / l0 - lam_full * (acc1 / l1)
    ms = jnp.mean(o * o, axis=0, keepdims=True)
    y = o * lax.rsqrt(ms + EPS) * g_ref[...] * (1.0 - lam_init)
    return y.T.astype(BF16)


def _diff_attn_bounded_kernel(q_ref, k_ref, v_ref, bias_ref, lam_ref, g_ref, o_ref,
                              qz_ref, s_ref, l_ref, acc_ref, *, lam_init, tile, unroll, qtiles):
    qstep = pl.program_id(2)
    nk = k_ref.shape[1] // tile
    first_map = lax.broadcasted_iota(jnp.int32, (LANES, 1), 0) < HEAD_DIM
    for qt in range(qtiles):
        q_t = q_ref[0, qt * tile:(qt + 1) * tile, :].astype(F32).T
        qz_ref[qt, 0] = jnp.where(first_map, q_t, 0.0).astype(BF16)
        qz_ref[qt, 1] = jnp.where(first_map, 0.0, q_t).astype(BF16)
    l_ref[...] = jnp.zeros(l_ref.shape, F32)
    acc_ref[...] = jnp.zeros(acc_ref.shape, F32)

    def rows(ki):
        return ki * tile if isinstance(ki, int) else pl.multiple_of(ki * tile, tile)

    def produce(qt, ki, buf):
        k = k_ref[0, pl.ds(rows(ki), tile), :]
        for c in range(2):
            s_ref[buf, c] = jnp.dot(k, qz_ref[qt, c], preferred_element_type=F32)

    def consume(qt, ki, buf):
        v = v_ref[0, pl.ds(rows(ki), tile), :]
        bias = bias_ref[0, jnp.clip(ki - (qstep * qtiles + qt) + 2, 0, 4)]
        for c in range(2):
            p = jnp.exp2(s_ref[buf, c] + bias)
            l_ref[qt, c] += jnp.sum(p.reshape(tile // 8, 8, tile), axis=0)
            acc_ref[qt, c] += lax.dot_general(v, p.astype(BF16), _TN, preferred_element_type=F32)

    def finish(qt):
        l0 = jnp.sum(l_ref[qt, 0], axis=0, keepdims=True)
        l1 = jnp.sum(l_ref[qt, 1], axis=0, keepdims=True)
        o_ref[0, qt * tile:(qt + 1) * tile, :] = _diff_out(acc_ref[qt, 0], l0, acc_ref[qt, 1], l1,
                                                           lam_ref, g_ref, lam_init)

    if qtiles > 1:
        steps = [(qt, ki) for qt in range(qtiles) for ki in range(nk)]
        produce(*steps[0], 0)
        for i, (qt, ki) in enumerate(steps):
            if i + 1 < len(steps):
                produce(*steps[i + 1], (i + 1) % 2)
            consume(qt, ki, i % 2)
            if ki == nk - 1:
                finish(qt)
        return

    def group(first, last_group):
        for t in range(unroll):
            if not (last_group and t == unroll - 1):
                produce(0, first + t + 1, (t + 1) % 2)
            consume(0, first + t, t % 2)

    def body(j, carry):
        group(j * unroll, False)
        return carry

    produce(0, 0, 0)
    lax.fori_loop(0, nk // unroll - 1, body, 0)
    group(nk - unroll, True)
    finish(0)


DIFF_STEPS = 16


def _diff_attn_bounded(qb, kb, vb, bias_t, lam, g_col, lam_init, tile):
    bsz, _, s_len, _ = qb.shape
    nk = s_len // tile
    unroll = next(u for u in (8, 4, 2) if nk % u == 0)
    qtiles = max(1, min(DIFF_STEPS // nk, nk))
    assert nk % qtiles == 0
    kernel = functools.partial(_diff_attn_bounded_kernel, lam_init=lam_init, tile=tile, unroll=unroll,
                               qtiles=qtiles)

    def body(q_ref, k_ref, v_ref, bias_ref, lam_ref, g_ref, o_ref, *scratch):
        kernel(q_ref.at[0], k_ref.at[0], v_ref.at[0], bias_ref, lam_ref, g_ref, o_ref.at[0], *scratch)

    q_rows = qtiles * tile
    return pl.pallas_call(
        body,
        grid=(bsz, B_HEADS, s_len // q_rows),
        in_specs=[
            pl.BlockSpec((1, 1, q_rows, LANES), lambda b, h, qi: (b, h, qi, 0)),
            pl.BlockSpec((1, 1, s_len, LANES), lambda b, h, qi: (b, h, 0, 0)),
            pl.BlockSpec((1, 1, s_len, LANES), lambda b, h, qi: (b, h, 0, 0)),
            pl.BlockSpec((1, 5, tile, tile), lambda b, h, qi: (h, 0, 0, 0)),
            _const_spec((4, HEAD_DIM)),
            _const_spec((2 * HEAD_DIM, 1)),
        ],
        out_specs=pl.BlockSpec((1, 1, q_rows, LANES), lambda b, h, qi: (b, h, qi, 0)),
        out_shape=jax.ShapeDtypeStruct((bsz, B_HEADS, s_len, LANES), BF16),
        scratch_shapes=[
            pltpu.VMEM((qtiles, 2, LANES, tile), BF16),
            pltpu.VMEM((2, 2, tile, tile), F32),
            pltpu.VMEM((qtiles, 2, 8, tile), F32),
            pltpu.VMEM((qtiles, 2, 2 * HEAD_DIM, tile), F32),
        ],
        compiler_params=_cparams(("parallel", "parallel", "parallel")),
        name="diff_attn",
    )(qb, kb, vb, bias_t, lam, g_col)


def _diff_attn_online_kernel(far_ref, q_ref, k_ref, v_ref, bias_ref, lam_ref, g_ref, o_ref,
                             qz_ref, m_ref, l_ref, acc_ref, *, lam_init):
    h = pl.program_id(1)
    qi = pl.program_id(2)
    ki = pl.program_id(3)
    nk = pl.num_programs(3)

    @pl.when(ki == 0)
    def _init():
        _split_maps(q_ref[0], qz_ref)
        m_ref[...] = jnp.full(m_ref.shape, NEG, F32)
        l_ref[...] = jnp.zeros(l_ref.shape, F32)
        acc_ref[...] = jnp.zeros(acc_ref.shape, F32)

    delta = ki - qi

    def step(c, bias_tile, bias_row):
        s_t = lax.dot_general(k_ref[0], qz_ref[c], _NT, preferred_element_type=F32)
        if bias_tile is not None:
            s_t = s_t + bias_tile
        m_old = m_ref[c]
        m_new = jnp.maximum(m_old, jnp.max(s_t, axis=0, keepdims=True) + bias_row)
        alpha = jnp.exp2(m_old - m_new)
        p = jnp.exp2(s_t - (m_new - bias_row))
        l_ref[c] = alpha * l_ref[c] + jnp.sum(p, axis=0, keepdims=True)
        pv = lax.dot_general(v_ref[0], p.astype(BF16), _TN, preferred_element_type=F32)
        acc_ref[c] = alpha * acc_ref[c] + pv
        m_ref[c] = m_new

    @pl.when(jnp.abs(delta) <= 1)
    def _near():
        bias_tile = bias_ref[0, delta + 1]
        for c in range(2):
            step(c, bias_tile, 0.0)

    @pl.when(jnp.abs(delta) > 1)
    def _far():
        side = jnp.where(delta < 0, far_ref[h, 0], far_ref[h, 1])
        for c in range(2):
            step(c, None, side)

    @pl.when(ki == nk - 1)
    def _finish():
        o_ref[0] = _diff_out(acc_ref[0], l_ref[0], acc_ref[1], l_ref[1], lam_ref, g_ref, lam_init)


def _diff_attn_online(qb, kb, vb, far, bias_t, lam, g_col, lam_init, tile):
    bsz, _, s_len, _ = qb.shape
    nt = s_len // tile
    kernel = functools.partial(_diff_attn_online_kernel, lam_init=lam_init)

    def body(far_ref, q_ref, k_ref, v_ref, bias_ref, lam_ref, g_ref, o_ref, *scratch):
        kernel(far_ref, q_ref.at[0], k_ref.at[0], v_ref.at[0], bias_ref, lam_ref, g_ref, o_ref.at[0], *scratch)

    return pl.pallas_call(
        body,
        grid=(bsz, B_HEADS, nt, nt),
        in_specs=[
            pl.BlockSpec(memory_space=pltpu.SMEM),
            pl.BlockSpec((1, 1, tile, LANES), lambda b, h, qi, ki: (b, h, qi, 0)),
            pl.BlockSpec((1, 1, tile, LANES), lambda b, h, qi, ki: (b, h, ki, 0)),
            pl.BlockSpec((1, 1, tile, LANES), lambda b, h, qi, ki: (b, h, ki, 0)),
            pl.BlockSpec((1, 3, tile, tile), lambda b, h, qi, ki: (h, 0, 0, 0)),
            _const_spec((4, HEAD_DIM)),
            _const_spec((2 * HEAD_DIM, 1)),
        ],
        out_specs=pl.BlockSpec((1, 1, tile, LANES), lambda b, h, qi, ki: (b, h, qi, 0)),
        out_shape=jax.ShapeDtypeStruct((bsz, B_HEADS, s_len, LANES), BF16),
        scratch_shapes=[
            pltpu.VMEM((2, tile, LANES), BF16),
            pltpu.VMEM((2, 1, tile), F32),
            pltpu.VMEM((2, 1, tile), F32),
            pltpu.VMEM((2, 2 * HEAD_DIM, tile), F32),
        ],
        compiler_params=_cparams(("parallel", "parallel", "parallel", "arbitrary")),
        name="diff_attn_online",
    )(far, qb, kb, vb, bias_t, lam, g_col)


def _diff_attn(qb, kb, vb, db, lam, g_col, lam_init, tile):
    bounded = lambda: _diff_attn_bounded(qb, kb, vb, db["shifted"], lam, g_col, lam_init, tile)
    online = lambda: _diff_attn_online(qb, kb, vb, db["far"], db["near"], lam, g_col, lam_init, tile)
    return lax.cond(db["bounded_ok"], bounded, online)


FFN_CHUNKS = ((0, 1536), (1536, FFN_HIDDEN))


def _mix_ffn_kernel(x_ref, a_ref, b_ref, wa_ref, wb_ref, g_ref, wg_ref, wu_ref, wd_ref, o_ref):
    if len(b_ref.shape) == 4:
        b = jnp.concatenate([b_ref[0, hd] for hd in range(b_ref.shape[1])], axis=-1)
    else:
        b = b_ref[0]
    mix = (jnp.dot(a_ref[0], wa_ref[...], preferred_element_type=F32)
           + jnp.dot(b, wb_ref[...], preferred_element_type=F32))
    x1 = x_ref[0] + mix
    h = _rms(x1, g_ref[...]).astype(BF16)
    gates = [jnp.dot(h, wg_ref[:, lo:hi], preferred_element_type=F32) for lo, hi in FFN_CHUNKS]
    ups = [jnp.dot(h, wu_ref[:, lo:hi], preferred_element_type=F32) for lo, hi in FFN_CHUNKS]
    down = None
    for (lo, hi), gate, up in zip(FFN_CHUNKS, gates, ups):
        act = (gate * jax.nn.sigmoid(gate) * up).astype(BF16)
        part = jnp.dot(act, wd_ref[lo:hi, :], preferred_element_type=F32)
        down = part if down is None else down + part
    o_ref[0] = x1 + down


def _mix_ffn(x, a, b, wa, wb, g, wg, wu, wd, tm):
    bsz, s_len, _ = x.shape
    spt = s_len // tm
    row_spec = lambda w: pl.BlockSpec((1, tm, w), lambda i: (i // spt, i % spt, 0))
    if b.ndim == 4:
        b_spec = pl.BlockSpec((1, b.shape[1], tm, LANES), lambda i: (i // spt, 0, i % spt, 0))
    else:
        b_spec = row_spec(b.shape[-1])
    resident = pl.BlockSpec(memory_space=pltpu.VMEM)
    return pl.pallas_call(
        _mix_ffn_kernel,
        grid=(bsz * spt,),
        in_specs=[row_spec(D_MODEL), row_spec(a.shape[-1]), b_spec] + [resident] * 6,
        out_specs=row_spec(D_MODEL),
        out_shape=jax.ShapeDtypeStruct(x.shape, F32),
        compiler_params=_cparams(("parallel",)),
        name="mix_ffn",
    )(x, a, b, wa, wb, g, wg, wu, wd)


def _conv_in_kernel(x_ref, g_ref, w_ref, gb_ref, gx_ref, glu_ref):
    h = _rms(x_ref[0], g_ref[...]).astype(BF16)
    c = C_WIDTH
    gb, gc, xc, a, gate = [jnp.dot(h, w_ref[:, i * c:(i + 1) * c], preferred_element_type=F32) for i in range(5)]
    gb_ref[0] = gb.astype(BF16)
    gx_ref[0] = (gc * xc).astype(BF16)
    glu_ref[0] = (a * jax.nn.sigmoid(gate)).astype(BF16)


def _conv_in(x, g, w, tm):
    bsz, s_len, _ = x.shape
    spt = s_len // tm
    row_spec = lambda width: pl.BlockSpec((1, tm, width), lambda i: (i // spt, i % spt, 0))
    out = jax.ShapeDtypeStruct((bsz, s_len, C_WIDTH), BF16)
    return pl.pallas_call(
        _conv_in_kernel,
        grid=(bsz * spt,),
        in_specs=[row_spec(D_MODEL), _const_spec((1, D_MODEL)), _const_spec((D_MODEL, w.shape[1]))],
        out_specs=[row_spec(C_WIDTH)] * 3,
        out_shape=[out] * 3,
        compiler_params=_cparams(("parallel",)),
        name="conv_in",
    )(x, g, w)


CONV_CHUNK = 32
SUBLANES = 8


def _conv_mix_kernel(gb_ref, gx_ref, gxp_ref, gxn_ref, u_ref, up_ref, un_ref,
                     scw_ref, dww_ref, dwb_ref, lng_ref, lnb_ref, yc_ref, yu_ref, xs_ref, u_ref_f32, *, ts):
    t = pl.program_id(1)
    nt = pl.num_programs(1)
    has_prev = (t > 0).astype(F32)
    has_next = (t < nt - 1).astype(F32)
    n_shifted = ts + 2 * HALO - SUBLANES

    def taps(width):
        return [(j,) + divmod(HALO - width // 2 + j, SUBLANES)[::-1] for j in range(width)]

    def fill(cur, prev, nxt, width):
        xs_ref[0, 0:HALO, :] = prev[0].astype(F32) * has_prev
        xs_ref[0, HALO:HALO + ts, :] = cur[0].astype(F32)
        xs_ref[0, HALO + ts:2 * HALO + ts, :] = nxt[0].astype(F32) * has_next
        for r in sorted({r for _, r, _ in taps(width)} - {0}):
            xs_ref[r, 0:n_shifted, :] = xs_ref[0, r:r + n_shifted, :]

    def conv(w_ref, width, s0):
        acc = None
        groups = CONV_CHUNK // SUBLANES
        for r in sorted({r for _, r, _ in taps(width)}):
            mine = [(j, q) for j, r_j, q in taps(width) if r_j == r]
            q_lo, q_hi = min(q for _, q in mine), max(q for _, q in mine)
            span = xs_ref[r, pl.ds(s0 + SUBLANES * q_lo, CONV_CHUNK + SUBLANES * (q_hi - q_lo)), :]
            span = span.reshape(groups + q_hi - q_lo, SUBLANES, -1)
            for j, q in mine:
                term = span[q - q_lo:q - q_lo + groups] * w_ref[j]
                acc = term if acc is None else acc + term
        return acc.reshape(CONV_CHUNK, -1)

    def chunks(body):
        def step(c, carry):
            body(pl.multiple_of(c * CONV_CHUNK, CONV_CHUNK))
            return carry
        lax.fori_loop(0, ts // CONV_CHUNK, step, 0, unroll=True)

    fill(gx_ref, gxp_ref, gxn_ref, SHORT_CONV)

    def short(s0):
        out = pl.ds(s0, CONV_CHUNK)
        yc_ref[0, out, :] = (gb_ref[0, out, :].astype(F32) * conv(scw_ref, SHORT_CONV, s0)).astype(BF16)
    chunks(short)

    fill(u_ref, up_ref, un_ref, CONF_CONV)

    def conf(s0):
        u_ref_f32[pl.ds(s0, CONV_CHUNK), :] = conv(dww_ref, CONF_CONV, s0) + dwb_ref[...]
    chunks(conf)

    u = u_ref_f32[...]
    mu = jnp.mean(u, axis=-1, keepdims=True)
    uc = u - mu
    var = jnp.mean(uc * uc, axis=-1, keepdims=True)
    y = uc * lax.rsqrt(var + EPS) * lng_ref[...] + lnb_ref[...]
    yu_ref[0] = (y * jax.nn.sigmoid(y)).astype(BF16)


def _conv_mix(gb, gx, glu, scw, dww, dwb, lng, lnb, ts):
    bsz, s_len, _ = gb.shape
    nt = s_len // ts
    r = ts // HALO
    nh = s_len // HALO
    cur = pl.BlockSpec((1, ts, 512), lambda b, t: (b, t, 0))
    prev = pl.BlockSpec((1, HALO, 512), lambda b, t: (b, jnp.maximum(t * r - 1, 0), 0))
    nxt = pl.BlockSpec((1, HALO, 512), lambda b, t: (b, jnp.minimum((t + 1) * r, nh - 1), 0))
    out_spec = pl.BlockSpec((1, ts, 512), lambda b, t: (b, t, 0))
    return pl.pallas_call(
        functools.partial(_conv_mix_kernel, ts=ts),
        grid=(bsz, nt),
        in_specs=[cur, cur, prev, nxt, cur, prev, nxt,
                  _const_spec((SHORT_CONV, SUBLANES, C_WIDTH)), _const_spec((CONF_CONV, SUBLANES, D_WIDTH)),
                  _const_spec((1, D_WIDTH)), _const_spec((1, D_WIDTH)), _const_spec((1, D_WIDTH))],
        out_specs=[out_spec, out_spec],
        out_shape=[jax.ShapeDtypeStruct((bsz, s_len, C_WIDTH), BF16),
                   jax.ShapeDtypeStruct((bsz, s_len, D_WIDTH), BF16)],
        scratch_shapes=[pltpu.VMEM((SUBLANES, ts + 2 * HALO, 512), F32), pltpu.VMEM((ts, D_WIDTH), F32)],
        compiler_params=_cparams(("parallel", "parallel")),
        name="conv_mix",
    )(gb, gx, gx, gx, glu, glu, glu, scw, dww, dwb, lng, lnb)


def _rel_bucket(rel):
    half = NUM_BUCKETS // 2
    max_exact = half // 2
    n = jnp.abs(rel)
    large = max_exact + (jnp.log(jnp.maximum(n, 1).astype(F32) / max_exact)
                         / math.log(MAX_DISTANCE / max_exact) * (half - max_exact)).astype(jnp.int32)
    large = jnp.minimum(large, half - 1)
    return jnp.where(rel > 0, half, 0) + jnp.where(n < max_exact, n, large)


def _toeplitz(u, rows, cols):
    length = u.shape[-1]
    flat = jnp.tile(u, (1,) * (u.ndim - 1) + (rows,))[..., :rows * (length - 1)]
    return flat.reshape(u.shape[:-1] + (rows, length - 1))[..., :cols]


LOG2E = math.log2(math.e)


def _win_bias(rel_bias):
    length = 4 * BLOCK
    n = jnp.arange(length)
    rel = jnp.where(n < BLOCK, -n, length - n) - BLOCK
    vec = rel_bias[:, :A_Q_HEADS][_rel_bucket(rel)].astype(F32).T * LOG2E
    vec = jnp.where((jnp.abs(rel) <= WINDOW)[None], vec, NEG)
    base = _toeplitz(vec, 3 * BLOCK, BLOCK)[jnp.array(WIN_HEAD_ORDER)]
    mid = base.transpose(1, 0, 2).reshape(3 * BLOCK, A_Q_HEADS * BLOCK)
    key_block = (jnp.arange(3 * BLOCK) // BLOCK)[:, None]
    return jnp.stack([jnp.where(key_block == 0, NEG, mid), mid, jnp.where(key_block == 2, NEG, mid)], axis=0)
MIN_EXP2_ARG = -120.0


def _diff_bias(rel_bias, b_qn, b_kn, tile):
    table = rel_bias[:, A_Q_HEADS:].astype(F32) * LOG2E
    length = 2 * tile
    n = jnp.arange(length)
    k_minus_q = jnp.where(n < tile, -n, length - n)
    rel = jnp.clip(k_minus_q[None, :] + jnp.array([-tile, 0, tile])[:, None], 1 - length, length - 1)
    near = _toeplitz(table[_rel_bucket(rel)].transpose(2, 0, 1), tile, tile)
    far = table[_rel_bucket(jnp.array([-length, length]))].T
    s_max = 1.02 * LOG2E * math.sqrt(HEAD_DIM) * jnp.max(jnp.abs(b_qn * b_kn))
    b_max, b_min = jnp.max(table, axis=0), jnp.min(table, axis=0)
    bound = s_max + b_max
    const = lambda col: jnp.broadcast_to((far[:, col] - bound)[:, None, None, None], (B_HEADS, 1, tile, tile))
    shifted = jnp.concatenate([const(0), near - bound[:, None, None, None], const(1)], axis=1)
    bounded_ok = jnp.all(-2.0 * s_max - (b_max - b_min) > MIN_EXP2_ARG)
    return {"near": near, "far": far, "shifted": shifted, "bounded_ok": bounded_ok}


def _by_win_head(m, axis):
    take = lambda h: lax.slice_in_dim(m, h * HEAD_DIM, (h + 1) * HEAD_DIM, axis=axis)
    return jnp.concatenate([take(h) for h in WIN_HEAD_ORDER], axis=axis)


def _attn_in_params(w_in, a_qn, a_kn, b_qn, b_kn):
    d = HEAD_DIM
    k0 = A_Q_HEADS * d
    w = jnp.concatenate([_by_win_head(w_in[:, :k0], 1), w_in[:, k0:]], axis=1).astype(BF16)
    scale = HEAD_DIM ** -0.5 * LOG2E
    ones = lambda n: jnp.ones((n,), F32)
    gain = jnp.concatenate([
        jnp.tile(a_qn, A_Q_HEADS) * scale, jnp.tile(a_kn, A_KV_HEADS), ones(A_KV_HEADS * d),
        jnp.tile(b_qn, 2 * B_HEADS) * scale, jnp.tile(b_kn, 2 * B_HEADS), ones(2 * B_HEADS * d),
    ]).astype(F32)[None, :]
    seg_id = jnp.arange(MXU_COLS) // d
    seg = jnp.where(seg_id[:, None] == seg_id[None, :], 1.0 / d, 0.0).astype(BF16)
    return w, gain, seg


def _trunk(x, p, *, tm, tile, ts, qb):
    qa, ka, va, qb_, kb, vb = _attn_in(x, p["mix_g"][0], p["attn_w"], p["attn_gain"], p["seg"], tm)
    ya = _win_attn(qa, ka, va, p["sink_row"], p["win_bias"], qb)
    yb = _diff_attn(qb_, kb, vb, p["diff_bias"][tile], p["lam"], p["subln"], p["lam_init"], tile)
    x = _mix_ffn(x, ya, yb, p["attn_wo_a"], p["attn_wo_b"],
                 p["ffn_g"][0], p["wg"][0], p["wu"][0], p["wd"][0], tm)

    gb, gx, glu = _conv_in(x, p["mix_g"][1], p["conv_w"], tm)
    yc, yu = _conv_mix(gb, gx, glu, p["scw"], p["dww"], p["dwb"], p["lng"], p["lnb"], ts)
    return _mix_ffn(x, yc, yu, p["conv_wo_a"], p["conv_wo_b"],
                    p["ffn_g"][1], p["wg"][1], p["wu"][1], p["wd"][1], tm)


def _prepare(rel_bias, mix_norm, ffn_norm, w_gate, w_up, w_down, attn_w_in, attn_w_out, a_q_norm, a_k_norm,
             a_sink, b_q_norm, b_k_norm, b_lambda, b_subln, conv_w_in, conv_w_out, short_conv_w, conf_dw_w,
             conf_dw_b, conf_ln_g, conf_ln_b, tiles):
    attn_w, attn_gain, seg = _attn_in_params(attn_w_in[0], a_q_norm[0], a_k_norm[0], b_q_norm[0], b_k_norm[0])
    half = A_Q_HEADS * HEAD_DIM
    return {
        "mix_g": [mix_norm[l][None, :].astype(F32) for l in range(2)],
        "ffn_g": [ffn_norm[l][None, :].astype(F32) for l in range(2)],
        "wg": [w_gate[l].astype(BF16) for l in range(2)],
        "wu": [w_up[l].astype(BF16) for l in range(2)],
        "wd": [w_down[l].astype(BF16) for l in range(2)],
        "attn_w": attn_w, "attn_gain": attn_gain, "seg": seg,
        "attn_wo_a": _by_win_head(attn_w_out[0][:half], 0).astype(BF16),
        "attn_wo_b": attn_w_out[0][half:].astype(BF16),
        "sink_row": jnp.repeat(a_sink[0].astype(F32)[jnp.array(WIN_HEAD_ORDER)] * LOG2E, BLOCK)[None, :],
        "win_bias": _win_bias(rel_bias),
        "diff_bias": {t: _diff_bias(rel_bias, b_q_norm[0], b_k_norm[0], t) for t in tiles},
        "lam": b_lambda[0].astype(F32),
        "subln": b_subln[0].astype(F32)[:, None],
        "lam_init": 0.8 - 0.6 * math.exp(-0.3 * 0),
        "conv_w": conv_w_in[0].astype(BF16),
        "conv_wo_a": conv_w_out[0][:C_WIDTH].astype(BF16), "conv_wo_b": conv_w_out[0][C_WIDTH:].astype(BF16),
        "scw": jnp.broadcast_to(short_conv_w[0].astype(F32)[:, None, :], (SHORT_CONV, SUBLANES, C_WIDTH)),
        "dww": jnp.broadcast_to(conf_dw_w[0].astype(F32)[:, None, :], (CONF_CONV, SUBLANES, D_WIDTH)),
        "dwb": conf_dw_b[0][None, :].astype(F32),
        "lng": conf_ln_g[0][None, :].astype(F32), "lnb": conf_ln_b[0][None, :].astype(F32),
    }


def _tiling(s_len):
    return dict(tm=min(1024, s_len), tile=min(512, s_len), ts=min(512, s_len), qb=8)


def kernel(x_prompt, x_sample, rel_bias, mix_norm, ffn_norm, w_gate, w_up, w_down, attn_w_in, attn_w_out,
           a_q_norm, a_k_norm, a_sink, b_q_norm, b_k_norm, b_lambda, b_subln, conv_w_in, conv_w_out,
           short_conv_w, conf_dw_w, conf_dw_b, conf_ln_g, conf_ln_b):
    tp, tsm = _tiling(x_prompt.shape[1]), _tiling(x_sample.shape[1])
    p = _prepare(rel_bias, mix_norm, ffn_norm, w_gate, w_up, w_down, attn_w_in, attn_w_out, a_q_norm,
                 a_k_norm, a_sink, b_q_norm, b_k_norm, b_lambda, b_subln, conv_w_in, conv_w_out,
                 short_conv_w, conf_dw_w, conf_dw_b, conf_ln_g, conf_ln_b, {tp["tile"], tsm["tile"]})
    return (_trunk(x_prompt, p, **tp), _trunk(x_sample, p, **tsm))
```

```python
import functools
import math

import jax
import jax.numpy as jnp
from jax import lax
from jax.experimental import pallas as pl
from jax.experimental.pallas import tpu as pltpu

D_MODEL = 1024
HEAD_DIM = 64
A_Q_HEADS = 8
A_KV_HEADS = 2
WINDOW = 128
BLOCK = 128
B_HEADS = 4
NUM_BUCKETS = 32
MAX_DISTANCE = 128
C_WIDTH = 512
D_WIDTH = 512
SHORT_CONV = 3
CONF_CONV = 31
FFN_HIDDEN = 2816
EPS = 1e-6
NEG = -1e30

LANES = 128
MXU_COLS = 256
VMEM_LIMIT_BYTES = 56 * 1024 * 1024

BF16 = jnp.bfloat16
F32 = jnp.float32

PROJ0_W = 2304
NORM_CHUNKS0 = ("all", "all", "low", "all", "all", "all", "all", None, None)
WIN_HEAD_ORDER = (0, 4, 1, 5, 2, 6, 3, 7)

HALO = 16


def _cparams(sem):
    return pltpu.CompilerParams(dimension_semantics=sem, vmem_limit_bytes=VMEM_LIMIT_BYTES)


def _const_spec(shape):
    nd = len(shape)
    return pl.BlockSpec(shape, lambda *_: (0,) * nd)


def _rms(x, g):
    ms = jnp.mean(x * x, axis=-1, keepdims=True)
    return x * lax.rsqrt(ms + EPS) * g


def _attn_in_kernel(x_ref, g_ref, w_ref, gain_ref, seg_ref, qa_ref, ka_ref, va_ref, qb_ref, kb_ref, vb_ref):
    half = MXU_COLS // 2

    def store(c, y):
        if c < 2:
            qa_ref[:, c * MXU_COLS:(c + 1) * MXU_COLS] = y
        elif c == 2:
            ka_ref[...] = y[:, :half]
            va_ref[...] = y[:, half:]
        else:
            ref, first = ((qb_ref, 3), (kb_ref, 5), (vb_ref, 7))[(c - 3) // 2]
            ref[0, 2 * (c - first)] = y[:, :half]
            ref[0, 2 * (c - first) + 1] = y[:, half:]

    h = _rms(x_ref[...], g_ref[...]).astype(BF16)
    seg = seg_ref[...]
    low = lax.broadcasted_iota(jnp.int32, (1, MXU_COLS), 1) < MXU_COLS // 2
    chunk = lambda c: slice(c * MXU_COLS, (c + 1) * MXU_COLS)
    accs = [jnp.dot(h, w_ref[:, chunk(c)], preferred_element_type=F32) for c in range(len(NORM_CHUNKS0))]
    for c, normed in enumerate(NORM_CHUNKS0):
        acc = accs[c]
        if normed is not None:
            sq = acc * acc
            sq_hi = sq.astype(BF16)
            sq_lo = (sq - sq_hi.astype(F32)).astype(BF16)
            ms = (jnp.dot(sq_hi, seg, preferred_element_type=F32)
                  + jnp.dot(sq_lo, seg, preferred_element_type=F32))
            scale = lax.rsqrt(ms + EPS) * gain_ref[:, chunk(c)]
            acc = acc * (scale if normed == "all" else jnp.where(low, scale, 1.0))
        store(c, acc.astype(BF16))


def _attn_in(x, g, w, gain, seg, tm):
    bsz, s_len, _ = x.shape
    spt = s_len // tm
    flat = lambda width: pl.BlockSpec((1, tm, width), lambda i: (i // spt, i % spt, 0))
    heads = pl.BlockSpec((1, B_HEADS, tm, LANES), lambda i: (i // spt, 0, i % spt, 0))
    sds = lambda *shape: jax.ShapeDtypeStruct(shape, BF16)
    head_major = sds(bsz, B_HEADS, s_len, LANES)

    def body(x_ref, g_ref, w_ref, gain_ref, seg_ref, qa_ref, ka_ref, va_ref, qb_ref, kb_ref, vb_ref):
        _attn_in_kernel(x_ref.at[0], g_ref, w_ref, gain_ref, seg_ref,
                        qa_ref.at[0], ka_ref.at[0], va_ref.at[0], qb_ref, kb_ref, vb_ref)

    return pl.pallas_call(
        body,
        grid=(bsz * spt,),
        in_specs=[
            flat(D_MODEL),
            _const_spec((1, D_MODEL)),
            _const_spec((D_MODEL, PROJ0_W)),
            _const_spec((1, PROJ0_W)),
            _const_spec((MXU_COLS, MXU_COLS)),
        ],
        out_specs=[flat(512), flat(LANES), flat(LANES), heads, heads, heads],
        out_shape=[sds(bsz, s_len, 512), sds(bsz, s_len, LANES), sds(bsz, s_len, LANES),
                   head_major, head_major, head_major],
        compiler_params=_cparams(("parallel",)),
        name="attn_in",
    )(x, g, w, gain, seg)


WIN_COLS = A_Q_HEADS * BLOCK


def _win_attn_kernel(q_ref, kp_ref, kc_ref, kn_ref, vp_ref, vc_ref, vn_ref, bias_ref, sink_ref, o_ref, *,
                     qb, bounded):
    step = pl.program_id(1)
    nb = pl.num_programs(1) * qb
    low = lax.broadcasted_iota(jnp.int32, (1, LANES), 1) < HEAD_DIM
    top = lax.broadcasted_iota(jnp.int32, (LANES, 1), 0) < HEAD_DIM
    sink = sink_ref[...]

    def piece(prev_ref, cur_ref, next_ref, i):
        if i == 0:
            return prev_ref[0]
        if i == qb + 1:
            return next_ref[0]
        return cur_ref[0, (i - 1) * BLOCK:i * BLOCK, :]

    def scores(b):
        kcat = jnp.concatenate([piece(kp_ref, kc_ref, kn_ref, b + i) for i in range(3)], axis=0)
        q = q_ref[0, b * BLOCK:(b + 1) * BLOCK, :]
        zero = jnp.zeros((BLOCK, LANES), BF16)
        halves = []
        for j in range(A_Q_HEADS // 2):
            slab = q[:, j * LANES:(j + 1) * LANES]
            halves += [jnp.where(low, slab, zero), jnp.where(low, zero, slab)]
        qcat = jnp.concatenate(halves, axis=0)
        return lax.dot_general(kcat, qcat, _NT, preferred_element_type=F32)

    def weights(b, s_t):
        n = step * qb + b
        edge = jnp.where(n == 0, 0, jnp.where(n == nb - 1, 2, 1))
        s_t = s_t + bias_ref[edge]
        if bounded:
            p = jnp.exp2(s_t)
            den = jnp.sum(p, axis=0, keepdims=True) + jnp.exp2(sink)
        else:
            m = jnp.maximum(jnp.max(s_t, axis=0, keepdims=True), sink)
            p = jnp.exp2(s_t - m)
            den = jnp.sum(p, axis=0, keepdims=True) + jnp.exp2(sink - m)
        return p.astype(BF16), den

    def values(b, p, den):
        vcat = jnp.concatenate([piece(vp_ref, vc_ref, vn_ref, b + i) for i in range(3)], axis=0)
        return lax.dot_general(vcat, p, _TN, preferred_element_type=F32) / den

    s_all = [scores(b) for b in range(qb)]
    w_all = [weights(b, s_all[b]) for b in range(qb)]
    o_all = [values(b, *w_all[b]) for b in range(qb)]
    for b in range(qb):
        for j in range(A_Q_HEADS // 2):
            c0 = 2 * j * BLOCK
            slab_t = jnp.where(top, o_all[b][:, c0:c0 + BLOCK], o_all[b][:, c0 + BLOCK:c0 + 2 * BLOCK])
            o_ref[0, b * BLOCK:(b + 1) * BLOCK, j * LANES:(j + 1) * LANES] = slab_t.T.astype(BF16)


def _win_attn(qa, ka, va, wb, qb):
    fast = lambda: _win_attn_call(qa, ka, va, wb["sink_shifted"], wb["bias_shifted"], qb, True)
    exact_max = lambda: _win_attn_call(qa, ka, va, wb["sink"], wb["bias"], qb, False)
    return lax.cond(wb["bounded_ok"], fast, exact_max)


def _win_attn_call(qa, ka, va, sink_row, bias, qb, bounded):
    bsz, s_len, _ = qa.shape
    nb = s_len // BLOCK
    assert nb % qb == 0 and nb >= 2
    kv_specs = [
        pl.BlockSpec((1, BLOCK, LANES), lambda b, t: (b, jnp.maximum(t * qb - 1, 0), 0)),
        pl.BlockSpec((1, qb * BLOCK, LANES), lambda b, t: (b, t, 0)),
        pl.BlockSpec((1, BLOCK, LANES), lambda b, t: (b, jnp.minimum((t + 1) * qb, nb - 1), 0)),
    ]
    return pl.pallas_call(
        functools.partial(_win_attn_kernel, qb=qb, bounded=bounded),
        grid=(bsz, nb // qb),
        in_specs=[pl.BlockSpec((1, qb * BLOCK, 512), lambda b, t: (b, t, 0))] + kv_specs + kv_specs
                 + [pl.BlockSpec(memory_space=pltpu.VMEM), pl.BlockSpec(memory_space=pltpu.VMEM)],
        out_specs=pl.BlockSpec((1, qb * BLOCK, 512), lambda b, t: (b, t, 0)),
        out_shape=jax.ShapeDtypeStruct((bsz, s_len, 512), BF16),
        compiler_params=_cparams(("parallel", "parallel")),
        name="win_attn",
    )(qa, ka, ka, ka, va, va, va, bias, sink_row)


_NT = (((1,), (1,)), ((), ()))
_TN = (((0,), (0,)), ((), ()))


def _split_maps(q, qz_ref):
    lane = lax.broadcasted_iota(jnp.int32, (1, LANES), 1)
    zero = jnp.zeros_like(q)
    qz_ref[0] = jnp.where(lane < HEAD_DIM, q, zero)
    qz_ref[1] = jnp.where(lane < HEAD_DIM, zero, q)


def _diff_out(acc0, l0, acc1, l1, lam_ref, g_ref, lam_init):
    lam = lam_ref[...]
    lam_full = (jnp.exp(jnp.sum(lam[0:1] * lam[1:2], axis=-1, keepdims=True))
                - jnp.exp(jnp.sum(lam[2:3] * lam[3:4], axis=-1, keepdims=True)) + lam_init)
    o = acc0 / l0 - lam_full * (acc1 / l1)
    ms = jnp.mean(o * o, axis=0, keepdims=True)
    y = o * lax.rsqrt(ms + EPS) * g_ref[...] * (1.0 - lam_init)
    return y.T.astype(BF16)


def _diff_attn_bounded_kernel(q_ref, k_ref, v_ref, bias_ref, lam_ref, g_ref, o_ref,
                              qz_ref, s_ref, l_ref, acc_ref, *, lam_init, tile, qtiles):
    qstep = pl.program_id(2)
    nk = k_ref.shape[1] // tile
    first_map = lax.broadcasted_iota(jnp.int32, (LANES, 1), 0) < HEAD_DIM
    for qt in range(qtiles):
        q_t = q_ref[0, qt * tile:(qt + 1) * tile, :].astype(F32).T
        qz_ref[qt, 0] = jnp.where(first_map, q_t, 0.0).astype(BF16)
        qz_ref[qt, 1] = jnp.where(first_map, 0.0, q_t).astype(BF16)
    l_ref[...] = jnp.zeros(l_ref.shape, F32)
    acc_ref[...] = jnp.zeros(acc_ref.shape, F32)

    def rows(ki):
        return ki * tile if isinstance(ki, int) else pl.multiple_of(ki * tile, tile)

    def produce(qt, ki, buf):
        k = k_ref[0, pl.ds(rows(ki), tile), :]
        for c in range(2):
            s_ref[buf, c] = jnp.dot(k, qz_ref[qt, c], preferred_element_type=F32)

    def consume(qt, ki, buf):
        v = v_ref[0, pl.ds(rows(ki), tile), :]
        bias = bias_ref[0, jnp.clip(ki - (qstep * qtiles + qt) + 2, 0, 4)]
        for c in range(2):
            p = jnp.exp2(s_ref[buf, c] + bias)
            l_ref[qt, c] += jnp.sum(p.reshape(tile // 8, 8, tile), axis=0)
            acc_ref[qt, c] += lax.dot_general(v, p.astype(BF16), _TN, preferred_element_type=F32)

    def finish(qt):
        l0 = jnp.sum(l_ref[qt, 0], axis=0, keepdims=True)
        l1 = jnp.sum(l_ref[qt, 1], axis=0, keepdims=True)
        o_ref[0, qt * tile:(qt + 1) * tile, :] = _diff_out(acc_ref[qt, 0], l0, acc_ref[qt, 1], l1,
                                                           lam_ref, g_ref, lam_init)

    steps = [(qt, ki) for qt in range(qtiles) for ki in range(nk)]
    produce(*steps[0], 0)
    for i, (qt, ki) in enumerate(steps):
        if i + 1 < len(steps):
            produce(*steps[i + 1], (i + 1) % 2)
        consume(qt, ki, i % 2)
        if ki == nk - 1:
            finish(qt)


DIFF_STEPS = 16


def _diff_attn_bounded(qb, kb, vb, bias_t, lam, g_col, lam_init, tile):
    bsz, _, s_len, _ = qb.shape
    nk = s_len // tile
    assert nk <= DIFF_STEPS and DIFF_STEPS % nk == 0 and nk % 2 == 0
    qtiles = min(DIFF_STEPS // nk, nk)
    assert nk % qtiles == 0
    kernel = functools.partial(_diff_attn_bounded_kernel, lam_init=lam_init, tile=tile, qtiles=qtiles)

    def body(q_ref, k_ref, v_ref, bias_ref, lam_ref, g_ref, o_ref, *scratch):
        kernel(q_ref.at[0], k_ref.at[0], v_ref.at[0], bias_ref, lam_ref, g_ref, o_ref.at[0], *scratch)

    q_rows = qtiles * tile
    return pl.pallas_call(
        body,
        grid=(bsz, B_HEADS, s_len // q_rows),
        in_specs=[
            pl.BlockSpec((1, 1, q_rows, LANES), lambda b, h, qi: (b, h, qi, 0)),
            pl.BlockSpec((1, 1, s_len, LANES), lambda b, h, qi: (b, h, 0, 0)),
            pl.BlockSpec((1, 1, s_len, LANES), lambda b, h, qi: (b, h, 0, 0)),
            pl.BlockSpec((1, 5, tile, tile), lambda b, h, qi: (h, 0, 0, 0)),
            _const_spec((4, HEAD_DIM)),
            _const_spec((2 * HEAD_DIM, 1)),
        ],
        out_specs=pl.BlockSpec((1, 1, q_rows, LANES), lambda b, h, qi: (b, h, qi, 0)),
        out_shape=jax.ShapeDtypeStruct((bsz, B_HEADS, s_len, LANES), BF16),
        scratch_shapes=[
            pltpu.VMEM((qtiles, 2, LANES, tile), BF16),
            pltpu.VMEM((2, 2, tile, tile), F32),
            pltpu.VMEM((qtiles, 2, 8, tile), F32),
            pltpu.VMEM((qtiles, 2, 2 * HEAD_DIM, tile), F32),
        ],
        compiler_params=_cparams(("parallel", "parallel", "parallel")),
        name="diff_attn",
    )(qb, kb, vb, bias_t, lam, g_col)


def _diff_attn_online_kernel(far_ref, q_ref, k_ref, v_ref, bias_ref, lam_ref, g_ref, o_ref,
                             qz_ref, m_ref, l_ref, acc_ref, *, lam_init):
    h = pl.program_id(1)
    qi = pl.program_id(2)
    ki = pl.program_id(3)
    nk = pl.num_programs(3)

    @pl.when(ki == 0)
    def _init():
        _split_maps(q_ref[0], qz_ref)
        m_ref[...] = jnp.full(m_ref.shape, NEG, F32)
        l_ref[...] = jnp.zeros(l_ref.shape, F32)
        acc_ref[...] = jnp.zeros(acc_ref.shape, F32)

    delta = ki - qi

    def step(c, bias_tile, bias_row):
        s_t = lax.dot_general(k_ref[0], qz_ref[c], _NT, preferred_element_type=F32)
        if bias_tile is not None:
            s_t = s_t + bias_tile
        m_old = m_ref[c]
        m_new = jnp.maximum(m_old, jnp.max(s_t, axis=0, keepdims=True) + bias_row)
        alpha = jnp.exp2(m_old - m_new)
        p = jnp.exp2(s_t - (m_new - bias_row))
        l_ref[c] = alpha * l_ref[c] + jnp.sum(p, axis=0, keepdims=True)
        pv = lax.dot_general(v_ref[0], p.astype(BF16), _TN, preferred_element_type=F32)
        acc_ref[c] = alpha * acc_ref[c] + pv
        m_ref[c] = m_new

    @pl.when(jnp.abs(delta) <= 1)
    def _near():
        bias_tile = bias_ref[0, delta + 1]
        for c in range(2):
            step(c, bias_tile, 0.0)

    @pl.when(jnp.abs(delta) > 1)
    def _far():
        side = jnp.where(delta < 0, far_ref[h, 0], far_ref[h, 1])
        for c in range(2):
            step(c, None, side)

    @pl.when(ki == nk - 1)
    def _finish():
        o_ref[0] = _diff_out(acc_ref[0], l_ref[0], acc_ref[1], l_ref[1], lam_ref, g_ref, lam_init)


def _diff_attn_online(qb, kb, vb, far, bias_t, lam, g_col, lam_init, tile):
    bsz, _, s_len, _ = qb.shape
    nt = s_len // tile
    kernel = functools.partial(_diff_attn_online_kernel, lam_init=lam_init)

    def body(far_ref, q_ref, k_ref, v_ref, bias_ref, lam_ref, g_ref, o_ref, *scratch):
        kernel(far_ref, q_ref.at[0], k_ref.at[0], v_ref.at[0], bias_ref, lam_ref, g_ref, o_ref.at[0], *scratch)

    return pl.pallas_call(
        body,
        grid=(bsz, B_HEADS, nt, nt),
        in_specs=[
            pl.BlockSpec(memory_space=pltpu.SMEM),
            pl.BlockSpec((1, 1, tile, LANES), lambda b, h, qi, ki: (b, h, qi, 0)),
            pl.BlockSpec((1, 1, tile, LANES), lambda b, h, qi, ki: (b, h, ki, 0)),
            pl.BlockSpec((1, 1, tile, LANES), lambda b, h, qi, ki: (b, h, ki, 0)),
            pl.BlockSpec((1, 3, tile, tile), lambda b, h, qi, ki: (h, 0, 0, 0)),
            _const_spec((4, HEAD_DIM)),
            _const_spec((2 * HEAD_DIM, 1)),
        ],
        out_specs=pl.BlockSpec((1, 1, tile, LANES), lambda b, h, qi, ki: (b, h, qi, 0)),
        out_shape=jax.ShapeDtypeStruct((bsz, B_HEADS, s_len, LANES), BF16),
        scratch_shapes=[
            pltpu.VMEM((2, tile, LANES), BF16),
            pltpu.VMEM((2, 1, tile), F32),
            pltpu.VMEM((2, 1, tile), F32),
            pltpu.VMEM((2, 2 * HEAD_DIM, tile), F32),
        ],
        compiler_params=_cparams(("parallel", "parallel", "parallel", "arbitrary")),
        name="diff_attn_online",
    )(far, qb, kb, vb, bias_t, lam, g_col)


def _diff_attn(qb, kb, vb, db, lam, g_col, lam_init, tile):
    bounded = lambda: _diff_attn_bounded(qb, kb, vb, db["shifted"], lam, g_col, lam_init, tile)
    online = lambda: _diff_attn_online(qb, kb, vb, db["far"], db["near"], lam, g_col, lam_init, tile)
    return lax.cond(db["bounded_ok"], bounded, online)


FFN_CHUNKS = ((0, 1536), (1536, FFN_HIDDEN))


def _mix_ffn_kernel(x_ref, a_ref, b_ref, wa_ref, wb_ref, g_ref, wg_ref, wu_ref, wd_ref, o_ref):
    if len(b_ref.shape) == 4:
        b = jnp.concatenate([b_ref[0, hd] for hd in range(b_ref.shape[1])], axis=-1)
    else:
        b = b_ref[0]
    mix = (jnp.dot(a_ref[0], wa_ref[...], preferred_element_type=F32)
           + jnp.dot(b, wb_ref[...], preferred_element_type=F32))
    x1 = x_ref[0] + mix
    h = _rms(x1, g_ref[...]).astype(BF16)
    gates = [jnp.dot(h, wg_ref[:, lo:hi], preferred_element_type=F32) for lo, hi in FFN_CHUNKS]
    ups = [jnp.dot(h, wu_ref[:, lo:hi], preferred_element_type=F32) for lo, hi in FFN_CHUNKS]
    down = None
    for (lo, hi), gate, up in zip(FFN_CHUNKS, gates, ups):
        act = (gate * jax.nn.sigmoid(gate) * up).astype(BF16)
        part = jnp.dot(act, wd_ref[lo:hi, :], preferred_element_type=F32)
        down = part if down is None else down + part
    o_ref[0] = x1 + down


def _mix_ffn(x, a, b, wa, wb, g, wg, wu, wd, tm):
    bsz, s_len, _ = x.shape
    spt = s_len // tm
    row_spec = lambda w: pl.BlockSpec((1, tm, w), lambda i: (i // spt, i % spt, 0))
    if b.ndim == 4:
        b_spec = pl.BlockSpec((1, b.shape[1], tm, LANES), lambda i: (i // spt, 0, i % spt, 0))
    else:
        b_spec = row_spec(b.shape[-1])
    resident = pl.BlockSpec(memory_space=pltpu.VMEM)
    return pl.pallas_call(
        _mix_ffn_kernel,
        grid=(bsz * spt,),
        in_specs=[row_spec(D_MODEL), row_spec(a.shape[-1]), b_spec] + [resident] * 6,
        out_specs=row_spec(D_MODEL),
        out_shape=jax.ShapeDtypeStruct(x.shape, F32),
        compiler_params=_cparams(("parallel",)),
        name="mix_ffn",
    )(x, a, b, wa, wb, g, wg, wu, wd)


def _conv_in_kernel(x_ref, g_ref, w_ref, gb_ref, gx_ref, glu_ref):
    h = _rms(x_ref[0], g_ref[...]).astype(BF16)
    c = C_WIDTH
    gb, gc, xc, a, gate = [jnp.dot(h, w_ref[:, i * c:(i + 1) * c], preferred_element_type=F32) for i in range(5)]
    gb_ref[0] = gb.astype(BF16)
    gx_ref[0] = (gc * xc).astype(BF16)
    glu_ref[0] = (a * jax.nn.sigmoid(gate)).astype(BF16)


def _conv_in(x, g, w, tm):
    bsz, s_len, _ = x.shape
    spt = s_len // tm
    row_spec = lambda width: pl.BlockSpec((1, tm, width), lambda i: (i // spt, i % spt, 0))
    out = jax.ShapeDtypeStruct((bsz, s_len, C_WIDTH), BF16)
    return pl.pallas_call(
        _conv_in_kernel,
        grid=(bsz * spt,),
        in_specs=[row_spec(D_MODEL), _const_spec((1, D_MODEL)), _const_spec((D_MODEL, w.shape[1]))],
        out_specs=[row_spec(C_WIDTH)] * 3,
        out_shape=[out] * 3,
        compiler_params=_cparams(("parallel",)),
        name="conv_in",
    )(x, g, w)


CONV_CHUNK = 32
SUBLANES = 8


def _conv_mix_kernel(gb_ref, gx_ref, gxp_ref, gxn_ref, u_ref, up_ref, un_ref,
                     scw_ref, dww_ref, dwb_ref, lng_ref, lnb_ref, yc_ref, yu_ref, xs_ref, u_ref_f32, *, ts):
    t = pl.program_id(1)
    nt = pl.num_programs(1)
    has_prev = (t > 0).astype(F32)
    has_next = (t < nt - 1).astype(F32)
    n_shifted = ts + 2 * HALO - SUBLANES

    def taps(width):
        return [(j,) + divmod(HALO - width // 2 + j, SUBLANES)[::-1] for j in range(width)]

    def fill(cur, prev, nxt, width):
        xs_ref[0, 0:HALO, :] = prev[0].astype(F32) * has_prev
        xs_ref[0, HALO:HALO + ts, :] = cur[0].astype(F32)
        xs_ref[0, HALO + ts:2 * HALO + ts, :] = nxt[0].astype(F32) * has_next
        for r in sorted({r for _, r, _ in taps(width)} - {0}):
            xs_ref[r, 0:n_shifted, :] = xs_ref[0, r:r + n_shifted, :]

    def conv(w_ref, width, s0):
        acc = None
        groups = CONV_CHUNK // SUBLANES
        for r in sorted({r for _, r, _ in taps(width)}):
            mine = [(j, q) for j, r_j, q in taps(width) if r_j == r]
            q_lo, q_hi = min(q for _, q in mine), max(q for _, q in mine)
            span = xs_ref[r, pl.ds(s0 + SUBLANES * q_lo, CONV_CHUNK + SUBLANES * (q_hi - q_lo)), :]
            span = span.reshape(groups + q_hi - q_lo, SUBLANES, -1)
            for j, q in mine:
                term = span[q - q_lo:q - q_lo + groups] * w_ref[j]
                acc = term if acc is None else acc + term
        return acc.reshape(CONV_CHUNK, -1)

    def chunks(body):
        def step(c, carry):
            body(pl.multiple_of(c * CONV_CHUNK, CONV_CHUNK))
            return carry
        lax.fori_loop(0, ts // CONV_CHUNK, step, 0, unroll=True)

    fill(gx_ref, gxp_ref, gxn_ref, SHORT_CONV)

    def short(s0):
        out = pl.ds(s0, CONV_CHUNK)
        yc_ref[0, out, :] = (gb_ref[0, out, :].astype(F32) * conv(scw_ref, SHORT_CONV, s0)).astype(BF16)
    chunks(short)

    fill(u_ref, up_ref, un_ref, CONF_CONV)

    def conf(s0):
        u_ref_f32[pl.ds(s0, CONV_CHUNK), :] = conv(dww_ref, CONF_CONV, s0) + dwb_ref[...]
    chunks(conf)

    u = u_ref_f32[...]
    mu = jnp.mean(u, axis=-1, keepdims=True)
    uc = u - mu
    var = jnp.mean(uc * uc, axis=-1, keepdims=True)
    y = uc * lax.rsqrt(var + EPS) * lng_ref[...] + lnb_ref[...]
    yu_ref[0] = (y * jax.nn.sigmoid(y)).astype(BF16)


def _conv_mix(gb, gx, glu, scw, dww, dwb, lng, lnb, ts):
    bsz, s_len, _ = gb.shape
    nt = s_len // ts
    r = ts // HALO
    nh = s_len // HALO
    cur = pl.BlockSpec((1, ts, 512), lambda b, t: (b, t, 0))
    prev = pl.BlockSpec((1, HALO, 512), lambda b, t: (b, jnp.maximum(t * r - 1, 0), 0))
    nxt = pl.BlockSpec((1, HALO, 512), lambda b, t: (b, jnp.minimum((t + 1) * r, nh - 1), 0))
    out_spec = pl.BlockSpec((1, ts, 512), lambda b, t: (b, t, 0))
    return pl.pallas_call(
        functools.partial(_conv_mix_kernel, ts=ts),
        grid=(bsz, nt),
        in_specs=[cur, cur, prev, nxt, cur, prev, nxt,
                  _const_spec((SHORT_CONV, SUBLANES, C_WIDTH)), _const_spec((CONF_CONV, SUBLANES, D_WIDTH)),
                  _const_spec((1, D_WIDTH)), _const_spec((1, D_WIDTH)), _const_spec((1, D_WIDTH))],
        out_specs=[out_spec, out_spec],
        out_shape=[jax.ShapeDtypeStruct((bsz, s_len, C_WIDTH), BF16),
                   jax.ShapeDtypeStruct((bsz, s_len, D_WIDTH), BF16)],
        scratch_shapes=[pltpu.VMEM((SUBLANES, ts + 2 * HALO, 512), F32), pltpu.VMEM((ts, D_WIDTH), F32)],
        compiler_params=_cparams(("parallel", "parallel")),
        name="conv_mix",
    )(gb, gx, gx, gx, glu, glu, glu, scw, dww, dwb, lng, lnb)


def _rel_bucket(rel):
    half = NUM_BUCKETS // 2
    max_exact = half // 2
    n = jnp.abs(rel)
    large = max_exact + (jnp.log(jnp.maximum(n, 1).astype(F32) / max_exact)
                         / math.log(MAX_DISTANCE / max_exact) * (half - max_exact)).astype(jnp.int32)
    large = jnp.minimum(large, half - 1)
    return jnp.where(rel > 0, half, 0) + jnp.where(n < max_exact, n, large)


def _toeplitz(u, rows, cols):
    length = u.shape[-1]
    flat = jnp.tile(u, (1,) * (u.ndim - 1) + (rows,))[..., :rows * (length - 1)]
    return flat.reshape(u.shape[:-1] + (rows, length - 1))[..., :cols]


LOG2E = math.log2(math.e)


def _win_bias(rel_bias, a_sink, a_qn, a_kn):
    order = jnp.array(WIN_HEAD_ORDER)
    table = rel_bias[:, :A_Q_HEADS].astype(F32) * LOG2E
    length = 4 * BLOCK
    n = jnp.arange(length)
    rel = jnp.where(n < BLOCK, -n, length - n) - BLOCK
    vec = jnp.where((jnp.abs(rel) <= WINDOW)[None], table[_rel_bucket(rel)].T, NEG)
    base = _toeplitz(vec, 3 * BLOCK, BLOCK)[order]
    mid = base.transpose(1, 0, 2).reshape(3 * BLOCK, A_Q_HEADS * BLOCK)
    key_block = (jnp.arange(3 * BLOCK) // BLOCK)[:, None]
    bias = jnp.stack([jnp.where(key_block == 0, NEG, mid), mid, jnp.where(key_block == 2, NEG, mid)], axis=0)
    sink = a_sink.astype(F32) * LOG2E
    s_max = 1.02 * LOG2E * math.sqrt(HEAD_DIM) * jnp.max(jnp.abs(a_qn * a_kn))
    bound = s_max + jnp.maximum(jnp.max(table, axis=0), sink)
    lowest = jnp.minimum(-s_max + jnp.min(table, axis=0), sink) - bound
    columns = lambda per_head: jnp.repeat(per_head[order], BLOCK)[None, :]
    return {"bias": bias, "sink": columns(sink),
            "bias_shifted": bias - columns(bound)[None], "sink_shifted": columns(sink - bound),
            "bounded_ok": jnp.all(lowest > MIN_EXP2_ARG)}
MIN_EXP2_ARG = -120.0


def _diff_bias(rel_bias, b_qn, b_kn, tile):
    table = rel_bias[:, A_Q_HEADS:].astype(F32) * LOG2E
    length = 2 * tile
    n = jnp.arange(length)
    k_minus_q = jnp.where(n < tile, -n, length - n)
    rel = jnp.clip(k_minus_q[None, :] + jnp.array([-tile, 0, tile])[:, None], 1 - length, length - 1)
    near = _toeplitz(table[_rel_bucket(rel)].transpose(2, 0, 1), tile, tile)
    far = table[_rel_bucket(jnp.array([-length, length]))].T
    s_max = 1.02 * LOG2E * math.sqrt(HEAD_DIM) * jnp.max(jnp.abs(b_qn * b_kn))
    b_max, b_min = jnp.max(table, axis=0), jnp.min(table, axis=0)
    bound = s_max + b_max
    const = lambda col: jnp.broadcast_to((far[:, col] - bound)[:, None, None, None], (B_HEADS, 1, tile, tile))
    shifted = jnp.concatenate([const(0), near - bound[:, None, None, None], const(1)], axis=1)
    bounded_ok = jnp.all(-2.0 * s_max - (b_max - b_min) > MIN_EXP2_ARG)
    return {"near": near, "far": far, "shifted": shifted, "bounded_ok": bounded_ok}


def _by_win_head(m, axis):
    take = lambda h: lax.slice_in_dim(m, h * HEAD_DIM, (h + 1) * HEAD_DIM, axis=axis)
    return jnp.concatenate([take(h) for h in WIN_HEAD_ORDER], axis=axis)


def _attn_in_params(w_in, a_qn, a_kn, b_qn, b_kn):
    d = HEAD_DIM
    k0 = A_Q_HEADS * d
    w = jnp.concatenate([_by_win_head(w_in[:, :k0], 1), w_in[:, k0:]], axis=1).astype(BF16)
    scale = HEAD_DIM ** -0.5 * LOG2E
    ones = lambda n: jnp.ones((n,), F32)
    gain = jnp.concatenate([
        jnp.tile(a_qn, A_Q_HEADS) * scale, jnp.tile(a_kn, A_KV_HEADS), ones(A_KV_HEADS * d),
        jnp.tile(b_qn, 2 * B_HEADS) * scale, jnp.tile(b_kn, 2 * B_HEADS), ones(2 * B_HEADS * d),
    ]).astype(F32)[None, :]
    seg_id = jnp.arange(MXU_COLS) // d
    seg = jnp.where(seg_id[:, None] == seg_id[None, :], 1.0 / d, 0.0).astype(BF16)
    return w, gain, seg


def _trunk(x, p, *, tm, tile, ts, qb):
    qa, ka, va, qb_, kb, vb = _attn_in(x, p["mix_g"][0], p["attn_w"], p["attn_gain"], p["seg"], tm)
    ya = _win_attn(qa, ka, va, p["win_bias"], qb)
    yb = _diff_attn(qb_, kb, vb, p["diff_bias"][tile], p["lam"], p["subln"], p["lam_init"], tile)
    x = _mix_ffn(x, ya, yb, p["attn_wo_a"], p["attn_wo_b"],
                 p["ffn_g"][0], p["wg"][0], p["wu"][0], p["wd"][0], tm)

    gb, gx, glu = _conv_in(x, p["mix_g"][1], p["conv_w"], tm)
    yc, yu = _conv_mix(gb, gx, glu, p["scw"], p["dww"], p["dwb"], p["lng"], p["lnb"], ts)
    return _mix_ffn(x, yc, yu, p["conv_wo_a"], p["conv_wo_b"],
                    p["ffn_g"][1], p["wg"][1], p["wu"][1], p["wd"][1], tm)


def _prepare(rel_bias, mix_norm, ffn_norm, w_gate, w_up, w_down, attn_w_in, attn_w_out, a_q_norm, a_k_norm,
             a_sink, b_q_norm, b_k_norm, b_lambda, b_subln, conv_w_in, conv_w_out, short_conv_w, conf_dw_w,
             conf_dw_b, conf_ln_g, conf_ln_b, tiles):
    attn_w, attn_gain, seg = _attn_in_params(attn_w_in[0], a_q_norm[0], a_k_norm[0], b_q_norm[0], b_k_norm[0])
    half = A_Q_HEADS * HEAD_DIM
    return {
        "mix_g": [mix_norm[l][None, :].astype(F32) for l in range(2)],
        "ffn_g": [ffn_norm[l][None, :].astype(F32) for l in range(2)],
        "wg": [w_gate[l].astype(BF16) for l in range(2)],
        "wu": [w_up[l].astype(BF16) for l in range(2)],
        "wd": [w_down[l].astype(BF16) for l in range(2)],
        "attn_w": attn_w, "attn_gain": attn_gain, "seg": seg,
        "attn_wo_a": _by_win_head(attn_w_out[0][:half], 0).astype(BF16),
        "attn_wo_b": attn_w_out[0][half:].astype(BF16),
        "win_bias": _win_bias(rel_bias, a_sink[0], a_q_norm[0], a_k_norm[0]),
        "diff_bias": {t: _diff_bias(rel_bias, b_q_norm[0], b_k_norm[0], t) for t in tiles},
        "lam": b_lambda[0].astype(F32),
        "subln": b_subln[0].astype(F32)[:, None],
        "lam_init": 0.8 - 0.6 * math.exp(-0.3 * 0),
        "conv_w": conv_w_in[0].astype(BF16),
        "conv_wo_a": conv_w_out[0][:C_WIDTH].astype(BF16), "conv_wo_b": conv_w_out[0][C_WIDTH:].astype(BF16),
        "scw": jnp.broadcast_to(short_conv_w[0].astype(F32)[:, None, :], (SHORT_CONV, SUBLANES, C_WIDTH)),
        "dww": jnp.broadcast_to(conf_dw_w[0].astype(F32)[:, None, :], (CONF_CONV, SUBLANES, D_WIDTH)),
        "dwb": conf_dw_b[0][None, :].astype(F32),
        "lng": conf_ln_g[0][None, :].astype(F32), "lnb": conf_ln_b[0][None, :].astype(F32),
    }


def _tiling(s_len):
    return dict(tm=min(1024, s_len), tile=min(512, s_len), ts=min(512, s_len), qb=8)


def kernel(x_prompt, x_sample, rel_bias, mix_norm, ffn_norm, w_gate, w_up, w_down, attn_w_in, attn_w_out,
           a_q_norm, a_k_norm, a_sink, b_q_norm, b_k_norm, b_lambda, b_subln, conv_w_in, conv_w_out,
           short_conv_w, conf_dw_w, conf_dw_b, conf_ln_g, conf_ln_b):
    tp, tsm = _tiling(x_prompt.shape[1]), _tiling(x_sample.shape[1])
    p = _prepare(rel_bias, mix_norm, ffn_norm, w_gate, w_up, w_down, attn_w_in, attn_w_out, a_q_norm,
                 a_k_norm, a_sink, b_q_norm, b_k_norm, b_lambda, b_subln, conv_w_in, conv_w_out,
                 short_conv_w, conf_dw_w, conf_dw_b, conf_ln_g, conf_ln_b, {tp["tile"], tsm["tile"]})
    return (_trunk(x_prompt, p, **tp), _trunk(x_sample, p, **tsm))
```

```python
import functools
import math

import jax
import jax.numpy as jnp
from jax import lax
from jax.experimental import pallas as pl
from jax.experimental.pallas import tpu as pltpu

D_MODEL = 1024
HEAD_DIM = 64
A_Q_HEADS = 8
A_KV_HEADS = 2
WINDOW = 128
BLOCK = 128
B_HEADS = 4
NUM_BUCKETS = 32
MAX_DISTANCE = 128
C_WIDTH = 512
D_WIDTH = 512
SHORT_CONV = 3
CONF_CONV = 31
FFN_HIDDEN = 2816
EPS = 1e-6
NEG = -1e30

LANES = 128
MXU_COLS = 256
VMEM_LIMIT_BYTES = 56 * 1024 * 1024

BF16 = jnp.bfloat16
F32 = jnp.float32

PROJ0_W = 2304
NORM_CHUNKS0 = ("all", "all", "low", "all", "all", "all", "all", None, None)
WIN_HEAD_ORDER = (0, 4, 1, 5, 2, 6, 3, 7)

HALO = 16


def _cparams(sem):
    return pltpu.CompilerParams(dimension_semantics=sem, vmem_limit_bytes=VMEM_LIMIT_BYTES)


def _const_spec(shape):
    nd = len(shape)
    return pl.BlockSpec(shape, lambda *_: (0,) * nd)


def _rms(x, g):
    ms = jnp.mean(x * x, axis=-1, keepdims=True)
    return x * lax.rsqrt(ms + EPS) * g


def _attn_in_kernel(x_ref, g_ref, w_ref, gain_ref, seg_ref, qa_ref, ka_ref, va_ref, qb_ref, kb_ref, vb_ref):
    half = MXU_COLS // 2

    def store(c, y):
        if c < 2:
            qa_ref[:, c * MXU_COLS:(c + 1) * MXU_COLS] = y
        elif c == 2:
            ka_ref[...] = y[:, :half]
            va_ref[...] = y[:, half:]
        else:
            ref, first = ((qb_ref, 3), (kb_ref, 5), (vb_ref, 7))[(c - 3) // 2]
            ref[0, 2 * (c - first)] = y[:, :half]
            ref[0, 2 * (c - first) + 1] = y[:, half:]

    h = _rms(x_ref[...], g_ref[...]).astype(BF16)
    seg = seg_ref[...]
    low = lax.broadcasted_iota(jnp.int32, (1, MXU_COLS), 1) < MXU_COLS // 2
    chunk = lambda c: slice(c * MXU_COLS, (c + 1) * MXU_COLS)
    accs = [jnp.dot(h, w_ref[:, chunk(c)], preferred_element_type=F32) for c in range(len(NORM_CHUNKS0))]
    for c, normed in enumerate(NORM_CHUNKS0):
        acc = accs[c]
        if normed is not None:
            sq = acc * acc
            sq_hi = sq.astype(BF16)
            sq_lo = (sq - sq_hi.astype(F32)).astype(BF16)
            ms = (jnp.dot(sq_hi, seg, preferred_element_type=F32)
                  + jnp.dot(sq_lo, seg, preferred_element_type=F32))
            scale = lax.rsqrt(ms + EPS) * gain_ref[:, chunk(c)]
            acc = acc * (scale if normed == "all" else jnp.where(low, scale, 1.0))
        store(c, acc.astype(BF16))


def _attn_in(x, g, w, gain, seg, tm):
    bsz, s_len, _ = x.shape
    spt = s_len // tm
    flat = lambda width: pl.BlockSpec((1, tm, width), lambda i: (i // spt, i % spt, 0))
    heads = pl.BlockSpec((1, B_HEADS, tm, LANES), lambda i: (i // spt, 0, i % spt, 0))
    sds = lambda *shape: jax.ShapeDtypeStruct(shape, BF16)
    head_major = sds(bsz, B_HEADS, s_len, LANES)

    def body(x_ref, g_ref, w_ref, gain_ref, seg_ref, qa_ref, ka_ref, va_ref, qb_ref, kb_ref, vb_ref):
        _attn_in_kernel(x_ref.at[0], g_ref, w_ref, gain_ref, seg_ref,
                        qa_ref.at[0], ka_ref.at[0], va_ref.at[0], qb_ref, kb_ref, vb_ref)

    return pl.pallas_call(
        body,
        grid=(bsz * spt,),
        in_specs=[
            flat(D_MODEL),
            _const_spec((1, D_MODEL)),
            _const_spec((D_MODEL, PROJ0_W)),
            _const_spec((1, PROJ0_W)),
            _const_spec((MXU_COLS, MXU_COLS)),
        ],
        out_specs=[flat(512), flat(LANES), flat(LANES), heads, heads, heads],
        out_shape=[sds(bsz, s_len, 512), sds(bsz, s_len, LANES), sds(bsz, s_len, LANES),
                   head_major, head_major, head_major],
        compiler_params=_cparams(("parallel",)),
        name="attn_in",
    )(x, g, w, gain, seg)


WIN_COLS = A_Q_HEADS * BLOCK


def _win_attn_kernel(q_ref, kp_ref, kc_ref, kn_ref, vp_ref, vc_ref, vn_ref, bias_ref, sink_ref, o_ref, *,
                     qb, bounded):
    step = pl.program_id(1)
    nb = pl.num_programs(1) * qb
    low = lax.broadcasted_iota(jnp.int32, (1, LANES), 1) < HEAD_DIM
    top = lax.broadcasted_iota(jnp.int32, (LANES, 1), 0) < HEAD_DIM
    sink = sink_ref[...]

    def piece(prev_ref, cur_ref, next_ref, i):
        if i == 0:
            return prev_ref[0]
        if i == qb + 1:
            return next_ref[0]
        return cur_ref[0, (i - 1) * BLOCK:i * BLOCK, :]

    def scores(b):
        kcat = jnp.concatenate([piece(kp_ref, kc_ref, kn_ref, b + i) for i in range(3)], axis=0)
        q = q_ref[0, b * BLOCK:(b + 1) * BLOCK, :]
        zero = jnp.zeros((BLOCK, LANES), BF16)
        halves = []
        for j in range(A_Q_HEADS // 2):
            slab = q[:, j * LANES:(j + 1) * LANES]
            halves += [jnp.where(low, slab, zero), jnp.where(low, zero, slab)]
        qcat = jnp.concatenate(halves, axis=0)
        return lax.dot_general(kcat, qcat, _NT, preferred_element_type=F32)

    def weights(b, s_t):
        n = step * qb + b
        edge = jnp.where(n == 0, 0, jnp.where(n == nb - 1, 2, 1))
        s_t = s_t + bias_ref[edge]
        if bounded:
            p = jnp.exp2(s_t)
            den = jnp.sum(p, axis=0, keepdims=True) + jnp.exp2(sink)
        else:
            m = jnp.maximum(jnp.max(s_t, axis=0, keepdims=True), sink)
            p = jnp.exp2(s_t - m)
            den = jnp.sum(p, axis=0, keepdims=True) + jnp.exp2(sink - m)
        return p.astype(BF16), den

    def values(b, p, den):
        vcat = jnp.concatenate([piece(vp_ref, vc_ref, vn_ref, b + i) for i in range(3)], axis=0)
        return lax.dot_general(vcat, p, _TN, preferred_element_type=F32) / den

    s_all = [scores(b) for b in range(qb)]
    w_all = [weights(b, s_all[b]) for b in range(qb)]
    o_all = [values(b, *w_all[b]) for b in range(qb)]
    for b in range(qb):
        for j in range(A_Q_HEADS // 2):
            c0 = 2 * j * BLOCK
            slab_t = jnp.where(top, o_all[b][:, c0:c0 + BLOCK], o_all[b][:, c0 + BLOCK:c0 + 2 * BLOCK])
            o_ref[0, b * BLOCK:(b + 1) * BLOCK, j * LANES:(j + 1) * LANES] = slab_t.T.astype(BF16)


def _win_attn(qa, ka, va, wb, qb):
    fast = lambda: _win_attn_call(qa, ka, va, wb["sink_shifted"], wb["bias_shifted"], qb, True)
    exact_max = lambda: _win_attn_call(qa, ka, va, wb["sink"], wb["bias"], qb, False)
    return lax.cond(wb["bounded_ok"], fast, exact_max)


def _win_attn_call(qa, ka, va, sink_row, bias, qb, bounded):
    bsz, s_len, _ = qa.shape
    nb = s_len // BLOCK
    assert nb % qb == 0 and nb >= 2
    kv_specs = [
        pl.BlockSpec((1, BLOCK, LANES), lambda b, t: (b, jnp.maximum(t * qb - 1, 0), 0)),
        pl.BlockSpec((1, qb * BLOCK, LANES), lambda b, t: (b, t, 0)),
        pl.BlockSpec((1, BLOCK, LANES), lambda b, t: (b, jnp.minimum((t + 1) * qb, nb - 1), 0)),
    ]
    return pl.pallas_call(
        functools.partial(_win_attn_kernel, qb=qb, bounded=bounded),
        grid=(bsz, nb // qb),
        in_specs=[pl.BlockSpec((1, qb * BLOCK, 512), lambda b, t: (b, t, 0))] + kv_specs + kv_specs
                 + [pl.BlockSpec(memory_space=pltpu.VMEM), pl.BlockSpec(memory_space=pltpu.VMEM)],
        out_specs=pl.BlockSpec((1, qb * BLOCK, 512), lambda b, t: (b, t, 0)),
        out_shape=jax.ShapeDtypeStruct((bsz, s_len, 512), BF16),
        compiler_params=_cparams(("parallel", "parallel")),
        name="win_attn",
    )(qa, ka, ka, ka, va, va, va, bias, sink_row)


_NT = (((1,), (1,)), ((), ()))
_TN = (((0,), (0,)), ((), ()))


def _split_maps(q, qz_ref):
    lane = lax.broadcasted_iota(jnp.int32, (1, LANES), 1)
    zero = jnp.zeros_like(q)
    qz_ref[0] = jnp.where(lane < HEAD_DIM, q, zero)
    qz_ref[1] = jnp.where(lane < HEAD_DIM, zero, q)


def _diff_out(acc0, l0, acc1, l1, lam_ref, g_ref, lam_init):
    lam = lam_ref[...]
    lam_full = (jnp.exp(jnp.sum(lam[0:1] * lam[1:2], axis=-1, keepdims=True))
                - jnp.exp(jnp.sum(lam[2:3] * lam[3:4], axis=-1, keepdims=True)) + lam_init)
    o = acc0 / l0 - lam_full * (acc1 / l1)
    ms = jnp.mean(o * o, axis=0, keepdims=True)
    y = o * lax.rsqrt(ms + EPS) * g_ref[...] * (1.0 - lam_init)
    return y.T.astype(BF16)


def _diff_attn_bounded_kernel(q_ref, k_ref, v_ref, bias_ref, lam_ref, g_ref, o_ref,
                              qz_ref, s_ref, l_ref, acc_ref, *, lam_init, tile, qtiles):
    qstep = pl.program_id(2)
    nk = k_ref.shape[1] // tile
    first_map = lax.broadcasted_iota(jnp.int32, (LANES, 1), 0) < HEAD_DIM
    for qt in range(qtiles):
        q_t = q_ref[0, qt * tile:(qt + 1) * tile, :].astype(F32).T
        qz_ref[qt, 0] = jnp.where(first_map, q_t, 0.0).astype(BF16)
        qz_ref[qt, 1] = jnp.where(first_map, 0.0, q_t).astype(BF16)
    l_ref[...] = jnp.zeros(l_ref.shape, F32)
    acc_ref[...] = jnp.zeros(acc_ref.shape, F32)

    def produce(qt, ki, buf):
        k = k_ref[0, ki * tile:(ki + 1) * tile, :]
        for c in range(2):
            s_ref[buf, c] = jnp.dot(k, qz_ref[qt, c], preferred_element_type=F32)

    def consume(qt, ki, buf):
        v = v_ref[0, ki * tile:(ki + 1) * tile, :]
        bias = bias_ref[0, jnp.clip(ki - (qstep * qtiles + qt) + 2, 0, 4)]
        for c in range(2):
            p = jnp.exp2(s_ref[buf, c] + bias)
            l_ref[qt, c] += jnp.sum(p.reshape(tile // 8, 8, tile), axis=0)
            acc_ref[qt, c] += lax.dot_general(v, p.astype(BF16), _TN, preferred_element_type=F32)

    def finish(qt):
        l0 = jnp.sum(l_ref[qt, 0], axis=0, keepdims=True)
        l1 = jnp.sum(l_ref[qt, 1], axis=0, keepdims=True)
        o_ref[0, qt * tile:(qt + 1) * tile, :] = _diff_out(acc_ref[qt, 0], l0, acc_ref[qt, 1], l1,
                                                           lam_ref, g_ref, lam_init)

    steps =[(qt, ki) for qt in range(qtiles) for ki in range(nk)]
    produce(*steps[0], 0)
    for i, (qt, ki) in enumerate(steps):
        if i + 1 < len(steps):
            produce(*steps[i + 1], (i + 1) % 2)
        consume(qt, ki, i % 2)
        if ki == nk - 1:
            finish(qt)


DIFF_STEPS = 16


def _diff_attn_bounded(qb, kb, vb, bias_t, lam, g_col, lam_init, tile):
    bsz, _, s_len, _ = qb.shape
    nk = s_len // tile
    assert nk <= DIFF_STEPS and DIFF_STEPS % nk == 0 and nk % 2 == 0
    qtiles = min(DIFF_STEPS // nk, nk)
    assert nk % qtiles == 0
    kernel = functools.partial(_diff_attn_bounded_kernel, lam_init=lam_init, tile=tile, qtiles=qtiles)

    def body(q_ref, k_ref, v_ref, bias_ref, lam_ref, g_ref, o_ref, *scratch):
        kernel(q_ref.at[0], k_ref.at[0], v_ref.at[0], bias_ref, lam_ref, g_ref, o_ref.at[0], *scratch)

    q_rows = qtiles * tile
    return pl.pallas_call(
        body,
        grid=(bsz, B_HEADS, s_len // q_rows),
        in_specs=[
            pl.BlockSpec((1, 1, q_rows, LANES), lambda b, h, qi: (b, h, qi, 0)),
            pl.BlockSpec((1, 1, s_len, LANES), lambda b, h, qi: (b, h, 0, 0)),
            pl.BlockSpec((1, 1, s_len, LANES), lambda b, h, qi: (b, h, 0, 0)),
            pl.BlockSpec((1, 5, tile, tile), lambda b, h, qi: (h, 0, 0, 0)),
            _const_spec((4, HEAD_DIM)),
            _const_spec((2 * HEAD_DIM, 1)),
        ],
        out_specs=pl.BlockSpec((1, 1, q_rows, LANES), lambda b, h, qi: (b, h, qi, 0)),
        out_shape=jax.ShapeDtypeStruct((bsz, B_HEADS, s_len, LANES), BF16),
        scratch_shapes=[
            pltpu.VMEM((qtiles, 2, LANES, tile), BF16),
            pltpu.VMEM((2, 2, tile, tile), F32),
            pltpu.VMEM((qtiles, 2, 8, tile), F32),
            pltpu.VMEM((qtiles, 2, 2 * HEAD_DIM, tile), F32),
        ],
        compiler_params=_cparams(("parallel", "parallel", "parallel")),
        name="diff_attn",
    )(qb, kb, vb, bias_t, lam, g_col)


def _diff_attn_online_kernel(far_ref, q_ref, k_ref, v_ref, bias_ref, lam_ref, g_ref, o_ref,
                             qz_ref, m_ref, l_ref, acc_ref, *, lam_init):
    h = pl.program_id(1)
    qi = pl.program_id(2)
    ki = pl.program_id(3)
    nk = pl.num_programs(3)

    @pl.when(ki == 0)
    def _init():
        _split_maps(q_ref[0], qz_ref)
        m_ref[...] = jnp.full(m_ref.shape, NEG, F32)
        l_ref[...] = jnp.zeros(l_ref.shape, F32)
        acc_ref[...] = jnp.zeros(acc_ref.shape, F32)

    delta = ki - qi

    def step(c, bias_tile, bias_row):
        s_t = lax.dot_general(k_ref[0], qz_ref[c], _NT, preferred_element_type=F32)
        if bias_tile is not None:
            s_t = s_t + bias_tile
        m_old = m_ref[c]
        m_new = jnp.maximum(m_old, jnp.max(s_t, axis=0, keepdims=True) + bias_row)
        alpha = jnp.exp2(m_old - m_new)
        p = jnp.exp2(s_t - (m_new - bias_row))
        l_ref[c] = alpha * l_ref[c] + jnp.sum(p, axis=0, keepdims=True)
        pv = lax.dot_general(v_ref[0], p.astype(BF16), _TN, preferred_element_type=F32)
        acc_ref[c] = alpha * acc_ref[c] + pv
        m_ref[c] = m_new

    @pl.when(jnp.abs(delta) <= 1)
    def _near():
        bias_tile = bias_ref[0, delta + 1]
        for c in range(2):
            step(c, bias_tile, 0.0)

    @pl.when(jnp.abs(delta) > 1)
    def _far():
        side = jnp.where(delta < 0, far_ref[h, 0], far_ref[h, 1])
        for c in range(2):
            step(c, None, side)

    @pl.when(ki == nk - 1)
    def _finish():
        o_ref[0] = _diff_out(acc_ref[0], l_ref[0], acc_ref[1], l_ref[1], lam_ref, g_ref, lam_init)


def _diff_attn_online(qb, kb, vb, far, bias_t, lam, g_col, lam_init, tile):
    bsz, _, s_len, _ = qb.shape
    nt = s_len // tile
    kernel = functools.partial(_diff_attn_online_kernel, lam_init=lam_init)

    def body(far_ref, q_ref, k_ref, v_ref, bias_ref, lam_ref, g_ref, o_ref, *scratch):
        kernel(far_ref, q_ref.at[0], k_ref.at[0], v_ref.at[0], bias_ref, lam_ref, g_ref, o_ref.at[0], *scratch)

    return pl.pallas_call(
        body,
        grid=(bsz, B_HEADS, nt, nt),
        in_specs=[
            pl.BlockSpec(memory_space=pltpu.SMEM),
            pl.BlockSpec((1, 1, tile, LANES), lambda b, h, qi, ki: (b, h, qi, 0)),
            pl.BlockSpec((1, 1, tile, LANES), lambda b, h, qi, ki: (b, h, ki, 0)),
            pl.BlockSpec((1, 1, tile, LANES), lambda b, h, qi, ki: (b, h, ki, 0)),
            pl.BlockSpec((1, 3, tile, tile), lambda b, h, qi, ki: (h, 0, 0, 0)),
            _const_spec((4, HEAD_DIM)),
            _const_spec((2 * HEAD_DIM, 1)),
        ],
        out_specs=pl.BlockSpec((1, 1, tile, LANES), lambda b, h, qi, ki: (b, h, qi, 0)),
        out_shape=jax.ShapeDtypeStruct((bsz, B_HEADS, s_len, LANES), BF16),
        scratch_shapes=[
            pltpu.VMEM((2, tile, LANES), BF16),
            pltpu.VMEM((2, 1, tile), F32),
            pltpu.VMEM((2, 1, tile), F32),
            pltpu.VMEM((2, 2 * HEAD_DIM, tile), F32),
        ],
        compiler_params=_cparams(("parallel", "parallel", "parallel", "arbitrary")),
        name="diff_attn_online",
    )(far, qb, kb, vb, bias_t, lam, g_col)


def _diff_attn(qb, kb, vb, db, lam, g_col, lam_init, tile):
    bounded = lambda: _diff_attn_bounded(qb, kb, vb, db["shifted"], lam, g_col, lam_init, tile)
    online = lambda: _diff_attn_online(qb, kb, vb, db["far"], db["near"], lam, g_col, lam_init, tile)
    return lax.cond(db["bounded_ok"], bounded, online)


FFN_CHUNKS = ((0, 1536), (1536, FFN_HIDDEN))


def _mix_ffn_kernel(x_ref, a_ref, b_ref, wa_ref, wb_ref, g_ref, wg_ref, wu_ref, wd_ref, o_ref):
    if len(b_ref.shape) == 4:
        b = jnp.concatenate([b_ref[0, hd] for hd in range(b_ref.shape[1])], axis=-1)
    else:
        b = b_ref[0]
    mix = (jnp.dot(a_ref[0], wa_ref[...], preferred_element_type=F32)
           + jnp.dot(b, wb_ref[...], preferred_element_type=F32))
    x1 = x_ref[0] + mix
    h = _rms(x1, g_ref[...]).astype(BF16)
    gates = [jnp.dot(h, wg_ref[:, lo:hi], preferred_element_type=F32) for lo, hi in FFN_CHUNKS]
    ups = [jnp.dot(h, wu_ref[:, lo:hi], preferred_element_type=F32) for lo, hi in FFN_CHUNKS]
    down = None
    for (lo, hi), gate, up in zip(FFN_CHUNKS, gates, ups):
        act = (gate * jax.nn.sigmoid(gate) * up).astype(BF16)
        part = jnp.dot(act, wd_ref[lo:hi, :], preferred_element_type=F32)
        down = part if down is None else down + part
    o_ref[0] = x1 + down


def _mix_ffn(x, a, b, wa, wb, g, wg, wu, wd, tm):
    bsz, s_len, _ = x.shape
    spt = s_len // tm
    row_spec = lambda w: pl.BlockSpec((1, tm, w), lambda i: (i // spt, i % spt, 0))
    if b.ndim == 4:
        b_spec = pl.BlockSpec((1, b.shape[1], tm, LANES), lambda i: (i // spt, 0, i % spt, 0))
    else:
        b_spec = row_spec(b.shape[-1])
    resident = pl.BlockSpec(memory_space=pltpu.VMEM)
    return pl.pallas_call(
        _mix_ffn_kernel,
        grid=(bsz * spt,),
        in_specs=[row_spec(D_MODEL), row_spec(a.shape[-1]), b_spec] + [resident] * 6,
        out_specs=row_spec(D_MODEL),
        out_shape=jax.ShapeDtypeStruct(x.shape, F32),
        compiler_params=_cparams(("parallel",)),
        name="mix_ffn",
    )(x, a, b, wa, wb, g, wg, wu, wd)


def _conv_in_kernel(x_ref, g_ref, w_ref, gb_ref, gx_ref, glu_ref):
    h = _rms(x_ref[0], g_ref[...]).astype(BF16)
    c = C_WIDTH
    gb, gc, xc, a, gate = [jnp.dot(h, w_ref[:, i * c:(i + 1) * c], preferred_element_type=F32) for i in range(5)]
    gb_ref[0] = gb.astype(BF16)
    gx_ref[0] = (gc * xc).astype(BF16)
    glu_ref[0] = (a * jax.nn.sigmoid(gate)).astype(BF16)


def _conv_in(x, g, w, tm):
    bsz, s_len, _ = x.shape
    spt = s_len // tm
    row_spec = lambda width: pl.BlockSpec((1, tm, width), lambda i: (i // spt, i % spt, 0))
    out = jax.ShapeDtypeStruct((bsz, s_len, C_WIDTH), BF16)
    return pl.pallas_call(
        _conv_in_kernel,
        grid=(bsz * spt,),
        in_specs=[row_spec(D_MODEL), _const_spec((1, D_MODEL)), _const_spec((D_MODEL, w.shape[1]))],
        out_specs=[row_spec(C_WIDTH)] * 3,
        out_shape=[out] * 3,
        compiler_params=_cparams(("parallel",)),
        name="conv_in",
    )(x, g, w)


CONV_CHUNK = 32
SUBLANES = 8


SHIFT_ROWS = 256
SHIFT_K = 384


def _conv_mix_kernel(gb_ref, gx_ref, gxp_ref, gxn_ref, u_ref, up_ref, un_ref, shift_ref,
                     scw_ref, dww_ref, dwb_ref, lng_ref, lnb_ref, yc_ref, yu_ref,
                     xb_ref, xs_ref, u_ref_f32, *, ts):
    t = pl.program_id(1)
    nt = pl.num_programs(1)
    padded = ts + 2 * HALO

    def taps(width):
        return [(j,) + divmod(HALO - width // 2 + j, SUBLANES)[::-1] for j in range(width)]

    def fill(cur, prev, nxt, width, xb, xs):
        zero = jnp.zeros((HALO, 512), BF16)
        xb[0:HALO, :] = jnp.where(t > 0, prev[0], zero)
        xb[HALO:HALO + ts, :] = cur[0]
        xb[HALO + ts:padded, :] = jnp.where(t < nt - 1, nxt[0], zero)
        xb[padded:, :] = jnp.zeros((xb.shape[0] - padded, 512), BF16)
        xs[0] = xb[0:padded, :].astype(F32)
        copies = sorted({r for _, r, _ in taps(width)} - {0})
        for o0 in range(0, ts, SHIFT_ROWS):
            for r in copies:
                xs[r, o0:o0 + SHIFT_ROWS, :] = jnp.dot(shift_ref[r - 1], xb[o0:o0 + SHIFT_K, :],
                                                       preferred_element_type=F32)
        for r in copies:
            xs[r, ts:padded, :] = jnp.dot(shift_ref[r - 1, 0:2 * HALO, 0:LANES], xb[ts:ts + LANES, :],
                                          preferred_element_type=F32)

    def conv(w_ref, width, s0, xs):
        acc = None
        groups = CONV_CHUNK // SUBLANES
        for r in sorted({r for _, r, _ in taps(width)}):
            mine = [(j, q) for j, r_j, q in taps(width) if r_j == r]
            q_lo, q_hi = min(q for _, q in mine), max(q for _, q in mine)
            span = xs[r, pl.ds(s0 + SUBLANES * q_lo, CONV_CHUNK + SUBLANES * (q_hi - q_lo)), :]
            span = span.reshape(groups + q_hi - q_lo, SUBLANES, -1)
            for j, q in mine:
                term = span[q - q_lo:q - q_lo + groups] * w_ref[j]
                acc = term if acc is None else acc + term
        return acc.reshape(CONV_CHUNK, -1)

    def chunks(body):
        for c in range(ts // CONV_CHUNK):
            body(c * CONV_CHUNK)

    fill(gx_ref, gxp_ref, gxn_ref, SHORT_CONV, xb_ref, xs_ref)

    def short(s0):
        out = pl.ds(s0, CONV_CHUNK)
        yc_ref[0, out, :] = (gb_ref[0, out, :].astype(F32) * conv(scw_ref, SHORT_CONV, s0, xs_ref)).astype(BF16)
    chunks(short)

    fill(u_ref, up_ref, un_ref, CONF_CONV, xb_ref, xs_ref)

    def conf(s0):
        u_ref_f32[pl.ds(s0, CONV_CHUNK), :] = conv(dww_ref, CONF_CONV, s0, xs_ref) + dwb_ref[...]
    chunks(conf)

    u = u_ref_f32[...]
    mu = jnp.mean(u, axis=-1, keepdims=True)
    uc = u - mu
    var = jnp.mean(uc * uc, axis=-1, keepdims=True)
    y = uc * lax.rsqrt(var + EPS) * lng_ref[...] + lnb_ref[...]
    yu_ref[0] = (y * jax.nn.sigmoid(y)).astype(BF16)


def _conv_mix(gb, gx, glu, scw, dww, dwb, lng, lnb, ts):
    bsz, s_len, _ = gb.shape
    nt = s_len // ts
    r = ts // HALO
    nh = s_len // HALO
    cur = pl.BlockSpec((1, ts, 512), lambda b, t: (b, t, 0))
    prev = pl.BlockSpec((1, HALO, 512), lambda b, t: (b, jnp.maximum(t * r - 1, 0), 0))
    nxt = pl.BlockSpec((1, HALO, 512), lambda b, t: (b, jnp.minimum((t + 1) * r, nh - 1), 0))
    out_spec = pl.BlockSpec((1, ts, 512), lambda b, t: (b, t, 0))
    assert ts % SHIFT_ROWS == 0
    i, j = jnp.arange(SHIFT_ROWS)[:, None], jnp.arange(SHIFT_K)[None, :]
    shift = jnp.stack([(j == i + r) for r in range(1, SUBLANES)]).astype(BF16)
    return pl.pallas_call(
        functools.partial(_conv_mix_kernel, ts=ts),
        grid=(bsz, nt),
        in_specs=[cur, cur, prev, nxt, cur, prev, nxt, _const_spec(shift.shape),
                  _const_spec((SHORT_CONV, SUBLANES, C_WIDTH)), _const_spec((CONF_CONV, SUBLANES, D_WIDTH)),
                  _const_spec((1, D_WIDTH)), _const_spec((1, D_WIDTH)), _const_spec((1, D_WIDTH))],
        out_specs=[out_spec, out_spec],
        out_shape=[jax.ShapeDtypeStruct((bsz, s_len, C_WIDTH), BF16),
                   jax.ShapeDtypeStruct((bsz, s_len, D_WIDTH), BF16)],
        scratch_shapes=[pltpu.VMEM((ts + LANES, 512), BF16),
                        pltpu.VMEM((SUBLANES, ts + 2 * HALO, 512), F32), pltpu.VMEM((ts, D_WIDTH), F32)],
        compiler_params=_cparams(("parallel", "parallel")),
        name="conv_mix",
    )(gb, gx, gx, gx, glu, glu, glu, shift, scw, dww, dwb, lng, lnb)


def _rel_bucket(rel):
    half = NUM_BUCKETS // 2
    max_exact = half // 2
    n = jnp.abs(rel)
    large = max_exact + (jnp.log(jnp.maximum(n, 1).astype(F32) / max_exact)
                         / math.log(MAX_DISTANCE / max_exact) * (half - max_exact)).astype(jnp.int32)
    large = jnp.minimum(large, half - 1)
    return jnp.where(rel > 0, half, 0) + jnp.where(n < max_exact, n, large)


def _toeplitz(u, rows, cols):
    length = u.shape[-1]
    flat = jnp.tile(u, (1,) * (u.ndim - 1) + (rows,))[..., :rows * (length - 1)]
    return flat.reshape(u.shape[:-1] + (rows, length - 1))[..., :cols]


LOG2E = math.log2(math.e)


def _win_bias(rel_bias, a_sink, a_qn, a_kn):
    order = jnp.array(WIN_HEAD_ORDER)
    table = rel_bias[:, :A_Q_HEADS].astype(F32) * LOG2E
    length = 4 * BLOCK
    n = jnp.arange(length)
    rel = jnp.where(n < BLOCK, -n, length - n) - BLOCK
    vec = jnp.where((jnp.abs(rel) <= WINDOW)[None], table[_rel_bucket(rel)].T, NEG)
    base = _toeplitz(vec, 3 * BLOCK, BLOCK)[order]
    mid = base.transpose(1, 0, 2).reshape(3 * BLOCK, A_Q_HEADS * BLOCK)
    key_block = (jnp.arange(3 * BLOCK) // BLOCK)[:, None]
    bias = jnp.stack([jnp.where(key_block == 0, NEG, mid), mid, jnp.where(key_block == 2, NEG, mid)], axis=0)
    sink = a_sink.astype(F32) * LOG2E
    s_max = 1.02 * LOG2E * math.sqrt(HEAD_DIM) * jnp.max(jnp.abs(a_qn * a_kn))
    bound = s_max + jnp.maximum(jnp.max(table, axis=0), sink)
    lowest = jnp.minimum(-s_max + jnp.min(table, axis=0), sink) - bound
    columns = lambda per_head: jnp.repeat(per_head[order], BLOCK)[None, :]
    return {"bias": bias, "sink": columns(sink),
            "bias_shifted": bias - columns(bound)[None], "sink_shifted": columns(sink - bound),
            "bounded_ok": jnp.all(lowest > MIN_EXP2_ARG)}
MIN_EXP2_ARG = -120.0


def _diff_bias(rel_bias, b_qn, b_kn, tile):
    table = rel_bias[:, A_Q_HEADS:].astype(F32) * LOG2E
    length = 2 * tile
    n = jnp.arange(length)
    k_minus_q = jnp.where(n < tile, -n, length - n)
    rel = jnp.clip(k_minus_q[None, :] + jnp.array([-tile, 0, tile])[:, None], 1 - length, length - 1)
    near = _toeplitz(table[_rel_bucket(rel)].transpose(2, 0, 1), tile, tile)
    far = table[_rel_bucket(jnp.array([-length, length]))].T
    s_max = 1.02 * LOG2E * math.sqrt(HEAD_DIM) * jnp.max(jnp.abs(b_qn * b_kn))
    b_max, b_min = jnp.max(table, axis=0), jnp.min(table, axis=0)
    bound = s_max + b_max
    const = lambda col: jnp.broadcast_to((far[:, col] - bound)[:, None, None, None], (B_HEADS, 1, tile, tile))
    shifted = jnp.concatenate([const(0), near - bound[:, None, None, None], const(1)], axis=1)
    bounded_ok = jnp.all(-2.0 * s_max - (b_max - b_min) > MIN_EXP2_ARG)
    return {"near": near, "far": far, "shifted": shifted, "bounded_ok": bounded_ok}


def _by_win_head(m, axis):
    take = lambda h: lax.slice_in_dim(m, h * HEAD_DIM, (h + 1) * HEAD_DIM, axis=axis)
    return jnp.concatenate([take(h) for h in WIN_HEAD_ORDER], axis=axis)


def _attn_in_params(w_in, a_qn, a_kn, b_qn, b_kn):
    d = HEAD_DIM
    k0 = A_Q_HEADS * d
    w = jnp.concatenate([_by_win_head(w_in[:, :k0], 1), w_in[:, k0:]], axis=1).astype(BF16)
    scale = HEAD_DIM ** -0.5 * LOG2E
    ones = lambda n: jnp.ones((n,), F32)
    gain = jnp.concatenate([
        jnp.tile(a_qn, A_Q_HEADS) * scale, jnp.tile(a_kn, A_KV_HEADS), ones(A_KV_HEADS * d),
        jnp.tile(b_qn, 2 * B_HEADS) * scale, jnp.tile(b_kn, 2 * B_HEADS), ones(2 * B_HEADS * d),
    ]).astype(F32)[None, :]
    seg_id = jnp.arange(MXU_COLS) // d
    seg = jnp.where(seg_id[:, None] == seg_id[None, :], 1.0 / d, 0.0).astype(BF16)
    return w, gain, seg


def _trunk(x, p, *, tm, tile, ts, qb):
    qa, ka, va, qb_, kb, vb = _attn_in(x, p["mix_g"][0], p["attn_w"], p["attn_gain"], p["seg"], tm)
    ya = _win_attn(qa, ka, va, p["win_bias"], qb)
    yb = _diff_attn(qb_, kb, vb, p["diff_bias"][tile], p["lam"], p["subln"], p["lam_init"], tile)
    x = _mix_ffn(x, ya, yb, p["attn_wo_a"], p["attn_wo_b"],
                 p["ffn_g"][0], p["wg"][0], p["wu"][0], p["wd"][0], tm)

    gb, gx, glu = _conv_in(x, p["mix_g"][1], p["conv_w"], tm)
    yc, yu = _conv_mix(gb, gx, glu, p["scw"], p["dww"], p["dwb"], p["lng"], p["lnb"], ts)
    return _mix_ffn(x, yc, yu, p["conv_wo_a"], p["conv_wo_b"],
                    p["ffn_g"][1], p["wg"][1], p["wu"][1], p["wd"][1], tm)


def _prepare(rel_bias, mix_norm, ffn_norm, w_gate, w_up, w_down, attn_w_in, attn_w_out, a_q_norm, a_k_norm,
             a_sink, b_q_norm, b_k_norm, b_lambda, b_subln, conv_w_in, conv_w_out, short_conv_w, conf_dw_w,
             conf_dw_b, conf_ln_g, conf_ln_b, tiles):
    attn_w, attn_gain, seg = _attn_in_params(attn_w_in[0], a_q_norm[0], a_k_norm[0], b_q_norm[0], b_k_norm[0])
    half = A_Q_HEADS * HEAD_DIM
    return {
        "mix_g": [mix_norm[l][None, :].astype(F32) for l in range(2)],
        "ffn_g": [ffn_norm[l][None, :].astype(F32) for l in range(2)],
        "wg": [w_gate[l].astype(BF16) for l in range(2)],
        "wu": [w_up[l].astype(BF16) for l in range(2)],
        "wd": [w_down[l].astype(BF16) for l in range(2)],
        "attn_w": attn_w, "attn_gain": attn_gain, "seg": seg,
        "attn_wo_a": _by_win_head(attn_w_out[0][:half], 0).astype(BF16),
        "attn_wo_b": attn_w_out[0][half:].astype(BF16),
        "win_bias": _win_bias(rel_bias, a_sink[0], a_q_norm[0], a_k_norm[0]),
        "diff_bias": {t: _diff_bias(rel_bias, b_q_norm[0], b_k_norm[0], t) for t in tiles},
        "lam": b_lambda[0].astype(F32),
        "subln": b_subln[0].astype(F32)[:, None],
        "lam_init": 0.8 - 0.6 * math.exp(-0.3 * 0),
        "conv_w": conv_w_in[0].astype(BF16),
        "conv_wo_a": conv_w_out[0][:C_WIDTH].astype(BF16), "conv_wo_b": conv_w_out[0][C_WIDTH:].astype(BF16),
        "scw": jnp.broadcast_to(short_conv_w[0].astype(F32)[:, None, :], (SHORT_CONV, SUBLANES, C_WIDTH)),
        "dww": jnp.broadcast_to(conf_dw_w[0].astype(F32)[:, None, :], (CONF_CONV, SUBLANES, D_WIDTH)),
        "dwb": conf_dw_b[0][None, :].astype(F32),
        "lng": conf_ln_g[0][None, :].astype(F32), "lnb": conf_ln_b[0][None, :].astype(F32),
    }


def _tiling(s_len):
    return dict(tm=min(1024, s_len), tile=min(512, s_len), ts=min(512, s_len), qb=8)


def kernel(x_prompt, x_sample, rel_bias, mix_norm, ffn_norm, w_gate, w_up, w_down, attn_w_in, attn_w_out,
           a_q_norm, a_k_norm, a_sink, b_q_norm, b_k_norm, b_lambda, b_subln, conv_w_in, conv_w_out,
           short_conv_w, conf_dw_w, conf_dw_b, conf_ln_g, conf_ln_b):
    tp, tsm = _tiling(x_prompt.shape[1]), _tiling(x_sample.shape[1])
    p = _prepare(rel_bias, mix_norm, ffn_norm, w_gate, w_up, w_down, attn_w_in, attn_w_out, a_q_norm,
                 a_k_norm, a_sink, b_q_norm, b_k_norm, b_lambda, b_subln, conv_w_in, conv_w_out,
                 short_conv_w, conf_dw_w, conf_dw_b, conf_ln_g, conf_ln_b, {tp["tile"], tsm["tile"]})
    return (_trunk(x_prompt, p, **tp), _trunk(x_sample, p, **tsm))
```

```python
import functools
import math

import jax
import jax.numpy as jnp
from jax import lax
from jax.experimental import pallas as pl
from jax.experimental.pallas import tpu as pltpu

D_MODEL = 1024
HEAD_DIM = 64
A_Q_HEADS = 8
A_KV_HEADS = 2
WINDOW = 128
BLOCK = 128
B_HEADS = 4
NUM_BUCKETS = 32
MAX_DISTANCE = 128
C_WIDTH = 512
D_WIDTH = 512
SHORT_CONV = 3
CONF_CONV = 31
FFN_HIDDEN = 2816
EPS = 1e-6
NEG = -1e30

LANES = 128
MXU_COLS = 256
VMEM_LIMIT_BYTES = 56 * 1024 * 1024

BF16 = jnp.bfloat16
F32 = jnp.float32

QA_W = A_Q_HEADS * HEAD_DIM
PROJ0_W = QA_W + 2 * A_KV_HEADS * HEAD_DIM + 3 * B_HEADS * 2 * HEAD_DIM
NORM_CHUNKS0 = ("all", "all", "low", "all", "all", "all", "all", None, None)
WIN_HEAD_ORDER = (0, 4, 1, 5, 2, 6, 3, 7)

HALO = 16


def _cparams(sem):
    return pltpu.CompilerParams(dimension_semantics=sem, vmem_limit_bytes=VMEM_LIMIT_BYTES)


def _const_spec(shape):
    nd = len(shape)
    return pl.BlockSpec(shape, lambda *_: (0,) * nd)


def _rms(x, g):
    ms = jnp.mean(x * x, axis=-1, keepdims=True)
    return x * lax.rsqrt(ms + EPS) * g


def _attn_in_kernel(x_ref, g_ref, w_ref, gain_ref, seg_ref, qa_ref, ka_ref, va_ref, qb_ref, kb_ref, vb_ref):
    half = MXU_COLS // 2

    def store(c, y):
        if c < 2:
            qa_ref[:, c * MXU_COLS:(c + 1) * MXU_COLS] = y
        elif c == 2:
            ka_ref[...] = y[:, :half]
            va_ref[...] = y[:, half:]
        else:
            ref, first = ((qb_ref, 3), (kb_ref, 5), (vb_ref, 7))[(c - 3) // 2]
            ref[0, 2 * (c - first)] = y[:, :half]
            ref[0, 2 * (c - first) + 1] = y[:, half:]

    h = _rms(x_ref[...], g_ref[...]).astype(BF16)
    seg = seg_ref[...]
    low = lax.broadcasted_iota(jnp.int32, (1, MXU_COLS), 1) < MXU_COLS // 2
    chunk = lambda c: slice(c * MXU_COLS, (c + 1) * MXU_COLS)
    accs = [jnp.dot(h, w_ref[:, chunk(c)], preferred_element_type=F32) for c in range(len(NORM_CHUNKS0))]
    for c, normed in enumerate(NORM_CHUNKS0):
        acc = accs[c]
        if normed is not None:
            sq = acc * acc
            sq_hi = sq.astype(BF16)
            sq_lo = (sq - sq_hi.astype(F32)).astype(BF16)
            ms = (jnp.dot(sq_hi, seg, preferred_element_type=F32)
                  + jnp.dot(sq_lo, seg, preferred_element_type=F32))
            scale = lax.rsqrt(ms + EPS) * gain_ref[:, chunk(c)]
            acc = acc * (scale if normed == "all" else jnp.where(low, scale, 1.0))
        store(c, acc.astype(BF16))


def _attn_in(x, g, w, gain, seg, tm):
    bsz, s_len, _ = x.shape
    spt = s_len // tm
    flat = lambda width: pl.BlockSpec((1, tm, width), lambda i: (i // spt, i % spt, 0))
    heads = pl.BlockSpec((1, B_HEADS, tm, LANES), lambda i: (i // spt, 0, i % spt, 0))
    sds = lambda *shape: jax.ShapeDtypeStruct(shape, BF16)
    head_major = sds(bsz, B_HEADS, s_len, LANES)

    def body(x_ref, g_ref, w_ref, gain_ref, seg_ref, qa_ref, ka_ref, va_ref, qb_ref, kb_ref, vb_ref):
        _attn_in_kernel(x_ref.at[0], g_ref, w_ref, gain_ref, seg_ref,
                        qa_ref.at[0], ka_ref.at[0], va_ref.at[0], qb_ref, kb_ref, vb_ref)

    return pl.pallas_call(
        body,
        grid=(bsz * spt,),
        in_specs=[
            flat(D_MODEL),
            _const_spec((1, D_MODEL)),
            _const_spec((D_MODEL, PROJ0_W)),
            _const_spec((1, PROJ0_W)),
            _const_spec((MXU_COLS, MXU_COLS)),
        ],
        out_specs=[flat(QA_W), flat(LANES), flat(LANES), heads, heads, heads],
        out_shape=[sds(bsz, s_len, QA_W), sds(bsz, s_len, LANES), sds(bsz, s_len, LANES),
                   head_major, head_major, head_major],
        compiler_params=_cparams(("parallel",)),
        name="attn_in",
    )(x, g, w, gain, seg)


WIN_COLS = A_Q_HEADS * BLOCK


def _win_attn_kernel(q_ref, kp_ref, kc_ref, kn_ref, vp_ref, vc_ref, vn_ref, bias_ref, sink_ref, o_ref, *,
                     qb, bounded):
    step = pl.program_id(1)
    nb = pl.num_programs(1) * qb
    low = lax.broadcasted_iota(jnp.int32, (1, LANES), 1) < HEAD_DIM
    top = lax.broadcasted_iota(jnp.int32, (LANES, 1), 0) < HEAD_DIM
    sink = sink_ref[...]

    def piece(prev_ref, cur_ref, next_ref, i):
        if i == 0:
            return prev_ref[0]
        if i == qb + 1:
            return next_ref[0]
        return cur_ref[0, (i - 1) * BLOCK:i * BLOCK, :]

    def scores(b):
        kcat = jnp.concatenate([piece(kp_ref, kc_ref, kn_ref, b + i) for i in range(3)], axis=0)
        q = q_ref[0, b * BLOCK:(b + 1) * BLOCK, :]
        zero = jnp.zeros((BLOCK, LANES), BF16)
        halves = []
        for j in range(A_Q_HEADS // 2):
            slab = q[:, j * LANES:(j + 1) * LANES]
            halves += [jnp.where(low, slab, zero), jnp.where(low, zero, slab)]
        qcat = jnp.concatenate(halves, axis=0)
        return lax.dot_general(kcat, qcat, _NT, preferred_element_type=F32)

    def weights(b, s_t):
        n = step * qb + b
        edge = jnp.where(n == 0, 0, jnp.where(n == nb - 1, 2, 1))
        s_t = s_t + bias_ref[edge]
        if bounded:
            p = jnp.exp2(s_t)
            den = jnp.sum(p, axis=0, keepdims=True) + jnp.exp2(sink)
        else:
            m = jnp.maximum(jnp.max(s_t, axis=0, keepdims=True), sink)
            p = jnp.exp2(s_t - m)
            den = jnp.sum(p, axis=0, keepdims=True) + jnp.exp2(sink - m)
        return p.astype(BF16), den

    def values(b, p, den):
        vcat = jnp.concatenate([piece(vp_ref, vc_ref, vn_ref, b + i) for i in range(3)], axis=0)
        return lax.dot_general(vcat, p, _TN, preferred_element_type=F32) / den

    s_all = [scores(b) for b in range(qb)]
    w_all = [weights(b, s_all[b]) for b in range(qb)]
    o_all = [values(b, *w_all[b]) for b in range(qb)]
    for b in range(qb):
        for j in range(A_Q_HEADS // 2):
            c0 = 2 * j * BLOCK
            slab_t = jnp.where(top, o_all[b][:, c0:c0 + BLOCK], o_all[b][:, c0 + BLOCK:c0 + 2 * BLOCK])
            o_ref[0, b * BLOCK:(b + 1) * BLOCK, j * LANES:(j + 1) * LANES] = slab_t.T.astype(BF16)


def _win_attn(qa, ka, va, wb, qb):
    fast = lambda: _win_attn_call(qa, ka, va, wb["sink_shifted"], wb["bias_shifted"], qb, True)
    exact_max = lambda: _win_attn_call(qa, ka, va, wb["sink"], wb["bias"], qb, False)
    return lax.cond(wb["bounded_ok"], fast, exact_max)


def _win_attn_call(qa, ka, va, sink_row, bias, qb, bounded):
    bsz, s_len, _ = qa.shape
    nb = s_len // BLOCK
    assert nb % qb == 0 and nb >= 2
    kv_specs = [
        pl.BlockSpec((1, BLOCK, LANES), lambda b, t: (b, jnp.maximum(t * qb - 1, 0), 0)),
        pl.BlockSpec((1, qb * BLOCK, LANES), lambda b, t: (b, t, 0)),
        pl.BlockSpec((1, BLOCK, LANES), lambda b, t: (b, jnp.minimum((t + 1) * qb, nb - 1), 0)),
    ]
    return pl.pallas_call(
        functools.partial(_win_attn_kernel, qb=qb, bounded=bounded),
        grid=(bsz, nb // qb),
        in_specs=[pl.BlockSpec((1, qb * BLOCK, QA_W), lambda b, t: (b, t, 0))] + kv_specs + kv_specs
                 + [pl.BlockSpec(memory_space=pltpu.VMEM), pl.BlockSpec(memory_space=pltpu.VMEM)],
        out_specs=pl.BlockSpec((1, qb * BLOCK, QA_W), lambda b, t: (b, t, 0)),
        out_shape=jax.ShapeDtypeStruct((bsz, s_len, QA_W), BF16),
        compiler_params=_cparams(("parallel", "parallel")),
        name="win_attn",
    )(qa, ka, ka, ka, va, va, va, bias, sink_row)


_NT = (((1,), (1,)), ((), ()))
_TN = (((0,), (0,)), ((), ()))


def _split_maps(q, qz_ref):
    lane = lax.broadcasted_iota(jnp.int32, (1, LANES), 1)
    zero = jnp.zeros_like(q)
    qz_ref[0] = jnp.where(lane < HEAD_DIM, q, zero)
    qz_ref[1] = jnp.where(lane < HEAD_DIM, zero, q)


def _diff_out(acc0, l0, acc1, l1, lam_ref, g_ref, lam_init):
    lam = lam_ref[...]
    lam_full = (jnp.exp(jnp.sum(lam[0:1] * lam[1:2], axis=-1, keepdims=True))
                - jnp.exp(jnp.sum(lam[2:3] * lam[3:4], axis=-1, keepdims=True)) + lam_init)
    o = acc0 / l0 - lam_full * (acc1 / l1)
    ms = jnp.mean(o * o, axis=0, keepdims=True)
    y = o * lax.rsqrt(ms + EPS) * g_ref[...] * (1.0 - lam_init)
    return y.T.astype(BF16)


def _diff_attn_bounded_kernel(q_ref, k_ref, v_ref, bias_ref, lam_ref, g_ref, o_ref,
                              qz_ref, s_ref, l_ref, acc_ref, *, lam_init, tile, qtiles):
    qstep = pl.program_id(2)
    nk = k_ref.shape[1] // tile
    first_map = lax.broadcasted_iota(jnp.int32, (LANES, 1), 0) < HEAD_DIM
    for qt in range(qtiles):
        q_t = q_ref[0, qt * tile:(qt + 1) * tile, :].astype(F32).T
        qz_ref[qt, 0] = jnp.where(first_map, q_t, 0.0).astype(BF16)
        qz_ref[qt, 1] = jnp.where(first_map, 0.0, q_t).astype(BF16)
    l_ref[...] = jnp.zeros(l_ref.shape, F32)
    acc_ref[...] = jnp.zeros(acc_ref.shape, F32)

    def produce(qt, ki, buf):
        k = k_ref[0, ki * tile:(ki + 1) * tile, :]
        for c in range(2):
            s_ref[buf, c] = jnp.dot(k, qz_ref[qt, c], preferred_element_type=F32)

    def consume(qt, ki, buf):
        v = v_ref[0, ki * tile:(ki + 1) * tile, :]
        bias = bias_ref[0, jnp.clip(ki - (qstep * qtiles + qt) + 2, 0, 4)]
        for c in range(2):
            p = jnp.exp2(s_ref[buf, c] + bias)
            l_ref[qt, c] += jnp.sum(p.reshape(tile // 8, 8, tile), axis=0)
            acc_ref[qt, c] += lax.dot_general(v, p.astype(BF16), _TN, preferred_element_type=F32)

    def finish(qt):
        l0 = jnp.sum(l_ref[qt, 0], axis=0, keepdims=True)
        l1 = jnp.sum(l_ref[qt, 1], axis=0, keepdims=True)
        o_ref[0, qt * tile:(qt + 1) * tile, :] = _diff_out(acc_ref[qt, 0], l0, acc_ref[qt, 1], l1,
                                                           lam_ref, g_ref, lam_init)

    steps =[(qt, ki) for qt in range(qtiles) for ki in range(nk)]
    produce(*steps[0], 0)
    for i, (qt, ki) in enumerate(steps):
        if i + 1 < len(steps):
            produce(*steps[i + 1], (i + 1) % 2)
        consume(qt, ki, i % 2)
        if ki == nk - 1:
            finish(qt)


DIFF_STEPS = 16


def _diff_attn_bounded(qb, kb, vb, bias_t, lam, g_col, lam_init, tile):
    bsz, _, s_len, _ = qb.shape
    nk = s_len // tile
    assert nk <= DIFF_STEPS and DIFF_STEPS % nk == 0 and nk % 2 == 0
    qtiles = min(DIFF_STEPS // nk, nk)
    assert nk % qtiles == 0
    kernel = functools.partial(_diff_attn_bounded_kernel, lam_init=lam_init, tile=tile, qtiles=qtiles)

    def body(q_ref, k_ref, v_ref, bias_ref, lam_ref, g_ref, o_ref, *scratch):
        kernel(q_ref.at[0], k_ref.at[0], v_ref.at[0], bias_ref, lam_ref, g_ref, o_ref.at[0], *scratch)

    q_rows = qtiles * tile
    return pl.pallas_call(
        body,
        grid=(bsz, B_HEADS, s_len // q_rows),
        in_specs=[
            pl.BlockSpec((1, 1, q_rows, LANES), lambda b, h, qi: (b, h, qi, 0)),
            pl.BlockSpec((1, 1, s_len, LANES), lambda b, h, qi: (b, h, 0, 0)),
            pl.BlockSpec((1, 1, s_len, LANES), lambda b, h, qi: (b, h, 0, 0)),
            pl.BlockSpec((1, 5, tile, tile), lambda b, h, qi: (h, 0, 0, 0)),
            _const_spec((4, HEAD_DIM)),
            _const_spec((2 * HEAD_DIM, 1)),
        ],
        out_specs=pl.BlockSpec((1, 1, q_rows, LANES), lambda b, h, qi: (b, h, qi, 0)),
        out_shape=jax.ShapeDtypeStruct((bsz, B_HEADS, s_len, LANES), BF16),
        scratch_shapes=[
            pltpu.VMEM((qtiles, 2, LANES, tile), BF16),
            pltpu.VMEM((2, 2, tile, tile), F32),
            pltpu.VMEM((qtiles, 2, 8, tile), F32),
            pltpu.VMEM((qtiles, 2, 2 * HEAD_DIM, tile), F32),
        ],
        compiler_params=_cparams(("parallel", "parallel", "parallel")),
        name="diff_attn",
    )(qb, kb, vb, bias_t, lam, g_col)


def _diff_attn_online_kernel(far_ref, q_ref, k_ref, v_ref, bias_ref, lam_ref, g_ref, o_ref,
                             qz_ref, m_ref, l_ref, acc_ref, *, lam_init):
    h = pl.program_id(1)
    qi = pl.program_id(2)
    ki = pl.program_id(3)
    nk = pl.num_programs(3)

    @pl.when(ki == 0)
    def _init():
        _split_maps(q_ref[0], qz_ref)
        m_ref[...] = jnp.full(m_ref.shape, NEG, F32)
        l_ref[...] = jnp.zeros(l_ref.shape, F32)
        acc_ref[...] = jnp.zeros(acc_ref.shape, F32)

    delta = ki - qi

    def step(c, bias_tile, bias_row):
        s_t = lax.dot_general(k_ref[0], qz_ref[c], _NT, preferred_element_type=F32)
        if bias_tile is not None:
            s_t = s_t + bias_tile
        m_old = m_ref[c]
        m_new = jnp.maximum(m_old, jnp.max(s_t, axis=0, keepdims=True) + bias_row)
        alpha = jnp.exp2(m_old - m_new)
        p = jnp.exp2(s_t - (m_new - bias_row))
        l_ref[c] = alpha * l_ref[c] + jnp.sum(p, axis=0, keepdims=True)
        pv = lax.dot_general(v_ref[0], p.astype(BF16), _TN, preferred_element_type=F32)
        acc_ref[c] = alpha * acc_ref[c] + pv
        m_ref[c] = m_new

    @pl.when(jnp.abs(delta) <= 1)
    def _near():
        bias_tile = bias_ref[0, delta + 1]
        for c in range(2):
            step(c, bias_tile, 0.0)

    @pl.when(jnp.abs(delta) > 1)
    def _far():
        side = jnp.where(delta < 0, far_ref[h, 0], far_ref[h, 1])
        for c in range(2):
            step(c, None, side)

    @pl.when(ki == nk - 1)
    def _finish():
        o_ref[0] = _diff_out(acc_ref[0], l_ref[0], acc_ref[1], l_ref[1], lam_ref, g_ref, lam_init)


def _diff_attn_online(qb, kb, vb, far, bias_t, lam, g_col, lam_init, tile):
    bsz, _, s_len, _ = qb.shape
    nt = s_len // tile
    kernel = functools.partial(_diff_attn_online_kernel, lam_init=lam_init)

    def body(far_ref, q_ref, k_ref, v_ref, bias_ref, lam_ref, g_ref, o_ref, *scratch):
        kernel(far_ref, q_ref.at[0], k_ref.at[0], v_ref.at[0], bias_ref, lam_ref, g_ref, o_ref.at[0], *scratch)

    return pl.pallas_call(
        body,
        grid=(bsz, B_HEADS, nt, nt),
        in_specs=[
            pl.BlockSpec(memory_space=pltpu.SMEM),
            pl.BlockSpec((1, 1, tile, LANES), lambda b, h, qi, ki: (b, h, qi, 0)),
            pl.BlockSpec((1, 1, tile, LANES), lambda b, h, qi, ki: (b, h, ki, 0)),
            pl.BlockSpec((1, 1, tile, LANES), lambda b, h, qi, ki: (b, h, ki, 0)),
            pl.BlockSpec((1, 3, tile, tile), lambda b, h, qi, ki: (h, 0, 0, 0)),
            _const_spec((4, HEAD_DIM)),
            _const_spec((2 * HEAD_DIM, 1)),
        ],
        out_specs=pl.BlockSpec((1, 1, tile, LANES), lambda b, h, qi, ki: (b, h, qi, 0)),
        out_shape=jax.ShapeDtypeStruct((bsz, B_HEADS, s_len, LANES), BF16),
        scratch_shapes=[
            pltpu.VMEM((2, tile, LANES), BF16),
            pltpu.VMEM((2, 1, tile), F32),
            pltpu.VMEM((2, 1, tile), F32),
            pltpu.VMEM((2, 2 * HEAD_DIM, tile), F32),
        ],
        compiler_params=_cparams(("parallel", "parallel", "parallel", "arbitrary")),
        name="diff_attn_online",
    )(far, qb, kb, vb, bias_t, lam, g_col)


def _diff_attn(qb, kb, vb, db, lam, g_col, lam_init, tile):
    bounded = lambda: _diff_attn_bounded(qb, kb, vb, db["shifted"], lam, g_col, lam_init, tile)
    online = lambda: _diff_attn_online(qb, kb, vb, db["far"], db["near"], lam, g_col, lam_init, tile)
    return lax.cond(db["bounded_ok"], bounded, online)


FFN_CHUNKS = ((0, 1536), (1536, FFN_HIDDEN))


def _mix_ffn_kernel(x_ref, a_ref, b_ref, wa_ref, wb_ref, g_ref, wg_ref, wu_ref, wd_ref, o_ref):
    if len(b_ref.shape) == 4:
        b = jnp.concatenate([b_ref[0, hd] for hd in range(b_ref.shape[1])], axis=-1)
    else:
        b = b_ref[0]
    mix = (jnp.dot(a_ref[0], wa_ref[...], preferred_element_type=F32)
           + jnp.dot(b, wb_ref[...], preferred_element_type=F32))
    x1 = x_ref[0] + mix
    h = _rms(x1, g_ref[...]).astype(BF16)
    gates = [jnp.dot(h, wg_ref[:, lo:hi], preferred_element_type=F32) for lo, hi in FFN_CHUNKS]
    ups = [jnp.dot(h, wu_ref[:, lo:hi], preferred_element_type=F32) for lo, hi in FFN_CHUNKS]
    down = None
    for (lo, hi), gate, up in zip(FFN_CHUNKS, gates, ups):
        act = (gate * jax.nn.sigmoid(gate) * up).astype(BF16)
        part = jnp.dot(act, wd_ref[lo:hi, :], preferred_element_type=F32)
        down = part if down is None else down + part
    o_ref[0] = x1 + down


def _mix_ffn(x, a, b, wa, wb, g, wg, wu, wd, tm):
    bsz, s_len, _ = x.shape
    spt = s_len // tm
    row_spec = lambda w: pl.BlockSpec((1, tm, w), lambda i: (i // spt, i % spt, 0))
    if b.ndim == 4:
        b_spec = pl.BlockSpec((1, b.shape[1], tm, LANES), lambda i: (i // spt, 0, i % spt, 0))
    else:
        b_spec = row_spec(b.shape[-1])
    resident = pl.BlockSpec(memory_space=pltpu.VMEM)
    return pl.pallas_call(
        _mix_ffn_kernel,
        grid=(bsz * spt,),
        in_specs=[row_spec(D_MODEL), row_spec(a.shape[-1]), b_spec] + [resident] * 6,
        out_specs=row_spec(D_MODEL),
        out_shape=jax.ShapeDtypeStruct(x.shape, F32),
        compiler_params=_cparams(("parallel",)),
        name="mix_ffn",
    )(x, a, b, wa, wb, g, wg, wu, wd)


def _conv_in_kernel(x_ref, g_ref, w_ref, gb_ref, gx_ref, glu_ref):
    h = _rms(x_ref[0], g_ref[...]).astype(BF16)
    c = C_WIDTH
    gb, gc, xc, a, gate = [jnp.dot(h, w_ref[:, i * c:(i + 1) * c], preferred_element_type=F32) for i in range(5)]
    gb_ref[0] = gb.astype(BF16)
    gx_ref[0] = (gc * xc).astype(BF16)
    glu_ref[0] = (a * jax.nn.sigmoid(gate)).astype(BF16)


def _conv_in(x, g, w, tm):
    bsz, s_len, _ = x.shape
    spt = s_len // tm
    row_spec = lambda width: pl.BlockSpec((1, tm, width), lambda i: (i // spt, i % spt, 0))
    out = jax.ShapeDtypeStruct((bsz, s_len, C_WIDTH), BF16)
    return pl.pallas_call(
        _conv_in_kernel,
        grid=(bsz * spt,),
        in_specs=[row_spec(D_MODEL), _const_spec((1, D_MODEL)), _const_spec((D_MODEL, w.shape[1]))],
        out_specs=[row_spec(C_WIDTH)] * 3,
        out_shape=[out] * 3,
        compiler_params=_cparams(("parallel",)),
        name="conv_in",
    )(x, g, w)


CONV_CHUNK = 32
SUBLANES = 8


SHIFT_ROWS = 256
SHIFT_K = 384


def _conv_mix_kernel(gb_ref, gx_ref, gxp_ref, gxn_ref, u_ref, up_ref, un_ref, shift_ref,
                     scw_ref, dww_ref, dwb_ref, lng_ref, lnb_ref, yc_ref, yu_ref,
                     xb_ref, xs_ref, u_ref_f32, *, ts):
    t = pl.program_id(1)
    nt = pl.num_programs(1)
    padded = ts + 2 * HALO

    def taps(width):
        return [(j,) + divmod(HALO - width // 2 + j, SUBLANES)[::-1] for j in range(width)]

    def fill(cur, prev, nxt, width, xb, xs):
        zero = jnp.zeros((HALO, C_WIDTH), BF16)
        xb[0:HALO, :] = jnp.where(t > 0, prev[0], zero)
        xb[HALO:HALO + ts, :] = cur[0]
        xb[HALO + ts:padded, :] = jnp.where(t < nt - 1, nxt[0], zero)
        xb[padded:, :] = jnp.zeros((xb.shape[0] - padded, C_WIDTH), BF16)
        xs[0] = xb[0:padded, :].astype(F32)
        copies = sorted({r for _, r, _ in taps(width)} - {0})
        for o0 in range(0, ts, SHIFT_ROWS):
            for r in copies:
                xs[r, o0:o0 + SHIFT_ROWS, :] = jnp.dot(shift_ref[r - 1], xb[o0:o0 + SHIFT_K, :],
                                                       preferred_element_type=F32)
        for r in copies:
            xs[r, ts:padded, :] = jnp.dot(shift_ref[r - 1, 0:2 * HALO, 0:LANES], xb[ts:ts + LANES, :],
                                          preferred_element_type=F32)

    def conv(w_ref, width, s0, xs):
        acc = None
        groups = CONV_CHUNK // SUBLANES
        for r in sorted({r for _, r, _ in taps(width)}):
            mine = [(j, q) for j, r_j, q in taps(width) if r_j == r]
            q_lo, q_hi = min(q for _, q in mine), max(q for _, q in mine)
            span = xs[r, pl.ds(s0 + SUBLANES * q_lo, CONV_CHUNK + SUBLANES * (q_hi - q_lo)), :]
            span = span.reshape(groups + q_hi - q_lo, SUBLANES, -1)
            for j, q in mine:
                term = span[q - q_lo:q - q_lo + groups] * w_ref[j]
                acc = term if acc is None else acc + term
        return acc.reshape(CONV_CHUNK, -1)

    def chunks(body):
        for c in range(ts // CONV_CHUNK):
            body(c * CONV_CHUNK)

    fill(gx_ref, gxp_ref, gxn_ref, SHORT_CONV, xb_ref, xs_ref)

    def short(s0):
        out = pl.ds(s0, CONV_CHUNK)
        yc_ref[0, out, :] = (gb_ref[0, out, :].astype(F32) * conv(scw_ref, SHORT_CONV, s0, xs_ref)).astype(BF16)
    chunks(short)

    fill(u_ref, up_ref, un_ref, CONF_CONV, xb_ref, xs_ref)

    def conf(s0):
        u_ref_f32[pl.ds(s0, CONV_CHUNK), :] = conv(dww_ref, CONF_CONV, s0, xs_ref) + dwb_ref[...]
    chunks(conf)

    u = u_ref_f32[...]
    mu = jnp.mean(u, axis=-1, keepdims=True)
    uc = u - mu
    var = jnp.mean(uc * uc, axis=-1, keepdims=True)
    y = uc * lax.rsqrt(var + EPS) * lng_ref[...] + lnb_ref[...]
    yu_ref[0] = (y * jax.nn.sigmoid(y)).astype(BF16)


def _conv_mix(gb, gx, glu, scw, dww, dwb, lng, lnb, ts):
    bsz, s_len, _ = gb.shape
    nt = s_len // ts
    r = ts // HALO
    nh = s_len // HALO
    assert C_WIDTH == D_WIDTH
    cur = pl.BlockSpec((1, ts, C_WIDTH), lambda b, t: (b, t, 0))
    prev = pl.BlockSpec((1, HALO, C_WIDTH), lambda b, t: (b, jnp.maximum(t * r - 1, 0), 0))
    nxt = pl.BlockSpec((1, HALO, C_WIDTH), lambda b, t: (b, jnp.minimum((t + 1) * r, nh - 1), 0))
    out_spec = pl.BlockSpec((1, ts, C_WIDTH), lambda b, t: (b, t, 0))
    assert ts % SHIFT_ROWS == 0
    i, j = jnp.arange(SHIFT_ROWS)[:, None], jnp.arange(SHIFT_K)[None, :]
    shift = jnp.stack([(j == i + r) for r in range(1, SUBLANES)]).astype(BF16)
    return pl.pallas_call(
        functools.partial(_conv_mix_kernel, ts=ts),
        grid=(bsz, nt),
        in_specs=[cur, cur, prev, nxt, cur, prev, nxt, _const_spec(shift.shape),
                  _const_spec((SHORT_CONV, SUBLANES, C_WIDTH)), _const_spec((CONF_CONV, SUBLANES, D_WIDTH)),
                  _const_spec((1, D_WIDTH)), _const_spec((1, D_WIDTH)), _const_spec((1, D_WIDTH))],
        out_specs=[out_spec, out_spec],
        out_shape=[jax.ShapeDtypeStruct((bsz, s_len, C_WIDTH), BF16),
                   jax.ShapeDtypeStruct((bsz, s_len, D_WIDTH), BF16)],
        scratch_shapes=[pltpu.VMEM((ts + LANES, C_WIDTH), BF16),
                        pltpu.VMEM((SUBLANES, ts + 2 * HALO, C_WIDTH), F32), pltpu.VMEM((ts, D_WIDTH), F32)],
        compiler_params=_cparams(("parallel", "parallel")),
        name="conv_mix",
    )(gb, gx, gx, gx, glu, glu, glu, shift, scw, dww, dwb, lng, lnb)


def _rel_bucket(rel):
    half = NUM_BUCKETS // 2
    max_exact = half // 2
    n = jnp.abs(rel)
    large = max_exact + (jnp.log(jnp.maximum(n, 1).astype(F32) / max_exact)
                         / math.log(MAX_DISTANCE / max_exact) * (half - max_exact)).astype(jnp.int32)
    large = jnp.minimum(large, half - 1)
    return jnp.where(rel > 0, half, 0) + jnp.where(n < max_exact, n, large)


def _toeplitz(u, rows, cols):
    length = u.shape[-1]
    flat = jnp.tile(u, (1,) * (u.ndim - 1) + (rows,))[..., :rows * (length - 1)]
    return flat.reshape(u.shape[:-1] + (rows, length - 1))[..., :cols]


LOG2E = math.log2(math.e)


def _win_bias(rel_bias, a_sink, a_qn, a_kn):
    order = jnp.array(WIN_HEAD_ORDER)
    table = rel_bias[:, :A_Q_HEADS].astype(F32) * LOG2E
    length = 4 * BLOCK
    n = jnp.arange(length)
    rel = jnp.where(n < BLOCK, -n, length - n) - BLOCK
    vec = jnp.where((jnp.abs(rel) <= WINDOW)[None], table[_rel_bucket(rel)].T, NEG)
    base = _toeplitz(vec, 3 * BLOCK, BLOCK)[order]
    mid = base.transpose(1, 0, 2).reshape(3 * BLOCK, A_Q_HEADS * BLOCK)
    key_block = (jnp.arange(3 * BLOCK) // BLOCK)[:, None]
    bias = jnp.stack([jnp.where(key_block == 0, NEG, mid), mid, jnp.where(key_block == 2, NEG, mid)], axis=0)
    sink = a_sink.astype(F32) * LOG2E
    s_max = 1.02 * LOG2E * math.sqrt(HEAD_DIM) * jnp.max(jnp.abs(a_qn * a_kn))
    bound = s_max + jnp.maximum(jnp.max(table, axis=0), sink)
    lowest = jnp.minimum(-s_max + jnp.min(table, axis=0), sink) - bound
    columns = lambda per_head: jnp.repeat(per_head[order], BLOCK)[None, :]
    return {"bias": bias, "sink": columns(sink),
            "bias_shifted": bias - columns(bound)[None], "sink_shifted": columns(sink - bound),
            "bounded_ok": jnp.all(lowest > MIN_EXP2_ARG)}
MIN_EXP2_ARG = -120.0


def _diff_bias(rel_bias, b_qn, b_kn, tile):
    table = rel_bias[:, A_Q_HEADS:].astype(F32) * LOG2E
    length = 2 * tile
    n = jnp.arange(length)
    k_minus_q = jnp.where(n < tile, -n, length - n)
    rel = jnp.clip(k_minus_q[None, :] + jnp.array([-tile, 0, tile])[:, None], 1 - length, length - 1)
    near = _toeplitz(table[_rel_bucket(rel)].transpose(2, 0, 1), tile, tile)
    far = table[_rel_bucket(jnp.array([-length, length]))].T
    s_max = 1.02 * LOG2E * math.sqrt(HEAD_DIM) * jnp.max(jnp.abs(b_qn * b_kn))
    b_max, b_min = jnp.max(table, axis=0), jnp.min(table, axis=0)
    bound = s_max + b_max
    const = lambda col: jnp.broadcast_to((far[:, col] - bound)[:, None, None, None], (B_HEADS, 1, tile, tile))
    shifted = jnp.concatenate([const(0), near - bound[:, None, None, None], const(1)], axis=1)
    bounded_ok = jnp.all(-2.0 * s_max - (b_max - b_min) > MIN_EXP2_ARG)
    return {"near": near, "far": far, "shifted": shifted, "bounded_ok": bounded_ok}


def _by_win_head(m, axis):
    take = lambda h: lax.slice_in_dim(m, h * HEAD_DIM, (h + 1) * HEAD_DIM, axis=axis)
    return jnp.concatenate([take(h) for h in WIN_HEAD_ORDER], axis=axis)


def _attn_in_params(w_in, a_qn, a_kn, b_qn, b_kn):
    d = HEAD_DIM
    k0 = A_Q_HEADS * d
    w = jnp.concatenate([_by_win_head(w_in[:, :k0], 1), w_in[:, k0:]], axis=1).astype(BF16)
    scale = HEAD_DIM ** -0.5 * LOG2E
    ones = lambda n: jnp.ones((n,), F32)
    gain = jnp.concatenate([
        jnp.tile(a_qn, A_Q_HEADS) * scale, jnp.tile(a_kn, A_KV_HEADS), ones(A_KV_HEADS * d),
        jnp.tile(b_qn, 2 * B_HEADS) * scale, jnp.tile(b_kn, 2 * B_HEADS), ones(2 * B_HEADS * d),
    ]).astype(F32)[None, :]
    seg_id = jnp.arange(MXU_COLS) // d
    seg = jnp.where(seg_id[:, None] == seg_id[None, :], 1.0 / d, 0.0).astype(BF16)
    return w, gain, seg


def _trunk(x, p, *, tm, tile, ts, qb):
    qa, ka, va, qb_, kb, vb = _attn_in(x, p["mix_g"][0], p["attn_w"], p["attn_gain"], p["seg"], tm)
    ya = _win_attn(qa, ka, va, p["win_bias"], qb)
    yb = _diff_attn(qb_, kb, vb, p["diff_bias"][tile], p["lam"], p["subln"], p["lam_init"], tile)
    x = _mix_ffn(x, ya, yb, p["attn_wo_a"], p["attn_wo_b"],
                 p["ffn_g"][0], p["wg"][0], p["wu"][0], p["wd"][0], tm)

    gb, gx, glu = _conv_in(x, p["mix_g"][1], p["conv_w"], tm)
    yc, yu = _conv_mix(gb, gx, glu, p["scw"], p["dww"], p["dwb"], p["lng"], p["lnb"], ts)
    return _mix_ffn(x, yc, yu, p["conv_wo_a"], p["conv_wo_b"],
                    p["ffn_g"][1], p["wg"][1], p["wu"][1], p["wd"][1], tm)


def _prepare(rel_bias, mix_norm, ffn_norm, w_gate, w_up, w_down, attn_w_in, attn_w_out, a_q_norm, a_k_norm,
             a_sink, b_q_norm, b_k_norm, b_lambda, b_subln, conv_w_in, conv_w_out, short_conv_w, conf_dw_w,
             conf_dw_b, conf_ln_g, conf_ln_b, tiles):
    attn_w, attn_gain, seg = _attn_in_params(attn_w_in[0], a_q_norm[0], a_k_norm[0], b_q_norm[0], b_k_norm[0])
    half = A_Q_HEADS * HEAD_DIM
    return {
        "mix_g": [mix_norm[l][None, :].astype(F32) for l in range(2)],
        "ffn_g": [ffn_norm[l][None, :].astype(F32) for l in range(2)],
        "wg": [w_gate[l].astype(BF16) for l in range(2)],
        "wu": [w_up[l].astype(BF16) for l in range(2)],
        "wd": [w_down[l].astype(BF16) for l in range(2)],
        "attn_w": attn_w, "attn_gain": attn_gain, "seg": seg,
        "attn_wo_a": _by_win_head(attn_w_out[0][:half], 0).astype(BF16),
        "attn_wo_b": attn_w_out[0][half:].astype(BF16),
        "win_bias": _win_bias(rel_bias, a_sink[0], a_q_norm[0], a_k_norm[0]),
        "diff_bias": {t: _diff_bias(rel_bias, b_q_norm[0], b_k_norm[0], t) for t in tiles},
        "lam": b_lambda[0].astype(F32),
        "subln": b_subln[0].astype(F32)[:, None],
        "lam_init": 0.8 - 0.6 * math.exp(-0.3 * 0),
        "conv_w": conv_w_in[0].astype(BF16),
        "conv_wo_a": conv_w_out[0][:C_WIDTH].astype(BF16), "conv_wo_b": conv_w_out[0][C_WIDTH:].astype(BF16),
        "scw": jnp.broadcast_to(short_conv_w[0].astype(F32)[:, None, :], (SHORT_CONV, SUBLANES, C_WIDTH)),
        "dww": jnp.broadcast_to(conf_dw_w[0].astype(F32)[:, None, :], (CONF_CONV, SUBLANES, D_WIDTH)),
        "dwb": conf_dw_b[0][None, :].astype(F32),
        "lng": conf_ln_g[0][None, :].astype(F32), "lnb": conf_ln_b[0][None, :].astype(F32),
    }


def _tiling(s_len):
    return dict(tm=min(1024, s_len), tile=min(512, s_len), ts=min(512, s_len), qb=8)


def kernel(x_prompt, x_sample, rel_bias, mix_norm, ffn_norm, w_gate, w_up, w_down, attn_w_in, attn_w_out,
           a_q_norm, a_k_norm, a_sink, b_q_norm, b_k_norm, b_lambda, b_subln, conv_w_in, conv_w_out,
           short_conv_w, conf_dw_w, conf_dw_b, conf_ln_g, conf_ln_b):
    tp, tsm = _tiling(x_prompt.shape[1]), _tiling(x_sample.shape[1])
    p = _prepare(rel_bias, mix_norm, ffn_norm, w_gate, w_up, w_down, attn_w_in, attn_w_out, a_q_norm,
                 a_k_norm, a_sink, b_q_norm, b_k_norm, b_lambda, b_subln, conv_w_in, conv_w_out,
                 short_conv_w, conf_dw_w, conf_dw_b, conf_ln_g, conf_ln_b, {tp["tile"], tsm["tile"]})
    return (_trunk(x_prompt, p, **tp), _trunk(x_sample, p, **tsm))
```

```python
import functools
import math

import jax
import jax.numpy as jnp
from jax import lax
from jax.experimental import pallas as pl
from jax.experimental.pallas import tpu as pltpu

D_MODEL = 1024
HEAD_DIM = 64
A_Q_HEADS = 8
A_KV_HEADS = 2
WINDOW = 128
BLOCK = 128
B_HEADS = 4
NUM_BUCKETS = 32
MAX_DISTANCE = 128
C_WIDTH = 512
D_WIDTH = 512
SHORT_CONV = 3
CONF_CONV = 31
FFN_HIDDEN = 2816
EPS = 1e-6
NEG = -1e30

LANES = 128
MXU_COLS = 256
VMEM_LIMIT_BYTES = 56 * 1024 * 1024

BF16 = jnp.bfloat16
F32 = jnp.float32

QA_W = A_Q_HEADS * HEAD_DIM
PROJ0_W = QA_W + 2 * A_KV_HEADS * HEAD_DIM + 3 * B_HEADS * 2 * HEAD_DIM
NORM_CHUNKS0 = ("all", "all", "low", "all", "all", "all", "all", None, None)
WIN_HEAD_ORDER = (0, 4, 1, 5, 2, 6, 3, 7)

HALO = 16


def _cparams(sem):
    return pltpu.CompilerParams(dimension_semantics=sem, vmem_limit_bytes=VMEM_LIMIT_BYTES)


def _const_spec(shape):
    nd = len(shape)
    return pl.BlockSpec(shape, lambda *_: (0,) * nd)


def _rms(x, g):
    ms = jnp.mean(x * x, axis=-1, keepdims=True)
    return x * lax.rsqrt(ms + EPS) * g


def _attn_in_kernel(x_ref, g_ref, w_ref, gain_ref, seg_ref, qa_ref, ka_ref, va_ref, qb_ref, kb_ref, vb_ref):
    half = MXU_COLS // 2

    def store(c, y):
        if c < 2:
            qa_ref[:, c * MXU_COLS:(c + 1) * MXU_COLS] = y
        elif c == 2:
            ka_ref[...] = y[:, :half]
            va_ref[...] = y[:, half:]
        else:
            ref, first = ((qb_ref, 3), (kb_ref, 5), (vb_ref, 7))[(c - 3) // 2]
            ref[0, 2 * (c - first)] = y[:, :half]
            ref[0, 2 * (c - first) + 1] = y[:, half:]

    h = _rms(x_ref[...], g_ref[...]).astype(BF16)
    seg = seg_ref[...]
    low = lax.broadcasted_iota(jnp.int32, (1, MXU_COLS), 1) < MXU_COLS // 2
    chunk = lambda c: slice(c * MXU_COLS, (c + 1) * MXU_COLS)
    accs = [jnp.dot(h, w_ref[:, chunk(c)], preferred_element_type=F32) for c in range(len(NORM_CHUNKS0))]
    for c, normed in enumerate(NORM_CHUNKS0):
        acc = accs[c]
        if normed is not None:
            sq = acc * acc
            sq_hi = sq.astype(BF16)
            sq_lo = (sq - sq_hi.astype(F32)).astype(BF16)
            ms = (jnp.dot(sq_hi, seg, preferred_element_type=F32)
                  + jnp.dot(sq_lo, seg, preferred_element_type=F32))
            scale = lax.rsqrt(ms + EPS) * gain_ref[:, chunk(c)]
            acc = acc * (scale if normed == "all" else jnp.where(low, scale, 1.0))
        store(c, acc.astype(BF16))


def _attn_in(x, g, w, gain, seg, tm):
    bsz, s_len, _ = x.shape
    spt = s_len // tm
    flat = lambda width: pl.BlockSpec((1, tm, width), lambda i: (i // spt, i % spt, 0))
    heads = pl.BlockSpec((1, B_HEADS, tm, LANES), lambda i: (i // spt, 0, i % spt, 0))
    sds = lambda *shape: jax.ShapeDtypeStruct(shape, BF16)
    head_major = sds(bsz, B_HEADS, s_len, LANES)

    def body(x_ref, g_ref, w_ref, gain_ref, seg_ref, qa_ref, ka_ref, va_ref, qb_ref, kb_ref, vb_ref):
        _attn_in_kernel(x_ref.at[0], g_ref, w_ref, gain_ref, seg_ref,
                        qa_ref.at[0], ka_ref.at[0], va_ref.at[0], qb_ref, kb_ref, vb_ref)

    return pl.pallas_call(
        body,
        grid=(bsz * spt,),
        in_specs=[
            flat(D_MODEL),
            _const_spec((1, D_MODEL)),
            _const_spec((D_MODEL, PROJ0_W)),
            _const_spec((1, PROJ0_W)),
            _const_spec((MXU_COLS, MXU_COLS)),
        ],
        out_specs=[flat(QA_W), flat(LANES), flat(LANES), heads, heads, heads],
        out_shape=[sds(bsz, s_len, QA_W), sds(bsz, s_len, LANES), sds(bsz, s_len, LANES),
                   head_major, head_major, head_major],
        compiler_params=_cparams(("parallel",)),
        name="attn_in",
    )(x, g, w, gain, seg)


WIN_COLS = A_Q_HEADS * BLOCK


def _win_attn_kernel(q_ref, kp_ref, kc_ref, kn_ref, vp_ref, vc_ref, vn_ref, bias_ref, sink_ref, o_ref, *,
                     qb, bounded):
    step = pl.program_id(1)
    nb = pl.num_programs(1) * qb
    low = lax.broadcasted_iota(jnp.int32, (1, LANES), 1) < HEAD_DIM
    top = lax.broadcasted_iota(jnp.int32, (LANES, 1), 0) < HEAD_DIM
    sink = sink_ref[...]

    def piece(prev_ref, cur_ref, next_ref, i):
        if i == 0:
            return prev_ref[0]
        if i == qb + 1:
            return next_ref[0]
        return cur_ref[0, (i - 1) * BLOCK:i * BLOCK, :]

    def scores(b):
        kcat = jnp.concatenate([piece(kp_ref, kc_ref, kn_ref, b + i) for i in range(3)], axis=0)
        q = q_ref[0, b * BLOCK:(b + 1) * BLOCK, :]
        zero = jnp.zeros((BLOCK, LANES), BF16)
        halves = []
        for j in range(A_Q_HEADS // 2):
            slab = q[:, j * LANES:(j + 1) * LANES]
            halves += [jnp.where(low, slab, zero), jnp.where(low, zero, slab)]
        qcat = jnp.concatenate(halves, axis=0)
        return lax.dot_general(kcat, qcat, _NT, preferred_element_type=F32)

    def weights(b, s_t):
        n = step * qb + b
        edge = jnp.where(n == 0, 0, jnp.where(n == nb - 1, 2, 1))
        s_t = s_t + bias_ref[edge]
        if bounded:
            p = jnp.exp2(s_t)
            den = jnp.sum(p, axis=0, keepdims=True) + jnp.exp2(sink)
        else:
            m = jnp.maximum(jnp.max(s_t, axis=0, keepdims=True), sink)
            p = jnp.exp2(s_t - m)
            den = jnp.sum(p, axis=0, keepdims=True) + jnp.exp2(sink - m)
        return p.astype(BF16), den

    def values(b, p, den):
        vcat = jnp.concatenate([piece(vp_ref, vc_ref, vn_ref, b + i) for i in range(3)], axis=0)
        return lax.dot_general(vcat, p, _TN, preferred_element_type=F32) / den

    s_all = [scores(b) for b in range(qb)]
    w_all = [weights(b, s_all[b]) for b in range(qb)]
    o_all = [values(b, *w_all[b]) for b in range(qb)]
    for b in range(qb):
        for j in range(A_Q_HEADS // 2):
            c0 = 2 * j * BLOCK
            slab_t = jnp.where(top, o_all[b][:, c0:c0 + BLOCK], o_all[b][:, c0 + BLOCK:c0 + 2 * BLOCK])
            o_ref[0, b * BLOCK:(b + 1) * BLOCK, j * LANES:(j + 1) * LANES] = slab_t.T.astype(BF16)


def _win_attn(qa, ka, va, wb, qb):
    fast = lambda: _win_attn_call(qa, ka, va, wb["sink_shifted"], wb["bias_shifted"], qb, True)
    exact_max = lambda: _win_attn_call(qa, ka, va, wb["sink"], wb["bias"], qb, False)
    return lax.cond(wb["bounded_ok"], fast, exact_max)


def _win_attn_call(qa, ka, va, sink_row, bias, qb, bounded):
    bsz, s_len, _ = qa.shape
    nb = s_len // BLOCK
    assert nb % qb == 0 and nb >= 2
    kv_specs = [
        pl.BlockSpec((1, BLOCK, LANES), lambda b, t: (b, jnp.maximum(t * qb - 1, 0), 0)),
        pl.BlockSpec((1, qb * BLOCK, LANES), lambda b, t: (b, t, 0)),
        pl.BlockSpec((1, BLOCK, LANES), lambda b, t: (b, jnp.minimum((t + 1) * qb, nb - 1), 0)),
    ]
    return pl.pallas_call(
        functools.partial(_win_attn_kernel, qb=qb, bounded=bounded),
        grid=(bsz, nb // qb),
        in_specs=[pl.BlockSpec((1, qb * BLOCK, QA_W), lambda b, t: (b, t, 0))] + kv_specs + kv_specs
                 + [pl.BlockSpec(memory_space=pltpu.VMEM), pl.BlockSpec(memory_space=pltpu.VMEM)],
        out_specs=pl.BlockSpec((1, qb * BLOCK, QA_W), lambda b, t: (b, t, 0)),
        out_shape=jax.ShapeDtypeStruct((bsz, s_len, QA_W), BF16),
        compiler_params=_cparams(("parallel", "parallel")),
        name="win_attn",
    )(qa, ka, ka, ka, va, va, va, bias, sink_row)


_NT = (((1,), (1,)), ((), ()))
_TN = (((0,), (0,)), ((), ()))


def _split_maps(q, qz_ref):
    lane = lax.broadcasted_iota(jnp.int32, (1, LANES), 1)
    zero = jnp.zeros_like(q)
    qz_ref[0] = jnp.where(lane < HEAD_DIM, q, zero)
    qz_ref[1] = jnp.where(lane < HEAD_DIM, zero, q)


def _diff_out(acc0, l0, acc1, l1, lam_ref, g_ref, lam_init):
    lam = lam_ref[...]
    lam_full = (jnp.exp(jnp.sum(lam[0:1] * lam[1:2], axis=-1, keepdims=True))
                - jnp.exp(jnp.sum(lam[2:3] * lam[3:4], axis=-1, keepdims=True)) + lam_init)
    o = acc0 / l0 - lam_full * (acc1 / l1)
    ms = jnp.mean(o * o, axis=0, keepdims=True)
    y = o * lax.rsqrt(ms + EPS) * g_ref[...] * (1.0 - lam_init)
    return y.T.astype(BF16)


def _diff_attn_bounded_kernel(q_ref, k_ref, v_ref, bias_ref, lam_ref, g_ref, o_ref,
                              qz_ref, s_ref, l_ref, acc_ref, *, lam_init, tile, qtiles, ktiles):
    qstep = pl.program_id(2)
    ktile = ktiles * tile
    nk = k_ref.shape[1] // ktile
    first_map = lax.broadcasted_iota(jnp.int32, (LANES, 1), 0) < HEAD_DIM
    for qt in range(qtiles):
        q_t = q_ref[0, qt * tile:(qt + 1) * tile, :].astype(F32).T
        qz_ref[qt, 0] = jnp.where(first_map, q_t, 0.0).astype(BF16)
        qz_ref[qt, 1] = jnp.where(first_map, 0.0, q_t).astype(BF16)
    l_ref[...] = jnp.zeros(l_ref.shape, F32)
    acc_ref[...] = jnp.zeros(acc_ref.shape, F32)

    def produce(qt, ki, buf):
        k = k_ref[0, ki * ktile:(ki + 1) * ktile, :]
        for c in range(2):
            s_ref[buf, c] = jnp.dot(k, qz_ref[qt, c], preferred_element_type=F32)

    def consume(qt, ki, buf):
        v = v_ref[0, ki * ktile:(ki + 1) * ktile, :]
        qi = qstep * qtiles + qt
        for c in range(2):
            parts = []
            for sub in range(ktiles):
                bias = bias_ref[0, jnp.clip(ki * ktiles + sub - qi + 2, 0, 4)]
                p = jnp.exp2(s_ref[buf, c, sub * tile:(sub + 1) * tile, :] + bias)
                l_ref[qt, c] += jnp.sum(p.reshape(tile // 8, 8, tile), axis=0)
                parts.append(p.astype(BF16))
            p_all = parts[0] if len(parts) == 1 else jnp.concatenate(parts, axis=0)
            acc_ref[qt, c] += lax.dot_general(v, p_all, _TN, preferred_element_type=F32)

    def finish(qt):
        l0 = jnp.sum(l_ref[qt, 0], axis=0, keepdims=True)
        l1 = jnp.sum(l_ref[qt, 1], axis=0, keepdims=True)
        o_ref[0, qt * tile:(qt + 1) * tile, :] = _diff_out(acc_ref[qt, 0], l0, acc_ref[qt, 1], l1,
                                                           lam_ref, g_ref, lam_init)

    steps =[(qt, ki) for qt in range(qtiles) for ki in range(nk)]
    produce(*steps[0], 0)
    for i, (qt, ki) in enumerate(steps):
        if i + 1 < len(steps):
            produce(*steps[i + 1], (i + 1) % 2)
        consume(qt, ki, i % 2)
        if ki == nk - 1:
            finish(qt)


DIFF_TILES = 16
DIFF_KEY_TILES = 4


def _diff_attn_bounded(qb, kb, vb, bias_t, lam, g_col, lam_init, tile):
    bsz, _, s_len, _ = qb.shape
    tiles = s_len // tile
    assert tiles <= DIFF_TILES
    qtiles = min(DIFF_TILES // tiles, tiles)
    ktiles = max(1, min(DIFF_KEY_TILES, tiles // 2))
    assert tiles % qtiles == 0 and tiles % ktiles == 0
    kernel = functools.partial(_diff_attn_bounded_kernel, lam_init=lam_init, tile=tile, qtiles=qtiles,
                               ktiles=ktiles)

    def body(q_ref, k_ref, v_ref, bias_ref, lam_ref, g_ref, o_ref, *scratch):
        kernel(q_ref.at[0], k_ref.at[0], v_ref.at[0], bias_ref, lam_ref, g_ref, o_ref.at[0], *scratch)

    q_rows = qtiles * tile
    return pl.pallas_call(
        body,
        grid=(bsz, B_HEADS, s_len // q_rows),
        in_specs=[
            pl.BlockSpec((1, 1, q_rows, LANES), lambda b, h, qi: (b, h, qi, 0)),
            pl.BlockSpec((1, 1, s_len, LANES), lambda b, h, qi: (b, h, 0, 0)),
            pl.BlockSpec((1, 1, s_len, LANES), lambda b, h, qi: (b, h, 0, 0)),
            pl.BlockSpec((1, 5, tile, tile), lambda b, h, qi: (h, 0, 0, 0)),
            _const_spec((4, HEAD_DIM)),
            _const_spec((2 * HEAD_DIM, 1)),
        ],
        out_specs=pl.BlockSpec((1, 1, q_rows, LANES), lambda b, h, qi: (b, h, qi, 0)),
        out_shape=jax.ShapeDtypeStruct((bsz, B_HEADS, s_len, LANES), BF16),
        scratch_shapes=[
            pltpu.VMEM((qtiles, 2, LANES, tile), BF16),
            pltpu.VMEM((2, 2, ktiles * tile, tile), F32),
            pltpu.VMEM((qtiles, 2, 8, tile), F32),
            pltpu.VMEM((qtiles, 2, 2 * HEAD_DIM, tile), F32),
        ],
        compiler_params=_cparams(("parallel", "parallel", "parallel")),
        name="diff_attn",
    )(qb, kb, vb, bias_t, lam, g_col)


def _diff_attn_online_kernel(far_ref, q_ref, k_ref, v_ref, bias_ref, lam_ref, g_ref, o_ref,
                             qz_ref, m_ref, l_ref, acc_ref, *, lam_init):
    h = pl.program_id(1)
    qi = pl.program_id(2)
    ki = pl.program_id(3)
    nk = pl.num_programs(3)

    @pl.when(ki == 0)
    def _init():
        _split_maps(q_ref[0], qz_ref)
        m_ref[...] = jnp.full(m_ref.shape, NEG, F32)
        l_ref[...] = jnp.zeros(l_ref.shape, F32)
        acc_ref[...] = jnp.zeros(acc_ref.shape, F32)

    delta = ki - qi

    def step(c, bias_tile, bias_row):
        s_t = lax.dot_general(k_ref[0], qz_ref[c], _NT, preferred_element_type=F32)
        if bias_tile is not None:
            s_t = s_t + bias_tile
        m_old = m_ref[c]
        m_new = jnp.maximum(m_old, jnp.max(s_t, axis=0, keepdims=True) + bias_row)
        alpha = jnp.exp2(m_old - m_new)
        p = jnp.exp2(s_t - (m_new - bias_row))
        l_ref[c] = alpha * l_ref[c] + jnp.sum(p, axis=0, keepdims=True)
        pv = lax.dot_general(v_ref[0], p.astype(BF16), _TN, preferred_element_type=F32)
        acc_ref[c] = alpha * acc_ref[c] + pv
        m_ref[c] = m_new

    @pl.when(jnp.abs(delta) <= 1)
    def _near():
        bias_tile = bias_ref[0, delta + 1]
        for c in range(2):
            step(c, bias_tile, 0.0)

    @pl.when(jnp.abs(delta) > 1)
    def _far():
        side = jnp.where(delta < 0, far_ref[h, 0], far_ref[h, 1])
        for c in range(2):
            step(c, None, side)

    @pl.when(ki == nk - 1)
    def _finish():
        o_ref[0] = _diff_out(acc_ref[0], l_ref[0], acc_ref[1], l_ref[1], lam_ref, g_ref, lam_init)


def _diff_attn_online(qb, kb, vb, far, bias_t, lam, g_col, lam_init, tile):
    bsz, _, s_len, _ = qb.shape
    nt = s_len // tile
    kernel = functools.partial(_diff_attn_online_kernel, lam_init=lam_init)

    def body(far_ref, q_ref, k_ref, v_ref, bias_ref, lam_ref, g_ref, o_ref, *scratch):
        kernel(far_ref, q_ref.at[0], k_ref.at[0], v_ref.at[0], bias_ref, lam_ref, g_ref, o_ref.at[0], *scratch)

    return pl.pallas_call(
        body,
        grid=(bsz, B_HEADS, nt, nt),
        in_specs=[
            pl.BlockSpec(memory_space=pltpu.SMEM),
            pl.BlockSpec((1, 1, tile, LANES), lambda b, h, qi, ki: (b, h, qi, 0)),
            pl.BlockSpec((1, 1, tile, LANES), lambda b, h, qi, ki: (b, h, ki, 0)),
            pl.BlockSpec((1, 1, tile, LANES), lambda b, h, qi, ki: (b, h, ki, 0)),
            pl.BlockSpec((1, 3, tile, tile), lambda b, h, qi, ki: (h, 0, 0, 0)),
            _const_spec((4, HEAD_DIM)),
            _const_spec((2 * HEAD_DIM, 1)),
        ],
        out_specs=pl.BlockSpec((1, 1, tile, LANES), lambda b, h, qi, ki: (b, h, qi, 0)),
        out_shape=jax.ShapeDtypeStruct((bsz, B_HEADS, s_len, LANES), BF16),
        scratch_shapes=[
            pltpu.VMEM((2, tile, LANES), BF16),
            pltpu.VMEM((2, 1, tile), F32),
            pltpu.VMEM((2, 1, tile), F32),
            pltpu.VMEM((2, 2 * HEAD_DIM, tile), F32),
        ],
        compiler_params=_cparams(("parallel", "parallel", "parallel", "arbitrary")),
        name="diff_attn_online",
    )(far, qb, kb, vb, bias_t, lam, g_col)


def _diff_attn(qb, kb, vb, db, lam, g_col, lam_init, tile):
    bounded = lambda: _diff_attn_bounded(qb, kb, vb, db["shifted"], lam, g_col, lam_init, tile)
    online = lambda: _diff_attn_online(qb, kb, vb, db["far"], db["near"], lam, g_col, lam_init, tile)
    return lax.cond(db["bounded_ok"], bounded, online)


FFN_CHUNKS = ((0, 1536), (1536, FFN_HIDDEN))


def _mix_ffn_kernel(x_ref, a_ref, b_ref, wa_ref, wb_ref, g_ref, wg_ref, wu_ref, wd_ref, o_ref):
    if len(b_ref.shape) == 4:
        b = jnp.concatenate([b_ref[0, hd] for hd in range(b_ref.shape[1])], axis=-1)
    else:
        b = b_ref[0]
    mix = (jnp.dot(a_ref[0], wa_ref[...], preferred_element_type=F32)
           + jnp.dot(b, wb_ref[...], preferred_element_type=F32))
    x1 = x_ref[0] + mix
    h = _rms(x1, g_ref[...]).astype(BF16)
    gates = [jnp.dot(h, wg_ref[:, lo:hi], preferred_element_type=F32) for lo, hi in FFN_CHUNKS]
    ups = [jnp.dot(h, wu_ref[:, lo:hi], preferred_element_type=F32) for lo, hi in FFN_CHUNKS]
    down = None
    for (lo, hi), gate, up in zip(FFN_CHUNKS, gates, ups):
        act = (gate * jax.nn.sigmoid(gate) * up).astype(BF16)
        part = jnp.dot(act, wd_ref[lo:hi, :], preferred_element_type=F32)
        down = part if down is None else down + part
    o_ref[0] = x1 + down


def _mix_ffn(x, a, b, wa, wb, g, wg, wu, wd, tm):
    bsz, s_len, _ = x.shape
    spt = s_len // tm
    row_spec = lambda w: pl.BlockSpec((1, tm, w), lambda i: (i // spt, i % spt, 0))
    if b.ndim == 4:
        b_spec = pl.BlockSpec((1, b.shape[1], tm, LANES), lambda i: (i // spt, 0, i % spt, 0))
    else:
        b_spec = row_spec(b.shape[-1])
    resident = pl.BlockSpec(memory_space=pltpu.VMEM)
    return pl.pallas_call(
        _mix_ffn_kernel,
        grid=(bsz * spt,),
        in_specs=[row_spec(D_MODEL), row_spec(a.shape[-1]), b_spec] + [resident] * 6,
        out_specs=row_spec(D_MODEL),
        out_shape=jax.ShapeDtypeStruct(x.shape, F32),
        compiler_params=_cparams(("parallel",)),
        name="mix_ffn",
    )(x, a, b, wa, wb, g, wg, wu, wd)


def _conv_in_kernel(x_ref, g_ref, w_ref, gb_ref, gx_ref, glu_ref):
    h = _rms(x_ref[0], g_ref[...]).astype(BF16)
    c = C_WIDTH
    gb, gc, xc, a, gate = [jnp.dot(h, w_ref[:, i * c:(i + 1) * c], preferred_element_type=F32) for i in range(5)]
    gb_ref[0] = gb.astype(BF16)
    gx_ref[0] = (gc * xc).astype(BF16)
    glu_ref[0] = (a * jax.nn.sigmoid(gate)).astype(BF16)


def _conv_in(x, g, w, tm):
    bsz, s_len, _ = x.shape
    spt = s_len // tm
    row_spec = lambda width: pl.BlockSpec((1, tm, width), lambda i: (i // spt, i % spt, 0))
    out = jax.ShapeDtypeStruct((bsz, s_len, C_WIDTH), BF16)
    return pl.pallas_call(
        _conv_in_kernel,
        grid=(bsz * spt,),
        in_specs=[row_spec(D_MODEL), _const_spec((1, D_MODEL)), _const_spec((D_MODEL, w.shape[1]))],
        out_specs=[row_spec(C_WIDTH)] * 3,
        out_shape=[out] * 3,
        compiler_params=_cparams(("parallel",)),
        name="conv_in",
    )(x, g, w)


CONV_CHUNK = 32
SUBLANES = 8


SHIFT_ROWS = 256
SHIFT_K = 384


def _conv_mix_kernel(gb_ref, gx_ref, gxp_ref, gxn_ref, u_ref, up_ref, un_ref, shift_ref,
                     scw_ref, dww_ref, dwb_ref, lng_ref, lnb_ref, yc_ref, yu_ref,
                     xb_ref, xs_ref, u_ref_f32, *, ts):
    t = pl.program_id(1)
    nt = pl.num_programs(1)
    padded = ts + 2 * HALO

    def taps(width):
        return [(j,) + divmod(HALO - width // 2 + j, SUBLANES)[::-1] for j in range(width)]

    def fill(cur, prev, nxt, width, xb, xs):
        zero = jnp.zeros((HALO, C_WIDTH), BF16)
        xb[0:HALO, :] = jnp.where(t > 0, prev[0], zero)
        xb[HALO:HALO + ts, :] = cur[0]
        xb[HALO + ts:padded, :] = jnp.where(t < nt - 1, nxt[0], zero)
        xb[padded:, :] = jnp.zeros((xb.shape[0] - padded, C_WIDTH), BF16)
        xs[0] = xb[0:padded, :].astype(F32)
        copies = sorted({r for _, r, _ in taps(width)} - {0})
        for o0 in range(0, ts, SHIFT_ROWS):
            for r in copies:
                xs[r, o0:o0 + SHIFT_ROWS, :] = jnp.dot(shift_ref[r - 1], xb[o0:o0 + SHIFT_K, :],
                                                       preferred_element_type=F32)
        for r in copies:
            xs[r, ts:padded, :] = jnp.dot(shift_ref[r - 1, 0:2 * HALO, 0:LANES], xb[ts:ts + LANES, :],
                                          preferred_element_type=F32)

    def conv(w_ref, width, s0, xs):
        acc = None
        groups = CONV_CHUNK // SUBLANES
        for r in sorted({r for _, r, _ in taps(width)}):
            mine = [(j, q) for j, r_j, q in taps(width) if r_j == r]
            q_lo, q_hi = min(q for _, q in mine), max(q for _, q in mine)
            span = xs[r, pl.ds(s0 + SUBLANES * q_lo, CONV_CHUNK + SUBLANES * (q_hi - q_lo)), :]
            span = span.reshape(groups + q_hi - q_lo, SUBLANES, -1)
            for j, q in mine:
                term = span[q - q_lo:q - q_lo + groups] * w_ref[j]
                acc = term if acc is None else acc + term
        return acc.reshape(CONV_CHUNK, -1)

    def chunks(body):
        for c in range(ts // CONV_CHUNK):
            body(c * CONV_CHUNK)

    fill(gx_ref, gxp_ref, gxn_ref, SHORT_CONV, xb_ref, xs_ref)

    def short(s0):
        out = pl.ds(s0, CONV_CHUNK)
        yc_ref[0, out, :] = (gb_ref[0, out, :].astype(F32) * conv(scw_ref, SHORT_CONV, s0, xs_ref)).astype(BF16)
    chunks(short)

    fill(u_ref, up_ref, un_ref, CONF_CONV, xb_ref, xs_ref)

    def conf(s0):
        u_ref_f32[pl.ds(s0, CONV_CHUNK), :] = conv(dww_ref, CONF_CONV, s0, xs_ref) + dwb_ref[...]
    chunks(conf)

    u = u_ref_f32[...]
    mu = jnp.mean(u, axis=-1, keepdims=True)
    uc = u - mu
    var = jnp.mean(uc * uc, axis=-1, keepdims=True)
    y = uc * lax.rsqrt(var + EPS) * lng_ref[...] + lnb_ref[...]
    yu_ref[0] = (y * jax.nn.sigmoid(y)).astype(BF16)


def _conv_mix(gb, gx, glu, scw, dww, dwb, lng, lnb, ts):
    bsz, s_len, _ = gb.shape
    nt = s_len // ts
    r = ts // HALO
    nh = s_len // HALO
    assert C_WIDTH == D_WIDTH
    cur = pl.BlockSpec((1, ts, C_WIDTH), lambda b, t: (b, t, 0))
    prev = pl.BlockSpec((1, HALO, C_WIDTH), lambda b, t: (b, jnp.maximum(t * r - 1, 0), 0))
    nxt = pl.BlockSpec((1, HALO, C_WIDTH), lambda b, t: (b, jnp.minimum((t + 1) * r, nh - 1), 0))
    out_spec = pl.BlockSpec((1, ts, C_WIDTH), lambda b, t: (b, t, 0))
    assert ts % SHIFT_ROWS == 0
    i, j = jnp.arange(SHIFT_ROWS)[:, None], jnp.arange(SHIFT_K)[None, :]
    shift = jnp.stack([(j == i + r) for r in range(1, SUBLANES)]).astype(BF16)
    return pl.pallas_call(
        functools.partial(_conv_mix_kernel, ts=ts),
        grid=(bsz, nt),
        in_specs=[cur, cur, prev, nxt, cur, prev, nxt, _const_spec(shift.shape),
                  _const_spec((SHORT_CONV, SUBLANES, C_WIDTH)), _const_spec((CONF_CONV, SUBLANES, D_WIDTH)),
                  _const_spec((1, D_WIDTH)), _const_spec((1, D_WIDTH)), _const_spec((1, D_WIDTH))],
        out_specs=[out_spec, out_spec],
        out_shape=[jax.ShapeDtypeStruct((bsz, s_len, C_WIDTH), BF16),
                   jax.ShapeDtypeStruct((bsz, s_len, D_WIDTH), BF16)],
        scratch_shapes=[pltpu.VMEM((ts + LANES, C_WIDTH), BF16),
                        pltpu.VMEM((SUBLANES, ts + 2 * HALO, C_WIDTH), F32), pltpu.VMEM((ts, D_WIDTH), F32)],
        compiler_params=_cparams(("parallel", "parallel")),
        name="conv_mix",
    )(gb, gx, gx, gx, glu, glu, glu, shift, scw, dww, dwb, lng, lnb)


def _rel_bucket(rel):
    half = NUM_BUCKETS // 2
    max_exact = half // 2
    n = jnp.abs(rel)
    large = max_exact + (jnp.log(jnp.maximum(n, 1).astype(F32) / max_exact)
                         / math.log(MAX_DISTANCE / max_exact) * (half - max_exact)).astype(jnp.int32)
    large = jnp.minimum(large, half - 1)
    return jnp.where(rel > 0, half, 0) + jnp.where(n < max_exact, n, large)


def _toeplitz(u, rows, cols):
    length = u.shape[-1]
    flat = jnp.tile(u, (1,) * (u.ndim - 1) + (rows,))[..., :rows * (length - 1)]
    return flat.reshape(u.shape[:-1] + (rows, length - 1))[..., :cols]


LOG2E = math.log2(math.e)


def _win_bias(rel_bias, a_sink, a_qn, a_kn):
    order = jnp.array(WIN_HEAD_ORDER)
    table = rel_bias[:, :A_Q_HEADS].astype(F32) * LOG2E
    length = 4 * BLOCK
    n = jnp.arange(length)
    rel = jnp.where(n < BLOCK, -n, length - n) - BLOCK
    vec = jnp.where((jnp.abs(rel) <= WINDOW)[None], table[_rel_bucket(rel)].T, NEG)
    base = _toeplitz(vec, 3 * BLOCK, BLOCK)[order]
    mid = base.transpose(1, 0, 2).reshape(3 * BLOCK, A_Q_HEADS * BLOCK)
    key_block = (jnp.arange(3 * BLOCK) // BLOCK)[:, None]
    bias = jnp.stack([jnp.where(key_block == 0, NEG, mid), mid, jnp.where(key_block == 2, NEG, mid)], axis=0)
    sink = a_sink.astype(F32) * LOG2E
    s_max = 1.02 * LOG2E * math.sqrt(HEAD_DIM) * jnp.max(jnp.abs(a_qn * a_kn))
    bound = s_max + jnp.maximum(jnp.max(table, axis=0), sink)
    lowest = jnp.minimum(-s_max + jnp.min(table, axis=0), sink) - bound
    columns = lambda per_head: jnp.repeat(per_head[order], BLOCK)[None, :]
    return {"bias": bias, "sink": columns(sink),
            "bias_shifted": bias - columns(bound)[None], "sink_shifted": columns(sink - bound),
            "bounded_ok": jnp.all(lowest > MIN_EXP2_ARG)}
MIN_EXP2_ARG = -120.0


def _diff_bias(rel_bias, b_qn, b_kn, tile):
    table = rel_bias[:, A_Q_HEADS:].astype(F32) * LOG2E
    length = 2 * tile
    n = jnp.arange(length)
    k_minus_q = jnp.where(n < tile, -n, length - n)
    rel = jnp.clip(k_minus_q[None, :] + jnp.array([-tile, 0, tile])[:, None], 1 - length, length - 1)
    near = _toeplitz(table[_rel_bucket(rel)].transpose(2, 0, 1), tile, tile)
    far = table[_rel_bucket(jnp.array([-length, length]))].T
    s_max = 1.02 * LOG2E * math.sqrt(HEAD_DIM) * jnp.max(jnp.abs(b_qn * b_kn))
    b_max, b_min = jnp.max(table, axis=0), jnp.min(table, axis=0)
    bound = s_max + b_max
    const = lambda col: jnp.broadcast_to((far[:, col] - bound)[:, None, None, None], (B_HEADS, 1, tile, tile))
    shifted = jnp.concatenate([const(0), near - bound[:, None, None, None], const(1)], axis=1)
    bounded_ok = jnp.all(-2.0 * s_max - (b_max - b_min) > MIN_EXP2_ARG)
    return {"near": near, "far": far, "shifted": shifted, "bounded_ok": bounded_ok}


def _by_win_head(m, axis):
    take = lambda h: lax.slice_in_dim(m, h * HEAD_DIM, (h + 1) * HEAD_DIM, axis=axis)
    return jnp.concatenate([take(h) for h in WIN_HEAD_ORDER], axis=axis)


def _attn_in_params(w_in, a_qn, a_kn, b_qn, b_kn):
    d = HEAD_DIM
    k0 = A_Q_HEADS * d
    w = jnp.concatenate([_by_win_head(w_in[:, :k0], 1), w_in[:, k0:]], axis=1).astype(BF16)
    scale = HEAD_DIM ** -0.5 * LOG2E
    ones = lambda n: jnp.ones((n,), F32)
    gain = jnp.concatenate([
        jnp.tile(a_qn, A_Q_HEADS) * scale, jnp.tile(a_kn, A_KV_HEADS), ones(A_KV_HEADS * d),
        jnp.tile(b_qn, 2 * B_HEADS) * scale, jnp.tile(b_kn, 2 * B_HEADS), ones(2 * B_HEADS * d),
    ]).astype(F32)[None, :]
    seg_id = jnp.arange(MXU_COLS) // d
    seg = jnp.where(seg_id[:, None] == seg_id[None, :], 1.0 / d, 0.0).astype(BF16)
    return w, gain, seg


def _trunk(x, p, *, tm, tile, ts, qb):
    qa, ka, va, qb_, kb, vb = _attn_in(x, p["mix_g"][0], p["attn_w"], p["attn_gain"], p["seg"], tm)
    ya = _win_attn(qa, ka, va, p["win_bias"], qb)
    yb = _diff_attn(qb_, kb, vb, p["diff_bias"][tile], p["lam"], p["subln"], p["lam_init"], tile)
    x = _mix_ffn(x, ya, yb, p["attn_wo_a"], p["attn_wo_b"],
                 p["ffn_g"][0], p["wg"][0], p["wu"][0], p["wd"][0], tm)

    gb, gx, glu = _conv_in(x, p["mix_g"][1], p["conv_w"], tm)
    yc, yu = _conv_mix(gb, gx, glu, p["scw"], p["dww"], p["dwb"], p["lng"], p["lnb"], ts)
    return _mix_ffn(x, yc, yu, p["conv_wo_a"], p["conv_wo_b"],
                    p["ffn_g"][1], p["wg"][1], p["wu"][1], p["wd"][1], tm)


def _prepare(rel_bias, mix_norm, ffn_norm, w_gate, w_up, w_down, attn_w_in, attn_w_out, a_q_norm, a_k_norm,
             a_sink, b_q_norm, b_k_norm, b_lambda, b_subln, conv_w_in, conv_w_out, short_conv_w, conf_dw_w,
             conf_dw_b, conf_ln_g, conf_ln_b, tiles):
    attn_w, attn_gain, seg = _attn_in_params(attn_w_in[0], a_q_norm[0], a_k_norm[0], b_q_norm[0], b_k_norm[0])
    half = A_Q_HEADS * HEAD_DIM
    return {
        "mix_g": [mix_norm[l][None, :].astype(F32) for l in range(2)],
        "ffn_g": [ffn_norm[l][None, :].astype(F32) for l in range(2)],
        "wg": [w_gate[l].astype(BF16) for l in range(2)],
        "wu": [w_up[l].astype(BF16) for l in range(2)],
        "wd": [w_down[l].astype(BF16) for l in range(2)],
        "attn_w": attn_w, "attn_gain": attn_gain, "seg": seg,
        "attn_wo_a": _by_win_head(attn_w_out[0][:half], 0).astype(BF16),
        "attn_wo_b": attn_w_out[0][half:].astype(BF16),
        "win_bias": _win_bias(rel_bias, a_sink[0], a_q_norm[0], a_k_norm[0]),
        "diff_bias": {t: _diff_bias(rel_bias, b_q_norm[0], b_k_norm[0], t) for t in tiles},
        "lam": b_lambda[0].astype(F32),
        "subln": b_subln[0].astype(F32)[:, None],
        "lam_init": 0.8 - 0.6 * math.exp(-0.3 * 0),
        "conv_w": conv_w_in[0].astype(BF16),
        "conv_wo_a": conv_w_out[0][:C_WIDTH].astype(BF16), "conv_wo_b": conv_w_out[0][C_WIDTH:].astype(BF16),
        "scw": jnp.broadcast_to(short_conv_w[0].astype(F32)[:, None, :], (SHORT_CONV, SUBLANES, C_WIDTH)),
        "dww": jnp.broadcast_to(conf_dw_w[0].astype(F32)[:, None, :], (CONF_CONV, SUBLANES, D_WIDTH)),
        "dwb": conf_dw_b[0][None, :].astype(F32),
        "lng": conf_ln_g[0][None, :].astype(F32), "lnb": conf_ln_b[0][None, :].astype(F32),
    }


def _tiling(s_len):
    return dict(tm=min(1024, s_len), tile=min(512, s_len), ts=min(512, s_len), qb=8)


def kernel(x_prompt, x_sample, rel_bias, mix_norm, ffn_norm, w_gate, w_up, w_down, attn_w_in, attn_w_out,
           a_q_norm, a_k_norm, a_sink, b_q_norm, b_k_norm, b_lambda, b_subln, conv_w_in, conv_w_out,
           short_conv_w, conf_dw_w, conf_dw_b, conf_ln_g, conf_ln_b):
    tp, tsm = _tiling(x_prompt.shape[1]), _tiling(x_sample.shape[1])
    p = _prepare(rel_bias, mix_norm, ffn_norm, w_gate, w_up, w_down, attn_w_in, attn_w_out, a_q_norm,
                 a_k_norm, a_sink, b_q_norm, b_k_norm, b_lambda, b_subln, conv_w_in, conv_w_out,
                 short_conv_w, conf_dw_w, conf_dw_b, conf_ln_g, conf_ln_b, {tp["tile"], tsm["tile"]})
    return (_trunk(x_prompt, p, **tp), _trunk(x_sample, p, **tsm))
```

```python
import functools
import math

import jax
import jax.numpy as jnp
from jax import lax
from jax.experimental import pallas as pl
from jax.experimental.pallas import tpu as pltpu

D_MODEL = 1024
HEAD_DIM = 64
A_Q_HEADS = 8
A_KV_HEADS = 2
WINDOW = 128
BLOCK = 128
B_HEADS = 4
NUM_BUCKETS = 32
MAX_DISTANCE = 128
C_WIDTH = 512
D_WIDTH = 512
SHORT_CONV = 3
CONF_CONV = 31
FFN_HIDDEN = 2816
EPS = 1e-6
NEG = -1e30

LANES = 128
MXU_COLS = 256
VMEM_LIMIT_BYTES = 56 * 1024 * 1024

BF16 = jnp.bfloat16
F32 = jnp.float32

QA_W = A_Q_HEADS * HEAD_DIM
PROJ0_W = QA_W + 2 * A_KV_HEADS * HEAD_DIM + 3 * B_HEADS * 2 * HEAD_DIM
NORM_CHUNKS0 = ("all", "all", "low", "all", "all", "all", "all", None, None)
WIN_HEAD_ORDER = (0, 4, 1, 5, 2, 6, 3, 7)

HALO = 16


def _cparams(sem):
    return pltpu.CompilerParams(dimension_semantics=sem, vmem_limit_bytes=VMEM_LIMIT_BYTES)


def _const_spec(shape):
    nd = len(shape)
    return pl.BlockSpec(shape, lambda *_: (0,) * nd)


def _rms(x, g):
    ms = jnp.mean(x * x, axis=-1, keepdims=True)
    return x * lax.rsqrt(ms + EPS) * g


def _attn_in_kernel(x_ref, g_ref, w_ref, gain_ref, seg_ref, qa_ref, ka_ref, va_ref, qb_ref, kb_ref, vb_ref):
    half = MXU_COLS // 2

    def store(c, y):
        if c < 2:
            qa_ref[:, c * MXU_COLS:(c + 1) * MXU_COLS] = y
        elif c == 2:
            ka_ref[...] = y[:, :half]
            va_ref[...] = y[:, half:]
        else:
            ref, first = ((qb_ref, 3), (kb_ref, 5), (vb_ref, 7))[(c - 3) // 2]
            ref[0, 2 * (c - first)] = y[:, :half]
            ref[0, 2 * (c - first) + 1] = y[:, half:]

    h = _rms(x_ref[...], g_ref[...]).astype(BF16)
    seg = seg_ref[...]
    low = lax.broadcasted_iota(jnp.int32, (1, MXU_COLS), 1) < MXU_COLS // 2
    chunk = lambda c: slice(c * MXU_COLS, (c + 1) * MXU_COLS)
    accs = [jnp.dot(h, w_ref[:, chunk(c)], preferred_element_type=F32) for c in range(len(NORM_CHUNKS0))]
    for c, normed in enumerate(NORM_CHUNKS0):
        acc = accs[c]
        if normed is not None:
            ms = jnp.dot((acc * acc).astype(BF16), seg, preferred_element_type=F32)
            scale = lax.rsqrt(ms + EPS) * gain_ref[:, chunk(c)]
            acc = acc * (scale if normed == "all" else jnp.where(low, scale, 1.0))
        store(c, acc.astype(BF16))


def _attn_in(x, g, w, gain, seg, tm):
    bsz, s_len, _ = x.shape
    spt = s_len // tm
    flat = lambda width: pl.BlockSpec((1, tm, width), lambda i: (i // spt, i % spt, 0))
    heads = pl.BlockSpec((1, B_HEADS, tm, LANES), lambda i: (i // spt, 0, i % spt, 0))
    sds = lambda *shape: jax.ShapeDtypeStruct(shape, BF16)
    head_major = sds(bsz, B_HEADS, s_len, LANES)

    def body(x_ref, g_ref, w_ref, gain_ref, seg_ref, qa_ref, ka_ref, va_ref, qb_ref, kb_ref, vb_ref):
        _attn_in_kernel(x_ref.at[0], g_ref, w_ref, gain_ref, seg_ref,
                        qa_ref.at[0], ka_ref.at[0], va_ref.at[0], qb_ref, kb_ref, vb_ref)

    return pl.pallas_call(
        body,
        grid=(bsz * spt,),
        in_specs=[
            flat(D_MODEL),
            _const_spec((1, D_MODEL)),
            _const_spec((D_MODEL, PROJ0_W)),
            _const_spec((1, PROJ0_W)),
            _const_spec((MXU_COLS, MXU_COLS)),
        ],
        out_specs=[flat(QA_W), flat(LANES), flat(LANES), heads, heads, heads],
        out_shape=[sds(bsz, s_len, QA_W), sds(bsz, s_len, LANES), sds(bsz, s_len, LANES),
                   head_major, head_major, head_major],
        compiler_params=_cparams(("parallel",)),
        name="attn_in",
    )(x, g, w, gain, seg)


WIN_COLS = A_Q_HEADS * BLOCK


def _win_attn_kernel(q_ref, kp_ref, kc_ref, kn_ref, vp_ref, vc_ref, vn_ref, bias_ref, sink_ref, o_ref, *,
                     qb, bounded):
    step = pl.program_id(1)
    nb = pl.num_programs(1) * qb
    low = lax.broadcasted_iota(jnp.int32, (1, LANES), 1) < HEAD_DIM
    top = lax.broadcasted_iota(jnp.int32, (LANES, 1), 0) < HEAD_DIM
    sink = sink_ref[...]

    def piece(prev_ref, cur_ref, next_ref, i):
        if i == 0:
            return prev_ref[0]
        if i == qb + 1:
            return next_ref[0]
        return cur_ref[0, (i - 1) * BLOCK:i * BLOCK, :]

    def scores(b):
        kcat = jnp.concatenate([piece(kp_ref, kc_ref, kn_ref, b + i) for i in range(3)], axis=0)
        q = q_ref[0, b * BLOCK:(b + 1) * BLOCK, :]
        zero = jnp.zeros((BLOCK, LANES), BF16)
        halves = []
        for j in range(A_Q_HEADS // 2):
            slab = q[:, j * LANES:(j + 1) * LANES]
            halves += [jnp.where(low, slab, zero), jnp.where(low, zero, slab)]
        qcat = jnp.concatenate(halves, axis=0)
        return lax.dot_general(kcat, qcat, _NT, preferred_element_type=F32)

    def weights(b, s_t):
        n = step * qb + b
        edge = jnp.where(n == 0, 0, jnp.where(n == nb - 1, 2, 1))
        s_t = s_t + bias_ref[edge]
        if bounded:
            p = jnp.exp2(s_t)
            den = jnp.sum(p, axis=0, keepdims=True) + jnp.exp2(sink)
        else:
            m = jnp.maximum(jnp.max(s_t, axis=0, keepdims=True), sink)
            p = jnp.exp2(s_t - m)
            den = jnp.sum(p, axis=0, keepdims=True) + jnp.exp2(sink - m)
        return p.astype(BF16), den

    def values(b, p, den):
        vcat = jnp.concatenate([piece(vp_ref, vc_ref, vn_ref, b + i) for i in range(3)], axis=0)
        return lax.dot_general(vcat, p, _TN, preferred_element_type=F32) / den

    s_all = [scores(b) for b in range(qb)]
    w_all = [weights(b, s_all[b]) for b in range(qb)]
    o_all = [values(b, *w_all[b]) for b in range(qb)]
    for b in range(qb):
        for j in range(A_Q_HEADS // 2):
            c0 = 2 * j * BLOCK
            slab_t = jnp.where(top, o_all[b][:, c0:c0 + BLOCK], o_all[b][:, c0 + BLOCK:c0 + 2 * BLOCK])
            o_ref[0, b * BLOCK:(b + 1) * BLOCK, j * LANES:(j + 1) * LANES] = slab_t.T.astype(BF16)


def _win_attn(qa, ka, va, wb, qb):
    fast = lambda: _win_attn_call(qa, ka, va, wb["sink_shifted"], wb["bias_shifted"], qb, True)
    exact_max = lambda: _win_attn_call(qa, ka, va, wb["sink"], wb["bias"], qb, False)
    return lax.cond(wb["bounded_ok"], fast, exact_max)


def _win_attn_call(qa, ka, va, sink_row, bias, qb, bounded):
    bsz, s_len, _ = qa.shape
    nb = s_len // BLOCK
    assert nb % qb == 0 and nb >= 2
    kv_specs = [
        pl.BlockSpec((1, BLOCK, LANES), lambda b, t: (b, jnp.maximum(t * qb - 1, 0), 0)),
        pl.BlockSpec((1, qb * BLOCK, LANES), lambda b, t: (b, t, 0)),
        pl.BlockSpec((1, BLOCK, LANES), lambda b, t: (b, jnp.minimum((t + 1) * qb, nb - 1), 0)),
    ]
    return pl.pallas_call(
        functools.partial(_win_attn_kernel, qb=qb, bounded=bounded),
        grid=(bsz, nb // qb),
        in_specs=[pl.BlockSpec((1, qb * BLOCK, QA_W), lambda b, t: (b, t, 0))] + kv_specs + kv_specs
                 + [pl.BlockSpec(memory_space=pltpu.VMEM), pl.BlockSpec(memory_space=pltpu.VMEM)],
        out_specs=pl.BlockSpec((1, qb * BLOCK, QA_W), lambda b, t: (b, t, 0)),
        out_shape=jax.ShapeDtypeStruct((bsz, s_len, QA_W), BF16),
        compiler_params=_cparams(("parallel", "parallel")),
        name="win_attn",
    )(qa, ka, ka, ka, va, va, va, bias, sink_row)


_NT = (((1,), (1,)), ((), ()))
_TN = (((0,), (0,)), ((), ()))


def _split_maps(q, qz_ref):
    lane = lax.broadcasted_iota(jnp.int32, (1, LANES), 1)
    zero = jnp.zeros_like(q)
    qz_ref[0] = jnp.where(lane < HEAD_DIM, q, zero)
    qz_ref[1] = jnp.where(lane < HEAD_DIM, zero, q)


def _diff_out(acc0, l0, acc1, l1, lam_ref, g_ref, lam_init):
    lam = lam_ref[...]
    lam_full = (jnp.exp(jnp.sum(lam[0:1] * lam[1:2], axis=-1, keepdims=True))
                - jnp.exp(jnp.sum(lam[2:3] * lam[3:4], axis=-1, keepdims=True)) + lam_init)
    o = acc0 / l0 - lam_full * (acc1 / l1)
    ms = jnp.mean(o * o, axis=0, keepdims=True)
    y = o * lax.rsqrt(ms + EPS) * g_ref[...] * (1.0 - lam_init)
    return y.T.astype(BF16)


def _diff_attn_bounded_kernel(q_ref, k_ref, v_ref, bias_ref, lam_ref, g_ref, o_ref,
                              qz_ref, s_ref, l_ref, acc_ref, *, lam_init, tile, qtiles, ktiles):
    qstep = pl.program_id(2)
    ktile = ktiles * tile
    nk = k_ref.shape[1] // ktile
    first_map = lax.broadcasted_iota(jnp.int32, (LANES, 1), 0) < HEAD_DIM
    for qt in range(qtiles):
        q_t = q_ref[0, qt * tile:(qt + 1) * tile, :].astype(F32).T
        qz_ref[qt, 0] = jnp.where(first_map, q_t, 0.0).astype(BF16)
        qz_ref[qt, 1] = jnp.where(first_map, 0.0, q_t).astype(BF16)
    l_ref[...] = jnp.zeros(l_ref.shape, F32)
    acc_ref[...] = jnp.zeros(acc_ref.shape, F32)

    def produce(qt, ki, buf):
        k = k_ref[0, ki * ktile:(ki + 1) * ktile, :]
        for c in range(2):
            s_ref[buf, c] = jnp.dot(k, qz_ref[qt, c], preferred_element_type=F32)

    def consume(qt, ki, buf):
        v = v_ref[0, ki * ktile:(ki + 1) * ktile, :]
        qi = qstep * qtiles + qt
        for c in range(2):
            parts = []
            for sub in range(ktiles):
                bias = bias_ref[0, jnp.clip(ki * ktiles + sub - qi + 2, 0, 4)]
                p = jnp.exp2(s_ref[buf, c, sub * tile:(sub + 1) * tile, :] + bias)
                l_ref[qt, c] += jnp.sum(p.reshape(tile // 8, 8, tile), axis=0)
                parts.append(p.astype(BF16))
            p_all = parts[0] if len(parts) == 1 else jnp.concatenate(parts, axis=0)
            acc_ref[qt, c] += lax.dot_general(v, p_all, _TN, preferred_element_type=F32)

    def finish(qt):
        l0 = jnp.sum(l_ref[qt, 0], axis=0, keepdims=True)
        l1 = jnp.sum(l_ref[qt, 1], axis=0, keepdims=True)
        o_ref[0, qt * tile:(qt + 1) * tile, :] = _diff_out(acc_ref[qt, 0], l0, acc_ref[qt, 1], l1,
                                                           lam_ref, g_ref, lam_init)

    steps =[(qt, ki) for qt in range(qtiles) for ki in range(nk)]
    produce(*steps[0], 0)
    for i, (qt, ki) in enumerate(steps):
        if i + 1 < len(steps):
            produce(*steps[i + 1], (i + 1) % 2)
        consume(qt, ki, i % 2)
        if ki == nk - 1:
            finish(qt)


DIFF_TILES = 16
DIFF_KEY_TILES = 4


def _diff_attn_bounded(qb, kb, vb, bias_t, lam, g_col, lam_init, tile):
    bsz, _, s_len, _ = qb.shape
    tiles = s_len // tile
    assert tiles <= DIFF_TILES
    qtiles = min(DIFF_TILES // tiles, tiles)
    ktiles = max(1, min(DIFF_KEY_TILES, tiles // 2))
    assert tiles % qtiles == 0 and tiles % ktiles == 0
    kernel = functools.partial(_diff_attn_bounded_kernel, lam_init=lam_init, tile=tile, qtiles=qtiles,
                               ktiles=ktiles)

    def body(q_ref, k_ref, v_ref, bias_ref, lam_ref, g_ref, o_ref, *scratch):
        kernel(q_ref.at[0], k_ref.at[0], v_ref.at[0], bias_ref, lam_ref, g_ref, o_ref.at[0], *scratch)

    q_rows = qtiles * tile
    return pl.pallas_call(
        body,
        grid=(bsz, B_HEADS, s_len // q_rows),
        in_specs=[
            pl.BlockSpec((1, 1, q_rows, LANES), lambda b, h, qi: (b, h, qi, 0)),
            pl.BlockSpec((1, 1, s_len, LANES), lambda b, h, qi: (b, h, 0, 0)),
            pl.BlockSpec((1, 1, s_len, LANES), lambda b, h, qi: (b, h, 0, 0)),
            pl.BlockSpec((1, 5, tile, tile), lambda b, h, qi: (h, 0, 0, 0)),
            _const_spec((4, HEAD_DIM)),
            _const_spec((2 * HEAD_DIM, 1)),
        ],
        out_specs=pl.BlockSpec((1, 1, q_rows, LANES), lambda b, h, qi: (b, h, qi, 0)),
        out_shape=jax.ShapeDtypeStruct((bsz, B_HEADS, s_len, LANES), BF16),
        scratch_shapes=[
            pltpu.VMEM((qtiles, 2, LANES, tile), BF16),
            pltpu.VMEM((2, 2, ktiles * tile, tile), F32),
            pltpu.VMEM((qtiles, 2, 8, tile), F32),
            pltpu.VMEM((qtiles, 2, 2 * HEAD_DIM, tile), F32),
        ],
        compiler_params=_cparams(("parallel", "parallel", "parallel")),
        name="diff_attn",
    )(qb, kb, vb, bias_t, lam, g_col)


def _diff_attn_online_kernel(far_ref, q_ref, k_ref, v_ref, bias_ref, lam_ref, g_ref, o_ref,
                             qz_ref, m_ref, l_ref, acc_ref, *, lam_init):
    h = pl.program_id(1)
    qi = pl.program_id(2)
    ki = pl.program_id(3)
    nk = pl.num_programs(3)

    @pl.when(ki == 0)
    def _init():
        _split_maps(q_ref[0], qz_ref)
        m_ref[...] = jnp.full(m_ref.shape, NEG, F32)
        l_ref[...] = jnp.zeros(l_ref.shape, F32)
        acc_ref[...] = jnp.zeros(acc_ref.shape, F32)

    delta = ki - qi

    def step(c, bias_tile, bias_row):
        s_t = lax.dot_general(k_ref[0], qz_ref[c], _NT, preferred_element_type=F32)
        if bias_tile is not None:
            s_t = s_t + bias_tile
        m_old = m_ref[c]
        m_new = jnp.maximum(m_old, jnp.max(s_t, axis=0, keepdims=True) + bias_row)
        alpha = jnp.exp2(m_old - m_new)
        p = jnp.exp2(s_t - (m_new - bias_row))
        l_ref[c] = alpha * l_ref[c] + jnp.sum(p, axis=0, keepdims=True)
        pv = lax.dot_general(v_ref[0], p.astype(BF16), _TN, preferred_element_type=F32)
        acc_ref[c] = alpha * acc_ref[c] + pv
        m_ref[c] = m_new

    @pl.when(jnp.abs(delta) <= 1)
    def _near():
        bias_tile = bias_ref[0, delta + 1]
        for c in range(2):
            step(c, bias_tile, 0.0)

    @pl.when(jnp.abs(delta) > 1)
    def _far():
        side = jnp.where(delta < 0, far_ref[h, 0], far_ref[h, 1])
        for c in range(2):
            step(c, None, side)

    @pl.when(ki == nk - 1)
    def _finish():
        o_ref[0] = _diff_out(acc_ref[0], l_ref[0], acc_ref[1], l_ref[1], lam_ref, g_ref, lam_init)


def _diff_attn_online(qb, kb, vb, far, bias_t, lam, g_col, lam_init, tile):
    bsz, _, s_len, _ = qb.shape
    nt = s_len // tile
    kernel = functools.partial(_diff_attn_online_kernel, lam_init=lam_init)

    def body(far_ref, q_ref, k_ref, v_ref, bias_ref, lam_ref, g_ref, o_ref, *scratch):
        kernel(far_ref, q_ref.at[0], k_ref.at[0], v_ref.at[0], bias_ref, lam_ref, g_ref, o_ref.at[0], *scratch)

    return pl.pallas_call(
        body,
        grid=(bsz, B_HEADS, nt, nt),
        in_specs=[
            pl.BlockSpec(memory_space=pltpu.SMEM),
            pl.BlockSpec((1, 1, tile, LANES), lambda b, h, qi, ki: (b, h, qi, 0)),
            pl.BlockSpec((1, 1, tile, LANES), lambda b, h, qi, ki: (b, h, ki, 0)),
            pl.BlockSpec((1, 1, tile, LANES), lambda b, h, qi, ki: (b, h, ki, 0)),
            pl.BlockSpec((1, 3, tile, tile), lambda b, h, qi, ki: (h, 0, 0, 0)),
            _const_spec((4, HEAD_DIM)),
            _const_spec((2 * HEAD_DIM, 1)),
        ],
        out_specs=pl.BlockSpec((1, 1, tile, LANES), lambda b, h, qi, ki: (b, h, qi, 0)),
        out_shape=jax.ShapeDtypeStruct((bsz, B_HEADS, s_len, LANES), BF16),
        scratch_shapes=[
            pltpu.VMEM((2, tile, LANES), BF16),
            pltpu.VMEM((2, 1, tile), F32),
            pltpu.VMEM((2, 1, tile), F32),
            pltpu.VMEM((2, 2 * HEAD_DIM, tile), F32),
        ],
        compiler_params=_cparams(("parallel", "parallel", "parallel", "arbitrary")),
        name="diff_attn_online",
    )(far, qb, kb, vb, bias_t, lam, g_col)


def _diff_attn(qb, kb, vb, db, lam, g_col, lam_init, tile):
    bounded = lambda: _diff_attn_bounded(qb, kb, vb, db["shifted"], lam, g_col, lam_init, tile)
    online = lambda: _diff_attn_online(qb, kb, vb, db["far"], db["near"], lam, g_col, lam_init, tile)
    return lax.cond(db["bounded_ok"], bounded, online)


FFN_CHUNKS = ((0, 1536), (1536, FFN_HIDDEN))


def _mix_ffn_kernel(x_ref, a_ref, b_ref, wa_ref, wb_ref, g_ref, wg_ref, wu_ref, wd_ref, o_ref):
    if len(b_ref.shape) == 4:
        b = jnp.concatenate([b_ref[0, hd] for hd in range(b_ref.shape[1])], axis=-1)
    else:
        b = b_ref[0]
    mix = (jnp.dot(a_ref[0], wa_ref[...], preferred_element_type=F32)
           + jnp.dot(b, wb_ref[...], preferred_element_type=F32))
    x1 = x_ref[0] + mix
    h = _rms(x1, g_ref[...]).astype(BF16)
    gates = [jnp.dot(h, wg_ref[:, lo:hi], preferred_element_type=F32) for lo, hi in FFN_CHUNKS]
    ups = [jnp.dot(h, wu_ref[:, lo:hi], preferred_element_type=F32) for lo, hi in FFN_CHUNKS]
    down = None
    for (lo, hi), gate, up in zip(FFN_CHUNKS, gates, ups):
        act = (gate * jax.nn.sigmoid(gate) * up).astype(BF16)
        part = jnp.dot(act, wd_ref[lo:hi, :], preferred_element_type=F32)
        down = part if down is None else down + part
    o_ref[0] = x1 + down


def _mix_ffn(x, a, b, wa, wb, g, wg, wu, wd, tm):
    bsz, s_len, _ = x.shape
    spt = s_len // tm
    row_spec = lambda w: pl.BlockSpec((1, tm, w), lambda i: (i // spt, i % spt, 0))
    if b.ndim == 4:
        b_spec = pl.BlockSpec((1, b.shape[1], tm, LANES), lambda i: (i // spt, 0, i % spt, 0))
    else:
        b_spec = row_spec(b.shape[-1])
    resident = pl.BlockSpec(memory_space=pltpu.VMEM)
    return pl.pallas_call(
        _mix_ffn_kernel,
        grid=(bsz * spt,),
        in_specs=[row_spec(D_MODEL), row_spec(a.shape[-1]), b_spec] + [resident] * 6,
        out_specs=row_spec(D_MODEL),
        out_shape=jax.ShapeDtypeStruct(x.shape, F32),
        compiler_params=_cparams(("parallel",)),
        name="mix_ffn",
    )(x, a, b, wa, wb, g, wg, wu, wd)


def _conv_in_kernel(x_ref, g_ref, w_ref, gb_ref, gx_ref, glu_ref):
    h = _rms(x_ref[0], g_ref[...]).astype(BF16)
    c = C_WIDTH
    proj = jnp.dot(h, w_ref[...], preferred_element_type=F32)
    gb, gc, xc, a, gate = [proj[:, i * c:(i + 1) * c] for i in range(5)]
    gb_ref[0] = gb.astype(BF16)
    gx_ref[0] = (gc * xc).astype(BF16)
    glu_ref[0] = (a * jax.nn.sigmoid(gate)).astype(BF16)


def _conv_in(x, g, w, tm):
    bsz, s_len, _ = x.shape
    spt = s_len // tm
    row_spec = lambda width: pl.BlockSpec((1, tm, width), lambda i: (i // spt, i % spt, 0))
    out = jax.ShapeDtypeStruct((bsz, s_len, C_WIDTH), BF16)
    return pl.pallas_call(
        _conv_in_kernel,
        grid=(bsz * spt,),
        in_specs=[row_spec(D_MODEL), _const_spec((1, D_MODEL)), _const_spec((D_MODEL, w.shape[1]))],
        out_specs=[row_spec(C_WIDTH)] * 3,
        out_shape=[out] * 3,
        compiler_params=_cparams(("parallel",)),
        name="conv_in",
    )(x, g, w)


CONV_CHUNK = 32
SUBLANES = 8


SHIFT_ROWS = 256
SHIFT_K = 384


def _conv_mix_kernel(gb_ref, gx_ref, gxp_ref, gxn_ref, u_ref, up_ref, un_ref, shift_ref,
                     scw_ref, dww_ref, dwb_ref, lng_ref, lnb_ref, yc_ref, yu_ref,
                     xb_ref, xs_ref, u_ref_f32, *, ts):
    t = pl.program_id(1)
    nt = pl.num_programs(1)
    padded = ts + 2 * HALO

    def taps(width):
        return [(j,) + divmod(HALO - width // 2 + j, SUBLANES)[::-1] for j in range(width)]

    def fill(cur, prev, nxt, width, xb, xs):
        zero = jnp.zeros((HALO, C_WIDTH), BF16)
        xb[0:HALO, :] = jnp.where(t > 0, prev[0], zero)
        xb[HALO:HALO + ts, :] = cur[0]
        xb[HALO + ts:padded, :] = jnp.where(t < nt - 1, nxt[0], zero)
        xb[padded:, :] = jnp.zeros((xb.shape[0] - padded, C_WIDTH), BF16)
        xs[0] = xb[0:padded, :].astype(F32)
        copies = sorted({r for _, r, _ in taps(width)} - {0})
        for o0 in range(0, ts, SHIFT_ROWS):
            for r in copies:
                xs[r, o0:o0 + SHIFT_ROWS, :] = jnp.dot(shift_ref[r - 1], xb[o0:o0 + SHIFT_K, :],
                                                       preferred_element_type=F32)
        for r in copies:
            xs[r, ts:padded, :] = jnp.dot(shift_ref[r - 1, 0:2 * HALO, 0:LANES], xb[ts:ts + LANES, :],
                                          preferred_element_type=F32)

    def conv(w_ref, width, s0, xs):
        acc = None
        groups = CONV_CHUNK // SUBLANES
        for r in sorted({r for _, r, _ in taps(width)}):
            mine = [(j, q) for j, r_j, q in taps(width) if r_j == r]
            q_lo, q_hi = min(q for _, q in mine), max(q for _, q in mine)
            span = xs[r, pl.ds(s0 + SUBLANES * q_lo, CONV_CHUNK + SUBLANES * (q_hi - q_lo)), :]
            span = span.reshape(groups + q_hi - q_lo, SUBLANES, -1)
            for j, q in mine:
                term = span[q - q_lo:q - q_lo + groups] * w_ref[j]
                acc = term if acc is None else acc + term
        return acc.reshape(CONV_CHUNK, -1)

    def chunks(body):
        for c in range(ts // CONV_CHUNK):
            body(c * CONV_CHUNK)

    fill(gx_ref, gxp_ref, gxn_ref, SHORT_CONV, xb_ref, xs_ref)

    def short(s0):
        out = pl.ds(s0, CONV_CHUNK)
        yc_ref[0, out, :] = (gb_ref[0, out, :].astype(F32) * conv(scw_ref, SHORT_CONV, s0, xs_ref)).astype(BF16)
    chunks(short)

    fill(u_ref, up_ref, un_ref, CONF_CONV, xb_ref, xs_ref)

    def conf(s0):
        u_ref_f32[pl.ds(s0, CONV_CHUNK), :] = conv(dww_ref, CONF_CONV, s0, xs_ref) + dwb_ref[...]
    chunks(conf)

    u = u_ref_f32[...]
    mu = jnp.mean(u, axis=-1, keepdims=True)
    uc = u - mu
    var = jnp.mean(uc * uc, axis=-1, keepdims=True)
    y = uc * lax.rsqrt(var + EPS) * lng_ref[...] + lnb_ref[...]
    yu_ref[0] = (y * jax.nn.sigmoid(y)).astype(BF16)


def _conv_mix(gb, gx, glu, scw, dww, dwb, lng, lnb, ts):
    bsz, s_len, _ = gb.shape
    nt = s_len // ts
    r = ts // HALO
    nh = s_len // HALO
    assert C_WIDTH == D_WIDTH
    cur = pl.BlockSpec((1, ts, C_WIDTH), lambda b, t: (b, t, 0))
    prev = pl.BlockSpec((1, HALO, C_WIDTH), lambda b, t: (b, jnp.maximum(t * r - 1, 0), 0))
    nxt = pl.BlockSpec((1, HALO, C_WIDTH), lambda b, t: (b, jnp.minimum((t + 1) * r, nh - 1), 0))
    out_spec = pl.BlockSpec((1, ts, C_WIDTH), lambda b, t: (b, t, 0))
    assert ts % SHIFT_ROWS == 0
    i, j = jnp.arange(SHIFT_ROWS)[:, None], jnp.arange(SHIFT_K)[None, :]
    shift = jnp.stack([(j == i + r) for r in range(1, SUBLANES)]).astype(BF16)
    return pl.pallas_call(
        functools.partial(_conv_mix_kernel, ts=ts),
        grid=(bsz, nt),
        in_specs=[cur, cur, prev, nxt, cur, prev, nxt, _const_spec(shift.shape),
                  _const_spec((SHORT_CONV, SUBLANES, C_WIDTH)), _const_spec((CONF_CONV, SUBLANES, D_WIDTH)),
                  _const_spec((1, D_WIDTH)), _const_spec((1, D_WIDTH)), _const_spec((1, D_WIDTH))],
        out_specs=[out_spec, out_spec],
        out_shape=[jax.ShapeDtypeStruct((bsz, s_len, C_WIDTH), BF16),
                   jax.ShapeDtypeStruct((bsz, s_len, D_WIDTH), BF16)],
        scratch_shapes=[pltpu.VMEM((ts + LANES, C_WIDTH), BF16),
                        pltpu.VMEM((SUBLANES, ts + 2 * HALO, C_WIDTH), F32), pltpu.VMEM((ts, D_WIDTH), F32)],
        compiler_params=_cparams(("parallel", "parallel")),
        name="conv_mix",
    )(gb, gx, gx, gx, glu, glu, glu, shift, scw, dww, dwb, lng, lnb)


def _rel_bucket(rel):
    half = NUM_BUCKETS // 2
    max_exact = half // 2
    n = jnp.abs(rel)
    large = max_exact + (jnp.log(jnp.maximum(n, 1).astype(F32) / max_exact)
                         / math.log(MAX_DISTANCE / max_exact) * (half - max_exact)).astype(jnp.int32)
    large = jnp.minimum(large, half - 1)
    return jnp.where(rel > 0, half, 0) + jnp.where(n < max_exact, n, large)


def _toeplitz(u, rows, cols):
    length = u.shape[-1]
    flat = jnp.tile(u, (1,) * (u.ndim - 1) + (rows,))[..., :rows * (length - 1)]
    return flat.reshape(u.shape[:-1] + (rows, length - 1))[..., :cols]


LOG2E = math.log2(math.e)


def _win_bias(rel_bias, a_sink, a_qn, a_kn):
    order = jnp.array(WIN_HEAD_ORDER)
    table = rel_bias[:, :A_Q_HEADS].astype(F32) * LOG2E
    length = 4 * BLOCK
    n = jnp.arange(length)
    rel = jnp.where(n < BLOCK, -n, length - n) - BLOCK
    vec = jnp.where((jnp.abs(rel) <= WINDOW)[None], table[_rel_bucket(rel)].T, NEG)
    base = _toeplitz(vec, 3 * BLOCK, BLOCK)[order]
    mid = base.transpose(1, 0, 2).reshape(3 * BLOCK, A_Q_HEADS * BLOCK)
    key_block = (jnp.arange(3 * BLOCK) // BLOCK)[:, None]
    bias = jnp.stack([jnp.where(key_block == 0, NEG, mid), mid, jnp.where(key_block == 2, NEG, mid)], axis=0)
    sink = a_sink.astype(F32) * LOG2E
    s_max = 1.02 * LOG2E * math.sqrt(HEAD_DIM) * jnp.max(jnp.abs(a_qn * a_kn))
    bound = s_max + jnp.maximum(jnp.max(table, axis=0), sink)
    lowest = jnp.minimum(-s_max + jnp.min(table, axis=0), sink) - bound
    columns = lambda per_head: jnp.repeat(per_head[order], BLOCK)[None, :]
    return {"bias": bias, "sink": columns(sink),
            "bias_shifted": bias - columns(bound)[None], "sink_shifted": columns(sink - bound),
            "bounded_ok": jnp.all(lowest > MIN_EXP2_ARG)}
MIN_EXP2_ARG = -120.0


def _diff_bias(rel_bias, b_qn, b_kn, tile):
    table = rel_bias[:, A_Q_HEADS:].astype(F32) * LOG2E
    length = 2 * tile
    n = jnp.arange(length)
    k_minus_q = jnp.where(n < tile, -n, length - n)
    rel = jnp.clip(k_minus_q[None, :] + jnp.array([-tile, 0, tile])[:, None], 1 - length, length - 1)
    near = _toeplitz(table[_rel_bucket(rel)].transpose(2, 0, 1), tile, tile)
    far = table[_rel_bucket(jnp.array([-length, length]))].T
    s_max = 1.02 * LOG2E * math.sqrt(HEAD_DIM) * jnp.max(jnp.abs(b_qn * b_kn))
    b_max, b_min = jnp.max(table, axis=0), jnp.min(table, axis=0)
    bound = s_max + b_max
    const = lambda col: jnp.broadcast_to((far[:, col] - bound)[:, None, None, None], (B_HEADS, 1, tile, tile))
    shifted = jnp.concatenate([const(0), near - bound[:, None, None, None], const(1)], axis=1)
    bounded_ok = jnp.all(-2.0 * s_max - (b_max - b_min) > MIN_EXP2_ARG)
    return {"near": near, "far": far, "shifted": shifted, "bounded_ok": bounded_ok}


def _by_win_head(m, axis):
    take = lambda h: lax.slice_in_dim(m, h * HEAD_DIM, (h + 1) * HEAD_DIM, axis=axis)
    return jnp.concatenate([take(h) for h in WIN_HEAD_ORDER], axis=axis)


def _attn_in_params(w_in, a_qn, a_kn, b_qn, b_kn):
    d = HEAD_DIM
    k0 = A_Q_HEADS * d
    w = jnp.concatenate([_by_win_head(w_in[:, :k0], 1), w_in[:, k0:]], axis=1).astype(BF16)
    scale = HEAD_DIM ** -0.5 * LOG2E
    ones = lambda n: jnp.ones((n,), F32)
    gain = jnp.concatenate([
        jnp.tile(a_qn, A_Q_HEADS) * scale, jnp.tile(a_kn, A_KV_HEADS), ones(A_KV_HEADS * d),
        jnp.tile(b_qn, 2 * B_HEADS) * scale, jnp.tile(b_kn, 2 * B_HEADS), ones(2 * B_HEADS * d),
    ]).astype(F32)[None, :]
    seg_id = jnp.arange(MXU_COLS) // d
    seg = jnp.where(seg_id[:, None] == seg_id[None, :], 1.0 / d, 0.0).astype(BF16)
    return w, gain, seg


def _trunk(x, p, *, tm, tile, ts, qb):
    qa, ka, va, qb_, kb, vb = _attn_in(x, p["mix_g"][0], p["attn_w"], p["attn_gain"], p["seg"], tm)
    ya = _win_attn(qa, ka, va, p["win_bias"], qb)
    yb = _diff_attn(qb_, kb, vb, p["diff_bias"][tile], p["lam"], p["subln"], p["lam_init"], tile)
    x = _mix_ffn(x, ya, yb, p["attn_wo_a"], p["attn_wo_b"],
                 p["ffn_g"][0], p["wg"][0], p["wu"][0], p["wd"][0], tm)

    gb, gx, glu = _conv_in(x, p["mix_g"][1], p["conv_w"], tm)
    yc, yu = _conv_mix(gb, gx, glu, p["scw"], p["dww"], p["dwb"], p["lng"], p["lnb"], ts)
    return _mix_ffn(x, yc, yu, p["conv_wo_a"], p["conv_wo_b"],
                    p["ffn_g"][1], p["wg"][1], p["wu"][1], p["wd"][1], tm)


def _prepare(rel_bias, mix_norm, ffn_norm, w_gate, w_up, w_down, attn_w_in, attn_w_out, a_q_norm, a_k_norm,
             a_sink, b_q_norm, b_k_norm, b_lambda, b_subln, conv_w_in, conv_w_out, short_conv_w, conf_dw_w,
             conf_dw_b, conf_ln_g, conf_ln_b, tiles):
    attn_w, attn_gain, seg = _attn_in_params(attn_w_in[0], a_q_norm[0], a_k_norm[0], b_q_norm[0], b_k_norm[0])
    half = A_Q_HEADS * HEAD_DIM
    return {
        "mix_g": [mix_norm[l][None, :].astype(F32) for l in range(2)],
        "ffn_g": [ffn_norm[l][None, :].astype(F32) for l in range(2)],
        "wg": [w_gate[l].astype(BF16) for l in range(2)],
        "wu": [w_up[l].astype(BF16) for l in range(2)],
        "wd": [w_down[l].astype(BF16) for l in range(2)],
        "attn_w": attn_w, "attn_gain": attn_gain, "seg": seg,
        "attn_wo_a": _by_win_head(attn_w_out[0][:half], 0).astype(BF16),
        "attn_wo_b": attn_w_out[0][half:].astype(BF16),
        "win_bias": _win_bias(rel_bias, a_sink[0], a_q_norm[0], a_k_norm[0]),
        "diff_bias": {t: _diff_bias(rel_bias, b_q_norm[0], b_k_norm[0], t) for t in tiles},
        "lam": b_lambda[0].astype(F32),
        "subln": b_subln[0].astype(F32)[:, None],
        "lam_init": 0.8 - 0.6 * math.exp(-0.3 * 0),
        "conv_w": conv_w_in[0].astype(BF16),
        "conv_wo_a": conv_w_out[0][:C_WIDTH].astype(BF16), "conv_wo_b": conv_w_out[0][C_WIDTH:].astype(BF16),
        "scw": jnp.broadcast_to(short_conv_w[0].astype(F32)[:, None, :], (SHORT_CONV, SUBLANES, C_WIDTH)),
        "dww": jnp.broadcast_to(conf_dw_w[0].astype(F32)[:, None, :], (CONF_CONV, SUBLANES, D_WIDTH)),
        "dwb": conf_dw_b[0][None, :].astype(F32),
        "lng": conf_ln_g[0][None, :].astype(F32), "lnb": conf_ln_b[0][None, :].astype(F32),
    }


def _tiling(s_len):
    return dict(tm=min(1024, s_len), tile=min(512, s_len), ts=min(512, s_len), qb=8)


def kernel(x_prompt, x_sample, rel_bias, mix_norm, ffn_norm, w_gate, w_up, w_down, attn_w_in, attn_w_out,
           a_q_norm, a_k_norm, a_sink, b_q_norm, b_k_norm, b_lambda, b_subln, conv_w_in, conv_w_out,
           short_conv_w, conf_dw_w, conf_dw_b, conf_ln_g, conf_ln_b):
    tp, tsm = _tiling(x_prompt.shape[1]), _tiling(x_sample.shape[1])
    p = _prepare(rel_bias, mix_norm, ffn_norm, w_gate, w_up, w_down, attn_w_in, attn_w_out, a_q_norm,
                 a_k_norm, a_sink, b_q_norm, b_k_norm, b_lambda, b_subln, conv_w_in, conv_w_out,
                 short_conv_w, conf_dw_w, conf_dw_b, conf_ln_g, conf_ln_b, {tp["tile"], tsm["tile"]})
    return (_trunk(x_prompt, p, **tp), _trunk(x_sample, p, **tsm))
```

```python
import functools
import math

import jax
import jax.numpy as jnp
from jax import lax
from jax.experimental import pallas as pl
from jax.experimental.pallas import tpu as pltpu

D_MODEL = 1024
HEAD_DIM = 64
A_Q_HEADS = 8
A_KV_HEADS = 2
WINDOW = 128
BLOCK = 128
B_HEADS = 4
NUM_BUCKETS = 32
MAX_DISTANCE = 128
C_WIDTH = 512
D_WIDTH = 512
SHORT_CONV = 3
CONF_CONV = 31
FFN_HIDDEN = 2816
EPS = 1e-6
NEG = -1e30

LANES = 128
MXU_COLS = 256
VMEM_LIMIT_BYTES = 56 * 1024 * 1024

BF16 = jnp.bfloat16
F32 = jnp.float32

QA_W = A_Q_HEADS * HEAD_DIM
PROJ0_W = QA_W + 2 * A_KV_HEADS * HEAD_DIM + 3 * B_HEADS * 2 * HEAD_DIM
NORM_CHUNKS0 = ("all", "all", "low", "all", "all", "all", "all", None, None)
WIN_HEAD_ORDER = (0, 4, 1, 5, 2, 6, 3, 7)

HALO = 16


def _cparams(sem):
    return pltpu.CompilerParams(dimension_semantics=sem, vmem_limit_bytes=VMEM_LIMIT_BYTES)


def _const_spec(shape):
    nd = len(shape)
    return pl.BlockSpec(shape, lambda *_: (0,) * nd)


def _rms(x, g):
    ms = jnp.mean(x * x, axis=-1, keepdims=True)
    return x * lax.rsqrt(ms + EPS) * g


def _attn_in_kernel(x_ref, g_ref, w_ref, gain_ref, seg_ref, qa_ref, ka_ref, va_ref, qb_ref, kb_ref, vb_ref):
    half = MXU_COLS // 2

    def store(c, y):
        if c < 2:
            qa_ref[:, c * MXU_COLS:(c + 1) * MXU_COLS] = y
        elif c == 2:
            ka_ref[...] = y[:, :half]
            va_ref[...] = y[:, half:]
        else:
            ref, first = ((qb_ref, 3), (kb_ref, 5), (vb_ref, 7))[(c - 3) // 2]
            ref[0, 2 * (c - first)] = y[:, :half]
            ref[0, 2 * (c - first) + 1] = y[:, half:]

    h = _rms(x_ref[...], g_ref[...]).astype(BF16)
    seg = seg_ref[...]
    low = lax.broadcasted_iota(jnp.int32, (1, MXU_COLS), 1) < MXU_COLS // 2
    chunk = lambda c: slice(c * MXU_COLS, (c + 1) * MXU_COLS)
    accs = [jnp.dot(h, w_ref[:, chunk(c)], preferred_element_type=F32) for c in range(len(NORM_CHUNKS0))]
    for c, normed in enumerate(NORM_CHUNKS0):
        acc = accs[c]
        if normed is not None:
            ms = jnp.dot((acc * acc).astype(BF16), seg, preferred_element_type=F32)
            scale = lax.rsqrt(ms + EPS) * gain_ref[:, chunk(c)]
            acc = acc * (scale if normed == "all" else jnp.where(low, scale, 1.0))
        store(c, acc.astype(BF16))


def _attn_in(x, g, w, gain, seg, tm):
    bsz, s_len, _ = x.shape
    spt = s_len // tm
    flat = lambda width: pl.BlockSpec((1, tm, width), lambda i: (i // spt, i % spt, 0))
    heads = pl.BlockSpec((1, B_HEADS, tm, LANES), lambda i: (i // spt, 0, i % spt, 0))
    sds = lambda *shape: jax.ShapeDtypeStruct(shape, BF16)
    head_major = sds(bsz, B_HEADS, s_len, LANES)

    def body(x_ref, g_ref, w_ref, gain_ref, seg_ref, qa_ref, ka_ref, va_ref, qb_ref, kb_ref, vb_ref):
        _attn_in_kernel(x_ref.at[0], g_ref, w_ref, gain_ref, seg_ref,
                        qa_ref.at[0], ka_ref.at[0], va_ref.at[0], qb_ref, kb_ref, vb_ref)

    return pl.pallas_call(
        body,
        grid=(bsz * spt,),
        in_specs=[
            flat(D_MODEL),
            _const_spec((1, D_MODEL)),
            _const_spec((D_MODEL, PROJ0_W)),
            _const_spec((1, PROJ0_W)),
            _const_spec((MXU_COLS, MXU_COLS)),
        ],
        out_specs=[flat(QA_W), flat(LANES), flat(LANES), heads, heads, heads],
        out_shape=[sds(bsz, s_len, QA_W), sds(bsz, s_len, LANES), sds(bsz, s_len, LANES),
                   head_major, head_major, head_major],
        compiler_params=_cparams(("parallel",)),
        name="attn_in",
    )(x, g, w, gain, seg)


WIN_COLS = A_Q_HEADS * BLOCK


def _win_attn_kernel(q_ref, kp_ref, kc_ref, kn_ref, vp_ref, vc_ref, vn_ref, bias_ref, sink_ref, o_ref, *,
                     qb, bounded):
    step = pl.program_id(1)
    nb = pl.num_programs(1) * qb
    low = lax.broadcasted_iota(jnp.int32, (1, LANES), 1) < HEAD_DIM
    top = lax.broadcasted_iota(jnp.int32, (LANES, 1), 0) < HEAD_DIM
    sink = sink_ref[...]

    def piece(prev_ref, cur_ref, next_ref, i):
        if i == 0:
            return prev_ref[0]
        if i == qb + 1:
            return next_ref[0]
        return cur_ref[0, (i - 1) * BLOCK:i * BLOCK, :]

    def scores(b):
        kcat = jnp.concatenate([piece(kp_ref, kc_ref, kn_ref, b + i) for i in range(3)], axis=0)
        q = q_ref[0, b * BLOCK:(b + 1) * BLOCK, :]
        zero = jnp.zeros((BLOCK, LANES), BF16)
        halves = []
        for j in range(A_Q_HEADS // 2):
            slab = q[:, j * LANES:(j + 1) * LANES]
            halves += [jnp.where(low, slab, zero), jnp.where(low, zero, slab)]
        qcat = jnp.concatenate(halves, axis=0)
        return lax.dot_general(kcat, qcat, _NT, preferred_element_type=F32)

    def weights(b, s_t):
        n = step * qb + b
        edge = jnp.where(n == 0, 0, jnp.where(n == nb - 1, 2, 1))
        s_t = s_t + bias_ref[edge]
        if bounded:
            p = jnp.exp2(s_t)
            den = jnp.sum(p, axis=0, keepdims=True) + jnp.exp2(sink)
        else:
            m = jnp.maximum(jnp.max(s_t, axis=0, keepdims=True), sink)
            p = jnp.exp2(s_t - m)
            den = jnp.sum(p, axis=0, keepdims=True) + jnp.exp2(sink - m)
        return p.astype(BF16), den

    def values(b, p, den):
        vcat = jnp.concatenate([piece(vp_ref, vc_ref, vn_ref, b + i) for i in range(3)], axis=0)
        return lax.dot_general(vcat, p, _TN, preferred_element_type=F32) / den

    s_all = [scores(b) for b in range(qb)]
    w_all = [weights(b, s_all[b]) for b in range(qb)]
    o_all = [values(b, *w_all[b]) for b in range(qb)]
    for b in range(qb):
        for j in range(A_Q_HEADS // 2):
            c0 = 2 * j * BLOCK
            slab_t = jnp.where(top, o_all[b][:, c0:c0 + BLOCK], o_all[b][:, c0 + BLOCK:c0 + 2 * BLOCK])
            o_ref[0, b * BLOCK:(b + 1) * BLOCK, j * LANES:(j + 1) * LANES] = slab_t.T.astype(BF16)


def _win_attn(qa, ka, va, wb, qb):
    fast = lambda: _win_attn_call(qa, ka, va, wb["sink_shifted"], wb["bias_shifted"], qb, True)
    exact_max = lambda: _win_attn_call(qa, ka, va, wb["sink"], wb["bias"], qb, False)
    return lax.cond(wb["bounded_ok"], fast, exact_max)


def _win_attn_call(qa, ka, va, sink_row, bias, qb, bounded):
    bsz, s_len, _ = qa.shape
    nb = s_len // BLOCK
    assert nb % qb == 0 and nb >= 2
    kv_specs = [
        pl.BlockSpec((1, BLOCK, LANES), lambda b, t: (b, jnp.maximum(t * qb - 1, 0), 0)),
        pl.BlockSpec((1, qb * BLOCK, LANES), lambda b, t: (b, t, 0)),
        pl.BlockSpec((1, BLOCK, LANES), lambda b, t: (b, jnp.minimum((t + 1) * qb, nb - 1), 0)),
    ]
    return pl.pallas_call(
        functools.partial(_win_attn_kernel, qb=qb, bounded=bounded),
        grid=(bsz, nb // qb),
        in_specs=[pl.BlockSpec((1, qb * BLOCK, QA_W), lambda b, t: (b, t, 0))] + kv_specs + kv_specs
                 + [pl.BlockSpec(memory_space=pltpu.VMEM), pl.BlockSpec(memory_space=pltpu.VMEM)],
        out_specs=pl.BlockSpec((1, qb * BLOCK, QA_W), lambda b, t: (b, t, 0)),
        out_shape=jax.ShapeDtypeStruct((bsz, s_len, QA_W), BF16),
        compiler_params=_cparams(("parallel", "parallel")),
        name="win_attn",
    )(qa, ka, ka, ka, va, va, va, bias, sink_row)


_NT = (((1,), (1,)), ((), ()))
_TN = (((0,), (0,)), ((), ()))


def _split_maps(q, qz_ref):
    lane = lax.broadcasted_iota(jnp.int32, (1, LANES), 1)
    zero = jnp.zeros_like(q)
    qz_ref[0] = jnp.where(lane < HEAD_DIM, q, zero)
    qz_ref[1] = jnp.where(lane < HEAD_DIM, zero, q)


def _diff_out(acc0, l0, acc1, l1, lam_ref, g_ref, lam_init):
    lam = lam_ref[...]
    lam_full = (jnp.exp(jnp.sum(lam[0:1] * lam[1:2], axis=-1, keepdims=True))
                - jnp.exp(jnp.sum(lam[2:3] * lam[3:4], axis=-1, keepdims=True)) + lam_init)
    o = acc0 / l0 - lam_full * (acc1 / l1)
    ms = jnp.mean(o * o, axis=0, keepdims=True)
    y = o * lax.rsqrt(ms + EPS) * g_ref[...] * (1.0 - lam_init)
    return y.T.astype(BF16)


def _diff_attn_bounded_kernel(q_ref, k_ref, v_ref, bias_ref, lam_ref, g_ref, o_ref,
                              qz_ref, s_ref, l_ref, acc_ref, *, lam_init, tile, qtiles, ktiles):
    qstep = pl.program_id(2)
    ktile = ktiles * tile
    nk = k_ref.shape[1] // ktile
    first_map = lax.broadcasted_iota(jnp.int32, (LANES, 1), 0) < HEAD_DIM
    for qt in range(qtiles):
        q_t = q_ref[0, qt * tile:(qt + 1) * tile, :].astype(F32).T
        qz_ref[qt, 0] = jnp.where(first_map, q_t, 0.0).astype(BF16)
        qz_ref[qt, 1] = jnp.where(first_map, 0.0, q_t).astype(BF16)
    l_ref[...] = jnp.zeros(l_ref.shape, F32)
    acc_ref[...] = jnp.zeros(acc_ref.shape, F32)

    def produce(qt, ki, buf):
        k = k_ref[0, ki * ktile:(ki + 1) * ktile, :]
        for c in range(2):
            s_ref[buf, c] = jnp.dot(k, qz_ref[qt, c], preferred_element_type=F32)

    def consume(qt, ki, buf):
        v = v_ref[0, ki * ktile:(ki + 1) * ktile, :]
        qi = qstep * qtiles + qt
        for c in range(2):
            parts = []
            for sub in range(ktiles):
                bias = bias_ref[0, jnp.clip(ki * ktiles + sub - qi + 2, 0, 4)]
                p = jnp.exp2(s_ref[buf, c, sub * tile:(sub + 1) * tile, :] + bias)
                l_ref[qt, c] += jnp.sum(p.reshape(tile // 8, 8, tile), axis=0)
                parts.append(p.astype(BF16))
            p_all = parts[0] if len(parts) == 1 else jnp.concatenate(parts, axis=0)
            acc_ref[qt, c] += lax.dot_general(v, p_all, _TN, preferred_element_type=F32)

    def finish(qt):
        l0 = jnp.sum(l_ref[qt, 0], axis=0, keepdims=True)
        l1 = jnp.sum(l_ref[qt, 1], axis=0, keepdims=True)
        o_ref[0, qt * tile:(qt + 1) * tile, :] = _diff_out(acc_ref[qt, 0], l0, acc_ref[qt, 1], l1,
                                                           lam_ref, g_ref, lam_init)

    steps =[(qt, ki) for qt in range(qtiles) for ki in range(nk)]
    produce(*steps[0], 0)
    for i, (qt, ki) in enumerate(steps):
        if i + 1 < len(steps):
            produce(*steps[i + 1], (i + 1) % 2)
        consume(qt, ki, i % 2)
        if ki == nk - 1:
            finish(qt)


DIFF_TILES = 16
DIFF_KEY_TILES = 4


def _diff_attn_bounded(qb, kb, vb, bias_t, lam, g_col, lam_init, tile):
    bsz, _, s_len, _ = qb.shape
    tiles = s_len // tile
    assert tiles <= DIFF_TILES
    qtiles = min(DIFF_TILES // tiles, tiles)
    ktiles = max(1, min(DIFF_KEY_TILES, tiles // 2))
    assert tiles % qtiles == 0 and tiles % ktiles == 0
    kernel = functools.partial(_diff_attn_bounded_kernel, lam_init=lam_init, tile=tile, qtiles=qtiles,
                               ktiles=ktiles)

    def body(q_ref, k_ref, v_ref, bias_ref, lam_ref, g_ref, o_ref, *scratch):
        kernel(q_ref.at[0], k_ref.at[0], v_ref.at[0], bias_ref, lam_ref, g_ref, o_ref.at[0], *scratch)

    q_rows = qtiles * tile
    return pl.pallas_call(
        body,
        grid=(bsz, B_HEADS, s_len // q_rows),
        in_specs=[
            pl.BlockSpec((1, 1, q_rows, LANES), lambda b, h, qi: (b, h, qi, 0)),
            pl.BlockSpec((1, 1, s_len, LANES), lambda b, h, qi: (b, h, 0, 0)),
            pl.BlockSpec((1, 1, s_len, LANES), lambda b, h, qi: (b, h, 0, 0)),
            pl.BlockSpec((1, 5, tile, tile), lambda b, h, qi: (h, 0, 0, 0)),
            _const_spec((4, HEAD_DIM)),
            _const_spec((2 * HEAD_DIM, 1)),
        ],
        out_specs=pl.BlockSpec((1, 1, q_rows, LANES), lambda b, h, qi: (b, h, qi, 0)),
        out_shape=jax.ShapeDtypeStruct((bsz, B_HEADS, s_len, LANES), BF16),
        scratch_shapes=[
            pltpu.VMEM((qtiles, 2, LANES, tile), BF16),
            pltpu.VMEM((2, 2, ktiles * tile, tile), F32),
            pltpu.VMEM((qtiles, 2, 8, tile), F32),
            pltpu.VMEM((qtiles, 2, 2 * HEAD_DIM, tile), F32),
        ],
        compiler_params=_cparams(("parallel", "parallel", "parallel")),
        name="diff_attn",
    )(qb, kb, vb, bias_t, lam, g_col)


def _diff_attn_online_kernel(far_ref, q_ref, k_ref, v_ref, bias_ref, lam_ref, g_ref, o_ref,
                             qz_ref, m_ref, l_ref, acc_ref, *, lam_init):
    h = pl.program_id(1)
    qi = pl.program_id(2)
    ki = pl.program_id(3)
    nk = pl.num_programs(3)

    @pl.when(ki == 0)
    def _init():
        _split_maps(q_ref[0], qz_ref)
        m_ref[...] = jnp.full(m_ref.shape, NEG, F32)
        l_ref[...] = jnp.zeros(l_ref.shape, F32)
        acc_ref[...] = jnp.zeros(acc_ref.shape, F32)

    delta = ki - qi

    def step(c, bias_tile, bias_row):
        s_t = lax.dot_general(k_ref[0], qz_ref[c], _NT, preferred_element_type=F32)
        if bias_tile is not None:
            s_t = s_t + bias_tile
        m_old = m_ref[c]
        m_new = jnp.maximum(m_old, jnp.max(s_t, axis=0, keepdims=True) + bias_row)
        alpha = jnp.exp2(m_old - m_new)
        p = jnp.exp2(s_t - (m_new - bias_row))
        l_ref[c] = alpha * l_ref[c] + jnp.sum(p, axis=0, keepdims=True)
        pv = lax.dot_general(v_ref[0], p.astype(BF16), _TN, preferred_element_type=F32)
        acc_ref[c] = alpha * acc_ref[c] + pv
        m_ref[c] = m_new

    @pl.when(jnp.abs(delta) <= 1)
    def _near():
        bias_tile = bias_ref[0, delta + 1]
        for c in range(2):
            step(c, bias_tile, 0.0)

    @pl.when(jnp.abs(delta) > 1)
    def _far():
        side = jnp.where(delta < 0, far_ref[h, 0], far_ref[h, 1])
        for c in range(2):
            step(c, None, side)

    @pl.when(ki == nk - 1)
    def _finish():
        o_ref[0] = _diff_out(acc_ref[0], l_ref[0], acc_ref[1], l_ref[1], lam_ref, g_ref, lam_init)


def _diff_attn_online(qb, kb, vb, far, bias_t, lam, g_col, lam_init, tile):
    bsz, _, s_len, _ = qb.shape
    nt = s_len // tile
    kernel = functools.partial(_diff_attn_online_kernel, lam_init=lam_init)

    def body(far_ref, q_ref, k_ref, v_ref, bias_ref, lam_ref, g_ref, o_ref, *scratch):
        kernel(far_ref, q_ref.at[0], k_ref.at[0], v_ref.at[0], bias_ref, lam_ref, g_ref, o_ref.at[0], *scratch)

    return pl.pallas_call(
        body,
        grid=(bsz, B_HEADS, nt, nt),
        in_specs=[
            pl.BlockSpec(memory_space=pltpu.SMEM),
            pl.BlockSpec((1, 1, tile, LANES), lambda b, h, qi, ki: (b, h, qi, 0)),
            pl.BlockSpec((1, 1, tile, LANES), lambda b, h, qi, ki: (b, h, ki, 0)),
            pl.BlockSpec((1, 1, tile, LANES), lambda b, h, qi, ki: (b, h, ki, 0)),
            pl.BlockSpec((1, 3, tile, tile), lambda b, h, qi, ki: (h, 0, 0, 0)),
            _const_spec((4, HEAD_DIM)),
            _const_spec((2 * HEAD_DIM, 1)),
        ],
        out_specs=pl.BlockSpec((1, 1, tile, LANES), lambda b, h, qi, ki: (b, h, qi, 0)),
        out_shape=jax.ShapeDtypeStruct((bsz, B_HEADS, s_len, LANES), BF16),
        scratch_shapes=[
            pltpu.VMEM((2, tile, LANES), BF16),
            pltpu.VMEM((2, 1, tile), F32),
            pltpu.VMEM((2, 1, tile), F32),
            pltpu.VMEM((2, 2 * HEAD_DIM, tile), F32),
        ],
        compiler_params=_cparams(("parallel", "parallel", "parallel", "arbitrary")),
        name="diff_attn_online",
    )(far, qb, kb, vb, bias_t, lam, g_col)


def _diff_attn(qb, kb, vb, db, lam, g_col, lam_init, tile):
    bounded = lambda: _diff_attn_bounded(qb, kb, vb, db["shifted"], lam, g_col, lam_init, tile)
    online = lambda: _diff_attn_online(qb, kb, vb, db["far"], db["near"], lam, g_col, lam_init, tile)
    return lax.cond(db["bounded_ok"], bounded, online)


FFN_CHUNKS = ((0, 1536), (1536, FFN_HIDDEN))


def _mix_ffn_kernel(x_ref, a_ref, b_ref, wa_ref, wb_ref, g_ref, wg_ref, wu_ref, wd_ref, o_ref):
    if len(b_ref.shape) == 4:
        b = jnp.concatenate([b_ref[0, hd] for hd in range(b_ref.shape[1])], axis=-1)
    else:
        b = b_ref[0]
    mix = (jnp.dot(a_ref[0], wa_ref[...], preferred_element_type=F32)
           + jnp.dot(b, wb_ref[...], preferred_element_type=F32))
    x1 = x_ref[0] + mix
    h = _rms(x1, g_ref[...]).astype(BF16)
    gates = [jnp.dot(h, wg_ref[:, lo:hi], preferred_element_type=F32) for lo, hi in FFN_CHUNKS]
    ups = [jnp.dot(h, wu_ref[:, lo:hi], preferred_element_type=F32) for lo, hi in FFN_CHUNKS]
    down = None
    for (lo, hi), gate, up in zip(FFN_CHUNKS, gates, ups):
        act = (gate * jax.nn.sigmoid(gate) * up).astype(BF16)
        part = jnp.dot(act, wd_ref[lo:hi, :], preferred_element_type=F32)
        down = part if down is None else down + part
    o_ref[0] = x1 + down


def _mix_ffn(x, a, b, wa, wb, g, wg, wu, wd, tm):
    bsz, s_len, _ = x.shape
    spt = s_len // tm
    row_spec = lambda w: pl.BlockSpec((1, tm, w), lambda i: (i // spt, i % spt, 0))
    if b.ndim == 4:
        b_spec = pl.BlockSpec((1, b.shape[1], tm, LANES), lambda i: (i // spt, 0, i % spt, 0))
    else:
        b_spec = row_spec(b.shape[-1])
    resident = pl.BlockSpec(memory_space=pltpu.VMEM)
    return pl.pallas_call(
        _mix_ffn_kernel,
        grid=(bsz * spt,),
        in_specs=[row_spec(D_MODEL), row_spec(a.shape[-1]), b_spec] + [resident] * 6,
        out_specs=row_spec(D_MODEL),
        out_shape=jax.ShapeDtypeStruct(x.shape, F32),
        compiler_params=_cparams(("parallel",)),
        name="mix_ffn",
    )(x, a, b, wa, wb, g, wg, wu, wd)


def _conv_in_kernel(x_ref, g_ref, w_ref, gb_ref, gx_ref, glu_ref):
    h = _rms(x_ref[0], g_ref[...]).astype(BF16)
    c = C_WIDTH
    proj = jnp.dot(h, w_ref[...], preferred_element_type=F32)
    gb, gc, xc, a, gate = [proj[:, i * c:(i + 1) * c] for i in range(5)]
    gb_ref[0] = gb.astype(BF16)
    gx_ref[0] = (gc * xc).astype(BF16)
    glu_ref[0] = (a * jax.nn.sigmoid(gate)).astype(BF16)


def _conv_in(x, g, w, tm):
    bsz, s_len, _ = x.shape
    spt = s_len // tm
    row_spec = lambda width: pl.BlockSpec((1, tm, width), lambda i: (i // spt, i % spt, 0))
    out = jax.ShapeDtypeStruct((bsz, s_len, C_WIDTH), BF16)
    return pl.pallas_call(
        _conv_in_kernel,
        grid=(bsz * spt,),
        in_specs=[row_spec(D_MODEL), _const_spec((1, D_MODEL)), _const_spec((D_MODEL, w.shape[1]))],
        out_specs=[row_spec(C_WIDTH)] * 3,
        out_shape=[out] * 3,
        compiler_params=_cparams(("parallel",)),
        name="conv_in",
    )(x, g, w)


CONV_CHUNK = 32
SUBLANES = 8


SHIFT_ROWS = 128
SHIFT_K = 256


def _conv_mix_kernel(gb_ref, gx_ref, gxp_ref, gxn_ref, u_ref, up_ref, un_ref, shift_ref,
                     scw_ref, dww_ref, dwb_ref, lng_ref, lnb_ref, yc_ref, yu_ref,
                     xb_ref, xs_ref, u_ref_f32, *, ts):
    t = pl.program_id(1)
    nt = pl.num_programs(1)
    padded = ts + 2 * HALO

    def taps(width):
        return [(j,) + divmod(HALO - width // 2 + j, SUBLANES)[::-1] for j in range(width)]

    def fill(cur, prev, nxt, width, xb, xs):
        zero = jnp.zeros((HALO, C_WIDTH), BF16)
        xb[0:HALO, :] = jnp.where(t > 0, prev[0], zero)
        xb[HALO:HALO + ts, :] = cur[0]
        xb[HALO + ts:padded, :] = jnp.where(t < nt - 1, nxt[0], zero)
        xb[padded:, :] = jnp.zeros((xb.shape[0] - padded, C_WIDTH), BF16)
        xs[0] = xb[0:padded, :].astype(F32)
        copies = sorted({r for _, r, _ in taps(width)} - {0})
        for o0 in range(0, ts, SHIFT_ROWS):
            for r in copies:
                xs[r, o0:o0 + SHIFT_ROWS, :] = jnp.dot(shift_ref[r - 1], xb[o0:o0 + SHIFT_K, :],
                                                       preferred_element_type=F32)
        for r in copies:
            xs[r, ts:padded, :] = jnp.dot(shift_ref[r - 1, 0:2 * HALO, 0:LANES], xb[ts:ts + LANES, :],
                                          preferred_element_type=F32)

    def conv(w_ref, width, s0, xs):
        acc = None
        groups = CONV_CHUNK // SUBLANES
        for r in sorted({r for _, r, _ in taps(width)}):
            mine = [(j, q) for j, r_j, q in taps(width) if r_j == r]
            q_lo, q_hi = min(q for _, q in mine), max(q for _, q in mine)
            span = xs[r, pl.ds(s0 + SUBLANES * q_lo, CONV_CHUNK + SUBLANES * (q_hi - q_lo)), :]
            span = span.reshape(groups + q_hi - q_lo, SUBLANES, -1)
            for j, q in mine:
                term = span[q - q_lo:q - q_lo + groups] * w_ref[j]
                acc = term if acc is None else acc + term
        return acc.reshape(CONV_CHUNK, -1)

    def chunks(body):
        for c in range(ts // CONV_CHUNK):
            body(c * CONV_CHUNK)

    fill(gx_ref, gxp_ref, gxn_ref, SHORT_CONV, xb_ref, xs_ref)

    def short(s0):
        out = pl.ds(s0, CONV_CHUNK)
        yc_ref[0, out, :] = (gb_ref[0, out, :].astype(F32) * conv(scw_ref, SHORT_CONV, s0, xs_ref)).astype(BF16)
    chunks(short)

    fill(u_ref, up_ref, un_ref, CONF_CONV, xb_ref, xs_ref)

    def conf(s0):
        u_ref_f32[pl.ds(s0, CONV_CHUNK), :] = conv(dww_ref, CONF_CONV, s0, xs_ref) + dwb_ref[...]
    chunks(conf)

    u = u_ref_f32[...]
    mu = jnp.mean(u, axis=-1, keepdims=True)
    uc = u - mu
    var = jnp.mean(uc * uc, axis=-1, keepdims=True)
    y = uc * lax.rsqrt(var + EPS) * lng_ref[...] + lnb_ref[...]
    yu_ref[0] = (y * jax.nn.sigmoid(y)).astype(BF16)


def _conv_mix(gb, gx, glu, scw, dww, dwb, lng, lnb, ts):
    bsz, s_len, _ = gb.shape
    nt = s_len // ts
    r = ts // HALO
    nh = s_len // HALO
    assert C_WIDTH == D_WIDTH
    cur = pl.BlockSpec((1, ts, C_WIDTH), lambda b, t: (b, t, 0))
    prev = pl.BlockSpec((1, HALO, C_WIDTH), lambda b, t: (b, jnp.maximum(t * r - 1, 0), 0))
    nxt = pl.BlockSpec((1, HALO, C_WIDTH), lambda b, t: (b, jnp.minimum((t + 1) * r, nh - 1), 0))
    out_spec = pl.BlockSpec((1, ts, C_WIDTH), lambda b, t: (b, t, 0))
    assert ts % SHIFT_ROWS == 0
    i, j = jnp.arange(SHIFT_ROWS)[:, None], jnp.arange(SHIFT_K)[None, :]
    shift = jnp.stack([(j == i + r) for r in range(1, SUBLANES)]).astype(BF16)
    return pl.pallas_call(
        functools.partial(_conv_mix_kernel, ts=ts),
        grid=(bsz, nt),
        in_specs=[cur, cur, prev, nxt, cur, prev, nxt, _const_spec(shift.shape),
                  _const_spec((SHORT_CONV, SUBLANES, C_WIDTH)), _const_spec((CONF_CONV, SUBLANES, D_WIDTH)),
                  _const_spec((1, D_WIDTH)), _const_spec((1, D_WIDTH)), _const_spec((1, D_WIDTH))],
        out_specs=[out_spec, out_spec],
        out_shape=[jax.ShapeDtypeStruct((bsz, s_len, C_WIDTH), BF16),
                   jax.ShapeDtypeStruct((bsz, s_len, D_WIDTH), BF16)],
        scratch_shapes=[pltpu.VMEM((ts + LANES, C_WIDTH), BF16),
                        pltpu.VMEM((SUBLANES, ts + 2 * HALO, C_WIDTH), F32), pltpu.VMEM((ts, D_WIDTH), F32)],
        compiler_params=_cparams(("parallel", "parallel")),
        name="conv_mix",
    )(gb, gx, gx, gx, glu, glu, glu, shift, scw, dww, dwb, lng, lnb)


def _rel_bucket(rel):
    half = NUM_BUCKETS // 2
    max_exact = half // 2
    n = jnp.abs(rel)
    large = max_exact + (jnp.log(jnp.maximum(n, 1).astype(F32) / max_exact)
                         / math.log(MAX_DISTANCE / max_exact) * (half - max_exact)).astype(jnp.int32)
    large = jnp.minimum(large, half - 1)
    return jnp.where(rel > 0, half, 0) + jnp.where(n < max_exact, n, large)


def _toeplitz(u, rows, cols):
    length = u.shape[-1]
    flat = jnp.tile(u, (1,) * (u.ndim - 1) + (rows,))[..., :rows * (length - 1)]
    return flat.reshape(u.shape[:-1] + (rows, length - 1))[..., :cols]


LOG2E = math.log2(math.e)


def _win_bias(rel_bias, a_sink, a_qn, a_kn):
    order = jnp.array(WIN_HEAD_ORDER)
    table = rel_bias[:, :A_Q_HEADS].astype(F32) * LOG2E
    length = 4 * BLOCK
    n = jnp.arange(length)
    rel = jnp.where(n < BLOCK, -n, length - n) - BLOCK
    vec = jnp.where((jnp.abs(rel) <= WINDOW)[None], table[_rel_bucket(rel)].T, NEG)
    base = _toeplitz(vec, 3 * BLOCK, BLOCK)[order]
    mid = base.transpose(1, 0, 2).reshape(3 * BLOCK, A_Q_HEADS * BLOCK)
    key_block = (jnp.arange(3 * BLOCK) // BLOCK)[:, None]
    bias = jnp.stack([jnp.where(key_block == 0, NEG, mid), mid, jnp.where(key_block == 2, NEG, mid)], axis=0)
    sink = a_sink.astype(F32) * LOG2E
    s_max = 1.02 * LOG2E * math.sqrt(HEAD_DIM) * jnp.max(jnp.abs(a_qn * a_kn))
    bound = s_max + jnp.maximum(jnp.max(table, axis=0), sink)
    lowest = jnp.minimum(-s_max + jnp.min(table, axis=0), sink) - bound
    columns = lambda per_head: jnp.repeat(per_head[order], BLOCK)[None, :]
    return {"bias": bias, "sink": columns(sink),
            "bias_shifted": bias - columns(bound)[None], "sink_shifted": columns(sink - bound),
            "bounded_ok": jnp.all(lowest > MIN_EXP2_ARG)}
MIN_EXP2_ARG = -120.0


def _diff_bias(rel_bias, b_qn, b_kn, tile):
    table = rel_bias[:, A_Q_HEADS:].astype(F32) * LOG2E
    length = 2 * tile
    n = jnp.arange(length)
    k_minus_q = jnp.where(n < tile, -n, length - n)
    rel = jnp.clip(k_minus_q[None, :] + jnp.array([-tile, 0, tile])[:, None], 1 - length, length - 1)
    near = _toeplitz(table[_rel_bucket(rel)].transpose(2, 0, 1), tile, tile)
    far = table[_rel_bucket(jnp.array([-length, length]))].T
    s_max = 1.02 * LOG2E * math.sqrt(HEAD_DIM) * jnp.max(jnp.abs(b_qn * b_kn))
    b_max, b_min = jnp.max(table, axis=0), jnp.min(table, axis=0)
    bound = s_max + b_max
    const = lambda col: jnp.broadcast_to((far[:, col] - bound)[:, None, None, None], (B_HEADS, 1, tile, tile))
    shifted = jnp.concatenate([const(0), near - bound[:, None, None, None], const(1)], axis=1)
    bounded_ok = jnp.all(-2.0 * s_max - (b_max - b_min) > MIN_EXP2_ARG)
    return {"near": near, "far": far, "shifted": shifted, "bounded_ok": bounded_ok}


def _by_win_head(m, axis):
    take = lambda h: lax.slice_in_dim(m, h * HEAD_DIM, (h + 1) * HEAD_DIM, axis=axis)
    return jnp.concatenate([take(h) for h in WIN_HEAD_ORDER], axis=axis)


def _attn_in_params(w_in, a_qn, a_kn, b_qn, b_kn):
    d = HEAD_DIM
    k0 = A_Q_HEADS * d
    w = jnp.concatenate([_by_win_head(w_in[:, :k0], 1), w_in[:, k0:]], axis=1).astype(BF16)
    scale = HEAD_DIM ** -0.5 * LOG2E
    ones = lambda n: jnp.ones((n,), F32)
    gain = jnp.concatenate([
        jnp.tile(a_qn, A_Q_HEADS) * scale, jnp.tile(a_kn, A_KV_HEADS), ones(A_KV_HEADS * d),
        jnp.tile(b_qn, 2 * B_HEADS) * scale, jnp.tile(b_kn, 2 * B_HEADS), ones(2 * B_HEADS * d),
    ]).astype(F32)[None, :]
    seg_id = jnp.arange(MXU_COLS) // d
    seg = jnp.where(seg_id[:, None] == seg_id[None, :], 1.0 / d, 0.0).astype(BF16)
    return w, gain, seg


def _trunk(x, p, *, tm, tile, ts, qb):
    qa, ka, va, qb_, kb, vb = _attn_in(x, p["mix_g"][0], p["attn_w"], p["attn_gain"], p["seg"], tm)
    ya = _win_attn(qa, ka, va, p["win_bias"], qb)
    yb = _diff_attn(qb_, kb, vb, p["diff_bias"][tile], p["lam"], p["subln"], p["lam_init"], tile)
    x = _mix_ffn(x, ya, yb, p["attn_wo_a"], p["attn_wo_b"],
                 p["ffn_g"][0], p["wg"][0], p["wu"][0], p["wd"][0], tm)

    gb, gx, glu = _conv_in(x, p["mix_g"][1], p["conv_w"], tm)
    yc, yu = _conv_mix(gb, gx, glu, p["scw"], p["dww"], p["dwb"], p["lng"], p["lnb"], ts)
    return _mix_ffn(x, yc, yu, p["conv_wo_a"], p["conv_wo_b"],
                    p["ffn_g"][1], p["wg"][1], p["wu"][1], p["wd"][1], tm)


def _prepare(rel_bias, mix_norm, ffn_norm, w_gate, w_up, w_down, attn_w_in, attn_w_out, a_q_norm, a_k_norm,
             a_sink, b_q_norm, b_k_norm, b_lambda, b_subln, conv_w_in, conv_w_out, short_conv_w, conf_dw_w,
             conf_dw_b, conf_ln_g, conf_ln_b, tiles):
    attn_w, attn_gain, seg = _attn_in_params(attn_w_in[0], a_q_norm[0], a_k_norm[0], b_q_norm[0], b_k_norm[0])
    half = A_Q_HEADS * HEAD_DIM
    return {
        "mix_g": [mix_norm[l][None, :].astype(F32) for l in range(2)],
        "ffn_g": [ffn_norm[l][None, :].astype(F32) for l in range(2)],
        "wg": [w_gate[l].astype(BF16) for l in range(2)],
        "wu": [w_up[l].astype(BF16) for l in range(2)],
        "wd": [w_down[l].astype(BF16) for l in range(2)],
        "attn_w": attn_w, "attn_gain": attn_gain, "seg": seg,
        "attn_wo_a": _by_win_head(attn_w_out[0][:half], 0).astype(BF16),
        "attn_wo_b": attn_w_out[0][half:].astype(BF16),
        "win_bias": _win_bias(rel_bias, a_sink[0], a_q_norm[0], a_k_norm[0]),
        "diff_bias": {t: _diff_bias(rel_bias, b_q_norm[0], b_k_norm[0], t) for t in tiles},
        "lam": b_lambda[0].astype(F32),
        "subln": b_subln[0].astype(F32)[:, None],
        "lam_init": 0.8 - 0.6 * math.exp(-0.3 * 0),
        "conv_w": conv_w_in[0].astype(BF16),
        "conv_wo_a": conv_w_out[0][:C_WIDTH].astype(BF16), "conv_wo_b": conv_w_out[0][C_WIDTH:].astype(BF16),
        "scw": jnp.broadcast_to(short_conv_w[0].astype(F32)[:, None, :], (SHORT_CONV, SUBLANES, C_WIDTH)),
        "dww": jnp.broadcast_to(conf_dw_w[0].astype(F32)[:, None, :], (CONF_CONV, SUBLANES, D_WIDTH)),
        "dwb": conf_dw_b[0][None, :].astype(F32),
        "lng": conf_ln_g[0][None, :].astype(F32), "lnb": conf_ln_b[0][None, :].astype(F32),
    }


def _tiling(s_len):
    return dict(tm=min(1024, s_len), tile=min(512, s_len), ts=min(512, s_len), qb=8)


def kernel(x_prompt, x_sample, rel_bias, mix_norm, ffn_norm, w_gate, w_up, w_down, attn_w_in, attn_w_out,
           a_q_norm, a_k_norm, a_sink, b_q_norm, b_k_norm, b_lambda, b_subln, conv_w_in, conv_w_out,
           short_conv_w, conf_dw_w, conf_dw_b, conf_ln_g, conf_ln_b):
    tp, tsm = _tiling(x_prompt.shape[1]), _tiling(x_sample.shape[1])
    p = _prepare(rel_bias, mix_norm, ffn_norm, w_gate, w_up, w_down, attn_w_in, attn_w_out, a_q_norm,
                 a_k_norm, a_sink, b_q_norm, b_k_norm, b_lambda, b_subln, conv_w_in, conv_w_out,
                 short_conv_w, conf_dw_w, conf_dw_b, conf_ln_g, conf_ln_b, {tp["tile"], tsm["tile"]})
    return (_trunk(x_prompt, p, **tp), _trunk(x_sample, p, **tsm))
```

```python
import functools
import math

import jax
import jax.numpy as jnp
from jax import lax
from jax.experimental import pallas as pl
from jax.experimental.pallas import tpu as pltpu

D_MODEL = 1024
HEAD_DIM = 64
A_Q_HEADS = 8
A_KV_HEADS = 2
WINDOW = 128
BLOCK = 128
B_HEADS = 4
NUM_BUCKETS = 32
MAX_DISTANCE = 128
C_WIDTH = 512
D_WIDTH = 512
SHORT_CONV = 3
CONF_CONV = 31
FFN_HIDDEN = 2816
EPS = 1e-6
NEG = -1e30

LANES = 128
MXU_COLS = 256
VMEM_LIMIT_BYTES = 56 * 1024 * 1024

BF16 = jnp.bfloat16
F32 = jnp.float32

QA_W = A_Q_HEADS * HEAD_DIM
PROJ0_W = QA_W + 2 * A_KV_HEADS * HEAD_DIM + 3 * B_HEADS * 2 * HEAD_DIM
NORM_CHUNKS0 = ("all", "all", "low", "all", "all", "all", "all", None, None)
WIN_HEAD_ORDER = (0, 4, 1, 5, 2, 6, 3, 7)

HALO = 16


def _cparams(sem):
    return pltpu.CompilerParams(dimension_semantics=sem, vmem_limit_bytes=VMEM_LIMIT_BYTES)


def _const_spec(shape):
    nd = len(shape)
    return pl.BlockSpec(shape, lambda *_: (0,) * nd)


def _rms(x, g):
    ms = jnp.mean(x * x, axis=-1, keepdims=True)
    return x * lax.rsqrt(ms + EPS) * g


def _attn_in_kernel(x_ref, g_ref, w_ref, gain_ref, seg_ref, qa_ref, ka_ref, va_ref, qb_ref, kb_ref, vb_ref):
    half = MXU_COLS // 2

    def store(c, y):
        if c < 2:
            qa_ref[:, c * MXU_COLS:(c + 1) * MXU_COLS] = y
        elif c == 2:
            ka_ref[...] = y[:, :half]
            va_ref[...] = y[:, half:]
        else:
            ref, first = ((qb_ref, 3), (kb_ref, 5), (vb_ref, 7))[(c - 3) // 2]
            ref[0, 2 * (c - first)] = y[:, :half]
            ref[0, 2 * (c - first) + 1] = y[:, half:]

    h = _rms(x_ref[...], g_ref[...]).astype(BF16)
    seg = seg_ref[...]
    low = lax.broadcasted_iota(jnp.int32, (1, MXU_COLS), 1) < MXU_COLS // 2
    chunk = lambda c: slice(c * MXU_COLS, (c + 1) * MXU_COLS)
    accs = [jnp.dot(h, w_ref[:, chunk(c)], preferred_element_type=F32) for c in range(len(NORM_CHUNKS0))]
    for c, normed in enumerate(NORM_CHUNKS0):
        acc = accs[c]
        if normed is not None:
            ms = jnp.dot((acc * acc).astype(BF16), seg, preferred_element_type=F32)
            scale = lax.rsqrt(ms + EPS) * gain_ref[:, chunk(c)]
            acc = acc * (scale if normed == "all" else jnp.where(low, scale, 1.0))
        store(c, acc.astype(BF16))


def _attn_in(x, g, w, gain, seg, tm):
    bsz, s_len, _ = x.shape
    spt = s_len // tm
    flat = lambda width: pl.BlockSpec((1, tm, width), lambda i: (i // spt, i % spt, 0))
    heads = pl.BlockSpec((1, B_HEADS, tm, LANES), lambda i: (i // spt, 0, i % spt, 0))
    sds = lambda *shape: jax.ShapeDtypeStruct(shape, BF16)
    head_major = sds(bsz, B_HEADS, s_len, LANES)

    def body(x_ref, g_ref, w_ref, gain_ref, seg_ref, qa_ref, ka_ref, va_ref, qb_ref, kb_ref, vb_ref):
        _attn_in_kernel(x_ref.at[0], g_ref, w_ref, gain_ref, seg_ref,
                        qa_ref.at[0], ka_ref.at[0], va_ref.at[0], qb_ref, kb_ref, vb_ref)

    return pl.pallas_call(
        body,
        grid=(bsz * spt,),
        in_specs=[
            flat(D_MODEL),
            _const_spec((1, D_MODEL)),
            _const_spec((D_MODEL, PROJ0_W)),
            _const_spec((1, PROJ0_W)),
            _const_spec((MXU_COLS, MXU_COLS)),
        ],
        out_specs=[flat(QA_W), flat(LANES), flat(LANES), heads, heads, heads],
        out_shape=[sds(bsz, s_len, QA_W), sds(bsz, s_len, LANES), sds(bsz, s_len, LANES),
                   head_major, head_major, head_major],
        compiler_params=_cparams(("parallel",)),
        name="attn_in",
    )(x, g, w, gain, seg)


WIN_COLS = A_Q_HEADS * BLOCK


def _win_attn_kernel(q_ref, kp_ref, kc_ref, kn_ref, vp_ref, vc_ref, vn_ref, bias_ref, sink_ref, o_ref, *,
                     qb, bounded):
    step = pl.program_id(1)
    nb = pl.num_programs(1) * qb
    low = lax.broadcasted_iota(jnp.int32, (1, LANES), 1) < HEAD_DIM
    top = lax.broadcasted_iota(jnp.int32, (LANES, 1), 0) < HEAD_DIM
    sink = sink_ref[...]

    def piece(prev_ref, cur_ref, next_ref, i):
        if i == 0:
            return prev_ref[0]
        if i == qb + 1:
            return next_ref[0]
        return cur_ref[0, (i - 1) * BLOCK:i * BLOCK, :]

    def scores(b):
        kcat = jnp.concatenate([piece(kp_ref, kc_ref, kn_ref, b + i) for i in range(3)], axis=0)
        q = q_ref[0, b * BLOCK:(b + 1) * BLOCK, :]
        zero = jnp.zeros((BLOCK, LANES), BF16)
        halves = []
        for j in range(A_Q_HEADS // 2):
            slab = q[:, j * LANES:(j + 1) * LANES]
            halves += [jnp.where(low, slab, zero), jnp.where(low, zero, slab)]
        qcat = jnp.concatenate(halves, axis=0)
        return lax.dot_general(kcat, qcat, _NT, preferred_element_type=F32)

    def weights(b, s_t):
        n = step * qb + b
        edge = jnp.where(n == 0, 0, jnp.where(n == nb - 1, 2, 1))
        s_t = s_t + bias_ref[edge]
        if bounded:
            p = jnp.exp2(s_t)
            den = jnp.sum(p, axis=0, keepdims=True) + jnp.exp2(sink)
        else:
            m = jnp.maximum(jnp.max(s_t, axis=0, keepdims=True), sink)
            p = jnp.exp2(s_t - m)
            den = jnp.sum(p, axis=0, keepdims=True) + jnp.exp2(sink - m)
        return p.astype(BF16), den

    def values(b, p, den):
        vcat = jnp.concatenate([piece(vp_ref, vc_ref, vn_ref, b + i) for i in range(3)], axis=0)
        return lax.dot_general(vcat, p, _TN, preferred_element_type=F32) / den

    s_all = [scores(b) for b in range(qb)]
    w_all = [weights(b, s_all[b]) for b in range(qb)]
    o_all = [values(b, *w_all[b]) for b in range(qb)]
    for b in range(qb):
        for j in range(A_Q_HEADS // 2):
            c0 = 2 * j * BLOCK
            slab_t = jnp.where(top, o_all[b][:, c0:c0 + BLOCK], o_all[b][:, c0 + BLOCK:c0 + 2 * BLOCK])
            o_ref[0, b * BLOCK:(b + 1) * BLOCK, j * LANES:(j + 1) * LANES] = slab_t.T.astype(BF16)


def _win_attn(qa, ka, va, wb, qb):
    fast = lambda: _win_attn_call(qa, ka, va, wb["sink_shifted"], wb["bias_shifted"], qb, True)
    exact_max = lambda: _win_attn_call(qa, ka, va, wb["sink"], wb["bias"], qb, False)
    return lax.cond(wb["bounded_ok"], fast, exact_max)


def _win_attn_call(qa, ka, va, sink_row, bias, qb, bounded):
    bsz, s_len, _ = qa.shape
    nb = s_len // BLOCK
    assert nb % qb == 0 and nb >= 2
    kv_specs = [
        pl.BlockSpec((1, BLOCK, LANES), lambda b, t: (b, jnp.maximum(t * qb - 1, 0), 0)),
        pl.BlockSpec((1, qb * BLOCK, LANES), lambda b, t: (b, t, 0)),
        pl.BlockSpec((1, BLOCK, LANES), lambda b, t: (b, jnp.minimum((t + 1) * qb, nb - 1), 0)),
    ]
    return pl.pallas_call(
        functools.partial(_win_attn_kernel, qb=qb, bounded=bounded),
        grid=(bsz, nb // qb),
        in_specs=[pl.BlockSpec((1, qb * BLOCK, QA_W), lambda b, t: (b, t, 0))] + kv_specs + kv_specs
                 + [pl.BlockSpec(memory_space=pltpu.VMEM), pl.BlockSpec(memory_space=pltpu.VMEM)],
        out_specs=pl.BlockSpec((1, qb * BLOCK, QA_W), lambda b, t: (b, t, 0)),
        out_shape=jax.ShapeDtypeStruct((bsz, s_len, QA_W), BF16),
        compiler_params=_cparams(("parallel", "parallel")),
        name="win_attn",
    )(qa, ka, ka, ka, va, va, va, bias, sink_row)


_NT = (((1,), (1,)), ((), ()))
_TN = (((0,), (0,)), ((), ()))


def _split_maps(q, qz_ref):
    lane = lax.broadcasted_iota(jnp.int32, (1, LANES), 1)
    zero = jnp.zeros_like(q)
    qz_ref[0] = jnp.where(lane < HEAD_DIM, q, zero)
    qz_ref[1] = jnp.where(lane < HEAD_DIM, zero, q)


def _diff_out(acc0, l0, acc1, l1, lam_ref, g_ref, lam_init):
    lam = lam_ref[...]
    lam_full = (jnp.exp(jnp.sum(lam[0:1] * lam[1:2], axis=-1, keepdims=True))
                - jnp.exp(jnp.sum(lam[2:3] * lam[3:4], axis=-1, keepdims=True)) + lam_init)
    o = acc0 / l0 - lam_full * (acc1 / l1)
    ms = jnp.mean(o * o, axis=0, keepdims=True)
    y = o * lax.rsqrt(ms + EPS) * g_ref[...] * (1.0 - lam_init)
    return y.T.astype(BF16)


def _diff_attn_bounded_kernel(q_ref, k_ref, v_ref, bias_ref, lam_ref, g_ref, o_ref,
                              qz_ref, s_ref, l_ref, acc_ref, *, lam_init, tile, qtiles, ktiles):
    qstep = pl.program_id(2)
    ktile = ktiles * tile
    nk = k_ref.shape[1] // ktile
    first_map = lax.broadcasted_iota(jnp.int32, (LANES, 1), 0) < HEAD_DIM
    for qt in range(qtiles):
        q_t = q_ref[0, qt * tile:(qt + 1) * tile, :].astype(F32).T
        qz_ref[qt, 0] = jnp.where(first_map, q_t, 0.0).astype(BF16)
        qz_ref[qt, 1] = jnp.where(first_map, 0.0, q_t).astype(BF16)
    l_ref[...] = jnp.zeros(l_ref.shape, F32)
    acc_ref[...] = jnp.zeros(acc_ref.shape, F32)

    def produce(qt, ki, buf):
        k = k_ref[0, ki * ktile:(ki + 1) * ktile, :]
        for c in range(2):
            s_ref[buf, c] = jnp.dot(k, qz_ref[qt, c], preferred_element_type=F32)

    def consume(qt, ki, buf):
        v = v_ref[0, ki * ktile:(ki + 1) * ktile, :]
        qi = qstep * qtiles + qt
        for c in range(2):
            parts = []
            for sub in range(ktiles):
                bias = bias_ref[0, jnp.clip(ki * ktiles + sub - qi + 2, 0, 4)]
                p = jnp.exp2(s_ref[buf, c, sub * tile:(sub + 1) * tile, :] + bias)
                l_ref[qt, c] += jnp.sum(p.reshape(tile // 8, 8, tile), axis=0)
                parts.append(p.astype(BF16))
            p_all = parts[0] if len(parts) == 1 else jnp.concatenate(parts, axis=0)
            acc_ref[qt, c] += lax.dot_general(v, p_all, _TN, preferred_element_type=F32)

    def finish(qt):
        l0 = jnp.sum(l_ref[qt, 0], axis=0, keepdims=True)
        l1 = jnp.sum(l_ref[qt, 1], axis=0, keepdims=True)
        o_ref[0, qt * tile:(qt + 1) * tile, :] = _diff_out(acc_ref[qt, 0], l0, acc_ref[qt, 1], l1,
                                                           lam_ref, g_ref, lam_init)

    steps =[(qt, ki) for qt in range(qtiles) for ki in range(nk)]
    produce(*steps[0], 0)
    for i, (qt, ki) in enumerate(steps):
        if i + 1 < len(steps):
            produce(*steps[i + 1], (i + 1) % 2)
        consume(qt, ki, i % 2)
        if ki == nk - 1:
            finish(qt)


DIFF_TILES = 32
DIFF_KEY_TILES = 4


def _diff_attn_bounded(qb, kb, vb, bias_t, lam, g_col, lam_init, tile):
    bsz, _, s_len, _ = qb.shape
    tiles = s_len // tile
    assert tiles <= DIFF_TILES
    qtiles = min(DIFF_TILES // tiles, tiles)
    ktiles = max(1, min(DIFF_KEY_TILES, tiles // 2))
    assert tiles % qtiles == 0 and tiles % ktiles == 0
    kernel = functools.partial(_diff_attn_bounded_kernel, lam_init=lam_init, tile=tile, qtiles=qtiles,
                               ktiles=ktiles)

    def body(q_ref, k_ref, v_ref, bias_ref, lam_ref, g_ref, o_ref, *scratch):
        kernel(q_ref.at[0], k_ref.at[0], v_ref.at[0], bias_ref, lam_ref, g_ref, o_ref.at[0], *scratch)

    q_rows = qtiles * tile
    return pl.pallas_call(
        body,
        grid=(bsz, B_HEADS, s_len // q_rows),
        in_specs=[
            pl.BlockSpec((1, 1, q_rows, LANES), lambda b, h, qi: (b, h, qi, 0)),
            pl.BlockSpec((1, 1, s_len, LANES), lambda b, h, qi: (b, h, 0, 0)),
            pl.BlockSpec((1, 1, s_len, LANES), lambda b, h, qi: (b, h, 0, 0)),
            pl.BlockSpec((1, 5, tile, tile), lambda b, h, qi: (h, 0, 0, 0)),
            _const_spec((4, HEAD_DIM)),
            _const_spec((2 * HEAD_DIM, 1)),
        ],
        out_specs=pl.BlockSpec((1, 1, q_rows, LANES), lambda b, h, qi: (b, h, qi, 0)),
        out_shape=jax.ShapeDtypeStruct((bsz, B_HEADS, s_len, LANES), BF16),
        scratch_shapes=[
            pltpu.VMEM((qtiles, 2, LANES, tile), BF16),
            pltpu.VMEM((2, 2, ktiles * tile, tile), F32),
            pltpu.VMEM((qtiles, 2, 8, tile), F32),
            pltpu.VMEM((qtiles, 2, 2 * HEAD_DIM, tile), F32),
        ],
        compiler_params=_cparams(("parallel", "parallel", "parallel")),
        name="diff_attn",
    )(qb, kb, vb, bias_t, lam, g_col)


def _diff_attn_online_kernel(far_ref, q_ref, k_ref, v_ref, bias_ref, lam_ref, g_ref, o_ref,
                             qz_ref, m_ref, l_ref, acc_ref, *, lam_init):
    h = pl.program_id(1)
    qi = pl.program_id(2)
    ki = pl.program_id(3)
    nk = pl.num_programs(3)

    @pl.when(ki == 0)
    def _init():
        _split_maps(q_ref[0], qz_ref)
        m_ref[...] = jnp.full(m_ref.shape, NEG, F32)
        l_ref[...] = jnp.zeros(l_ref.shape, F32)
        acc_ref[...] = jnp.zeros(acc_ref.shape, F32)

    delta = ki - qi

    def step(c, bias_tile, bias_row):
        s_t = lax.dot_general(k_ref[0], qz_ref[c], _NT, preferred_element_type=F32)
        if bias_tile is not None:
            s_t = s_t + bias_tile
        m_old = m_ref[c]
        m_new = jnp.maximum(m_old, jnp.max(s_t, axis=0, keepdims=True) + bias_row)
        alpha = jnp.exp2(m_old - m_new)
        p = jnp.exp2(s_t - (m_new - bias_row))
        l_ref[c] = alpha * l_ref[c] + jnp.sum(p, axis=0, keepdims=True)
        pv = lax.dot_general(v_ref[0], p.astype(BF16), _TN, preferred_element_type=F32)
        acc_ref[c] = alpha * acc_ref[c] + pv
        m_ref[c] = m_new

    @pl.when(jnp.abs(delta) <= 1)
    def _near():
        bias_tile = bias_ref[0, delta + 1]
        for c in range(2):
            step(c, bias_tile, 0.0)

    @pl.when(jnp.abs(delta) > 1)
    def _far():
        side = jnp.where(delta < 0, far_ref[h, 0], far_ref[h, 1])
        for c in range(2):
            step(c, None, side)

    @pl.when(ki == nk - 1)
    def _finish():
        o_ref[0] = _diff_out(acc_ref[0], l_ref[0], acc_ref[1], l_ref[1], lam_ref, g_ref, lam_init)


def _diff_attn_online(qb, kb, vb, far, bias_t, lam, g_col, lam_init, tile):
    bsz, _, s_len, _ = qb.shape
    nt = s_len // tile
    kernel = functools.partial(_diff_attn_online_kernel, lam_init=lam_init)

    def body(far_ref, q_ref, k_ref, v_ref, bias_ref, lam_ref, g_ref, o_ref, *scratch):
        kernel(far_ref, q_ref.at[0], k_ref.at[0], v_ref.at[0], bias_ref, lam_ref, g_ref, o_ref.at[0], *scratch)

    return pl.pallas_call(
        body,
        grid=(bsz, B_HEADS, nt, nt),
        in_specs=[
            pl.BlockSpec(memory_space=pltpu.SMEM),
            pl.BlockSpec((1, 1, tile, LANES), lambda b, h, qi, ki: (b, h, qi, 0)),
            pl.BlockSpec((1, 1, tile, LANES), lambda b, h, qi, ki: (b, h, ki, 0)),
            pl.BlockSpec((1, 1, tile, LANES), lambda b, h, qi, ki: (b, h, ki, 0)),
            pl.BlockSpec((1, 3, tile, tile), lambda b, h, qi, ki: (h, 0, 0, 0)),
            _const_spec((4, HEAD_DIM)),
            _const_spec((2 * HEAD_DIM, 1)),
        ],
        out_specs=pl.BlockSpec((1, 1, tile, LANES), lambda b, h, qi, ki: (b, h, qi, 0)),
        out_shape=jax.ShapeDtypeStruct((bsz, B_HEADS, s_len, LANES), BF16),
        scratch_shapes=[
            pltpu.VMEM((2, tile, LANES), BF16),
            pltpu.VMEM((2, 1, tile), F32),
            pltpu.VMEM((2, 1, tile), F32),
            pltpu.VMEM((2, 2 * HEAD_DIM, tile), F32),
        ],
        compiler_params=_cparams(("parallel", "parallel", "parallel", "arbitrary")),
        name="diff_attn_online",
    )(far, qb, kb, vb, bias_t, lam, g_col)


def _diff_attn(qb, kb, vb, db, lam, g_col, lam_init, tile):
    bounded = lambda: _diff_attn_bounded(qb, kb, vb, db["shifted"], lam, g_col, lam_init, tile)
    online = lambda: _diff_attn_online(qb, kb, vb, db["far"], db["near"], lam, g_col, lam_init, tile)
    return lax.cond(db["bounded_ok"], bounded, online)


FFN_CHUNKS = ((0, 1536), (1536, FFN_HIDDEN))


def _mix_ffn_kernel(x_ref, a_ref, b_ref, wa_ref, wb_ref, g_ref, wg_ref, wu_ref, wd_ref, o_ref):
    if len(b_ref.shape) == 4:
        b = jnp.concatenate([b_ref[0, hd] for hd in range(b_ref.shape[1])], axis=-1)
    else:
        b = b_ref[0]
    mix = (jnp.dot(a_ref[0], wa_ref[...], preferred_element_type=F32)
           + jnp.dot(b, wb_ref[...], preferred_element_type=F32))
    x1 = x_ref[0] + mix
    h = _rms(x1, g_ref[...]).astype(BF16)
    gates = [jnp.dot(h, wg_ref[:, lo:hi], preferred_element_type=F32) for lo, hi in FFN_CHUNKS]
    ups = [jnp.dot(h, wu_ref[:, lo:hi], preferred_element_type=F32) for lo, hi in FFN_CHUNKS]
    down = None
    for (lo, hi), gate, up in zip(FFN_CHUNKS, gates, ups):
        act = (gate * jax.nn.sigmoid(gate) * up).astype(BF16)
        part = jnp.dot(act, wd_ref[lo:hi, :], preferred_element_type=F32)
        down = part if down is None else down + part
    o_ref[0] = x1 + down


def _mix_ffn(x, a, b, wa, wb, g, wg, wu, wd, tm):
    bsz, s_len, _ = x.shape
    spt = s_len // tm
    row_spec = lambda w: pl.BlockSpec((1, tm, w), lambda i: (i // spt, i % spt, 0))
    if b.ndim == 4:
        b_spec = pl.BlockSpec((1, b.shape[1], tm, LANES), lambda i: (i // spt, 0, i % spt, 0))
    else:
        b_spec = row_spec(b.shape[-1])
    resident = pl.BlockSpec(memory_space=pltpu.VMEM)
    return pl.pallas_call(
        _mix_ffn_kernel,
        grid=(bsz * spt,),
        in_specs=[row_spec(D_MODEL), row_spec(a.shape[-1]), b_spec] + [resident] * 6,
        out_specs=row_spec(D_MODEL),
        out_shape=jax.ShapeDtypeStruct(x.shape, F32),
        compiler_params=_cparams(("parallel",)),
        name="mix_ffn",
    )(x, a, b, wa, wb, g, wg, wu, wd)


def _conv_in_kernel(x_ref, g_ref, w_ref, gb_ref, gx_ref, glu_ref):
    h = _rms(x_ref[0], g_ref[...]).astype(BF16)
    c = C_WIDTH
    part = lambda i: jnp.dot(h, w_ref[:, i * c:(i + 1) * c], preferred_element_type=F32)
    gate, a = part(4), part(3)
    glu_ref[0] = (a * jax.nn.sigmoid(gate)).astype(BF16)
    gx_ref[0] = (part(1) * part(2)).astype(BF16)
    gb_ref[0] = part(0).astype(BF16)


def _conv_in(x, g, w, tm):
    bsz, s_len, _ = x.shape
    spt = s_len // tm
    row_spec = lambda width: pl.BlockSpec((1, tm, width), lambda i: (i // spt, i % spt, 0))
    out = jax.ShapeDtypeStruct((bsz, s_len, C_WIDTH), BF16)
    return pl.pallas_call(
        _conv_in_kernel,
        grid=(bsz * spt,),
        in_specs=[row_spec(D_MODEL), _const_spec((1, D_MODEL)), _const_spec((D_MODEL, w.shape[1]))],
        out_specs=[row_spec(C_WIDTH)] * 3,
        out_shape=[out] * 3,
        compiler_params=_cparams(("parallel",)),
        name="conv_in",
    )(x, g, w)


CONV_CHUNK = 32
SUBLANES = 8


SHIFT_ROWS = 128
SHIFT_K = 256


def _conv_mix_kernel(gb_ref, gx_ref, gxp_ref, gxn_ref, u_ref, up_ref, un_ref, shift_ref,
                     scw_ref, dww_ref, dwb_ref, lng_ref, lnb_ref, yc_ref, yu_ref,
                     xb_ref, xs_ref, u_ref_f32, *, ts):
    t = pl.program_id(1)
    nt = pl.num_programs(1)
    padded = ts + 2 * HALO

    def taps(width):
        return [(j,) + divmod(HALO - width // 2 + j, SUBLANES)[::-1] for j in range(width)]

    def fill(cur, prev, nxt, width, xb, xs):
        zero = jnp.zeros((HALO, C_WIDTH), BF16)
        xb[0:HALO, :] = jnp.where(t > 0, prev[0], zero)
        xb[HALO:HALO + ts, :] = cur[0]
        xb[HALO + ts:padded, :] = jnp.where(t < nt - 1, nxt[0], zero)
        xb[padded:, :] = jnp.zeros((xb.shape[0] - padded, C_WIDTH), BF16)
        xs[0] = xb[0:padded, :].astype(F32)
        copies = sorted({r for _, r, _ in taps(width)} - {0})
        for o0 in range(0, ts, SHIFT_ROWS):
            for r in copies:
                xs[r, o0:o0 + SHIFT_ROWS, :] = jnp.dot(shift_ref[r - 1], xb[o0:o0 + SHIFT_K, :],
                                                       preferred_element_type=F32)
        for r in copies:
            xs[r, ts:padded, :] = jnp.dot(shift_ref[r - 1, 0:2 * HALO, 0:LANES], xb[ts:ts + LANES, :],
                                          preferred_element_type=F32)

    def conv(w_ref, width, s0, xs):
        acc = None
        groups = CONV_CHUNK // SUBLANES
        for r in sorted({r for _, r, _ in taps(width)}):
            mine = [(j, q) for j, r_j, q in taps(width) if r_j == r]
            q_lo, q_hi = min(q for _, q in mine), max(q for _, q in mine)
            span = xs[r, pl.ds(s0 + SUBLANES * q_lo, CONV_CHUNK + SUBLANES * (q_hi - q_lo)), :]
            span = span.reshape(groups + q_hi - q_lo, SUBLANES, -1)
            for j, q in mine:
                term = span[q - q_lo:q - q_lo + groups] * w_ref[j]
                acc = term if acc is None else acc + term
        return acc.reshape(CONV_CHUNK, -1)

    def chunks(body):
        for c in range(ts // CONV_CHUNK):
            body(c * CONV_CHUNK)

    fill(gx_ref, gxp_ref, gxn_ref, SHORT_CONV, xb_ref, xs_ref)

    def short(s0):
        out = pl.ds(s0, CONV_CHUNK)
        yc_ref[0, out, :] = (gb_ref[0, out, :].astype(F32) * conv(scw_ref, SHORT_CONV, s0, xs_ref)).astype(BF16)
    chunks(short)

    fill(u_ref, up_ref, un_ref, CONF_CONV, xb_ref, xs_ref)

    def conf(s0):
        u_ref_f32[pl.ds(s0, CONV_CHUNK), :] = conv(dww_ref, CONF_CONV, s0, xs_ref) + dwb_ref[...]
    chunks(conf)

    u = u_ref_f32[...]
    mu = jnp.mean(u, axis=-1, keepdims=True)
    uc = u - mu
    var = jnp.mean(uc * uc, axis=-1, keepdims=True)
    y = uc * lax.rsqrt(var + EPS) * lng_ref[...] + lnb_ref[...]
    yu_ref[0] = (y * jax.nn.sigmoid(y)).astype(BF16)


def _conv_mix(gb, gx, glu, scw, dww, dwb, lng, lnb, ts):
    bsz, s_len, _ = gb.shape
    nt = s_len // ts
    r = ts // HALO
    nh = s_len // HALO
    assert C_WIDTH == D_WIDTH
    cur = pl.BlockSpec((1, ts, C_WIDTH), lambda b, t: (b, t, 0))
    prev = pl.BlockSpec((1, HALO, C_WIDTH), lambda b, t: (b, jnp.maximum(t * r - 1, 0), 0))
    nxt = pl.BlockSpec((1, HALO, C_WIDTH), lambda b, t: (b, jnp.minimum((t + 1) * r, nh - 1), 0))
    out_spec = pl.BlockSpec((1, ts, C_WIDTH), lambda b, t: (b, t, 0))
    assert ts % SHIFT_ROWS == 0
    i, j = jnp.arange(SHIFT_ROWS)[:, None], jnp.arange(SHIFT_K)[None, :]
    shift = jnp.stack([(j == i + r) for r in range(1, SUBLANES)]).astype(BF16)
    return pl.pallas_call(
        functools.partial(_conv_mix_kernel, ts=ts),
        grid=(bsz, nt),
        in_specs=[cur, cur, prev, nxt, cur, prev, nxt, _const_spec(shift.shape),
                  _const_spec((SHORT_CONV, SUBLANES, C_WIDTH)), _const_spec((CONF_CONV, SUBLANES, D_WIDTH)),
                  _const_spec((1, D_WIDTH)), _const_spec((1, D_WIDTH)), _const_spec((1, D_WIDTH))],
        out_specs=[out_spec, out_spec],
        out_shape=[jax.ShapeDtypeStruct((bsz, s_len, C_WIDTH), BF16),
                   jax.ShapeDtypeStruct((bsz, s_len, D_WIDTH), BF16)],
        scratch_shapes=[pltpu.VMEM((ts + LANES, C_WIDTH), BF16),
                        pltpu.VMEM((SUBLANES, ts + 2 * HALO, C_WIDTH), F32), pltpu.VMEM((ts, D_WIDTH), F32)],
        compiler_params=_cparams(("parallel", "parallel")),
        name="conv_mix",
    )(gb, gx, gx, gx, glu, glu, glu, shift, scw, dww, dwb, lng, lnb)


def _rel_bucket(rel):
    half = NUM_BUCKETS // 2
    max_exact = half // 2
    n = jnp.abs(rel)
    large = max_exact + (jnp.log(jnp.maximum(n, 1).astype(F32) / max_exact)
                         / math.log(MAX_DISTANCE / max_exact) * (half - max_exact)).astype(jnp.int32)
    large = jnp.minimum(large, half - 1)
    return jnp.where(rel > 0, half, 0) + jnp.where(n < max_exact, n, large)


def _toeplitz(u, rows, cols):
    length = u.shape[-1]
    flat = jnp.tile(u, (1,) * (u.ndim - 1) + (rows,))[..., :rows * (length - 1)]
    return flat.reshape(u.shape[:-1] + (rows, length - 1))[..., :cols]


LOG2E = math.log2(math.e)


def _win_bias(rel_bias, a_sink, a_qn, a_kn):
    order = jnp.array(WIN_HEAD_ORDER)
    table = rel_bias[:, :A_Q_HEADS].astype(F32) * LOG2E
    length = 4 * BLOCK
    n = jnp.arange(length)
    rel = jnp.where(n < BLOCK, -n, length - n) - BLOCK
    vec = jnp.where((jnp.abs(rel) <= WINDOW)[None], table[_rel_bucket(rel)].T, NEG)
    base = _toeplitz(vec, 3 * BLOCK, BLOCK)[order]
    mid = base.transpose(1, 0, 2).reshape(3 * BLOCK, A_Q_HEADS * BLOCK)
    key_block = (jnp.arange(3 * BLOCK) // BLOCK)[:, None]
    bias = jnp.stack([jnp.where(key_block == 0, NEG, mid), mid, jnp.where(key_block == 2, NEG, mid)], axis=0)
    sink = a_sink.astype(F32) * LOG2E
    s_max = 1.02 * LOG2E * math.sqrt(HEAD_DIM) * jnp.max(jnp.abs(a_qn * a_kn))
    bound = s_max + jnp.maximum(jnp.max(table, axis=0), sink)
    lowest = jnp.minimum(-s_max + jnp.min(table, axis=0), sink) - bound
    columns = lambda per_head: jnp.repeat(per_head[order], BLOCK)[None, :]
    return {"bias": bias, "sink": columns(sink),
            "bias_shifted": bias - columns(bound)[None], "sink_shifted": columns(sink - bound),
            "bounded_ok": jnp.all(lowest > MIN_EXP2_ARG)}
MIN_EXP2_ARG = -120.0


def _diff_bias(rel_bias, b_qn, b_kn, tile):
    table = rel_bias[:, A_Q_HEADS:].astype(F32) * LOG2E
    length = 2 * tile
    n = jnp.arange(length)
    k_minus_q = jnp.where(n < tile, -n, length - n)
    rel = jnp.clip(k_minus_q[None, :] + jnp.array([-tile, 0, tile])[:, None], 1 - length, length - 1)
    near = _toeplitz(table[_rel_bucket(rel)].transpose(2, 0, 1), tile, tile)
    far = table[_rel_bucket(jnp.array([-length, length]))].T
    s_max = 1.02 * LOG2E * math.sqrt(HEAD_DIM) * jnp.max(jnp.abs(b_qn * b_kn))
    b_max, b_min = jnp.max(table, axis=0), jnp.min(table, axis=0)
    bound = s_max + b_max
    const = lambda col: jnp.broadcast_to((far[:, col] - bound)[:, None, None, None], (B_HEADS, 1, tile, tile))
    shifted = jnp.concatenate([const(0), near - bound[:, None, None, None], const(1)], axis=1)
    bounded_ok = jnp.all(-2.0 * s_max - (b_max - b_min) > MIN_EXP2_ARG)
    return {"near": near, "far": far, "shifted": shifted, "bounded_ok": bounded_ok}


def _by_win_head(m, axis):
    take = lambda h: lax.slice_in_dim(m, h * HEAD_DIM, (h + 1) * HEAD_DIM, axis=axis)
    return jnp.concatenate([take(h) for h in WIN_HEAD_ORDER], axis=axis)


def _attn_in_params(w_in, a_qn, a_kn, b_qn, b_kn):
    d = HEAD_DIM
    k0 = A_Q_HEADS * d
    w = jnp.concatenate([_by_win_head(w_in[:, :k0], 1), w_in[:, k0:]], axis=1).astype(BF16)
    scale = HEAD_DIM ** -0.5 * LOG2E
    ones = lambda n: jnp.ones((n,), F32)
    gain = jnp.concatenate([
        jnp.tile(a_qn, A_Q_HEADS) * scale, jnp.tile(a_kn, A_KV_HEADS), ones(A_KV_HEADS * d),
        jnp.tile(b_qn, 2 * B_HEADS) * scale, jnp.tile(b_kn, 2 * B_HEADS), ones(2 * B_HEADS * d),
    ]).astype(F32)[None, :]
    seg_id = jnp.arange(MXU_COLS) // d
    seg = jnp.where(seg_id[:, None] == seg_id[None, :], 1.0 / d, 0.0).astype(BF16)
    return w, gain, seg


def _trunk(x, p, *, tm, tile, ts, qb):
    qa, ka, va, qb_, kb, vb = _attn_in(x, p["mix_g"][0], p["attn_w"], p["attn_gain"], p["seg"], tm)
    ya = _win_attn(qa, ka, va, p["win_bias"], qb)
    yb = _diff_attn(qb_, kb, vb, p["diff_bias"][tile], p["lam"], p["subln"], p["lam_init"], tile)
    x = _mix_ffn(x, ya, yb, p["attn_wo_a"], p["attn_wo_b"],
                 p["ffn_g"][0], p["wg"][0], p["wu"][0], p["wd"][0], tm)

    gb, gx, glu = _conv_in(x, p["mix_g"][1], p["conv_w"], tm)
    yc, yu = _conv_mix(gb, gx, glu, p["scw"], p["dww"], p["dwb"], p["lng"], p["lnb"], ts)
    return _mix_ffn(x, yc, yu, p["conv_wo_a"], p["conv_wo_b"],
                    p["ffn_g"][1], p["wg"][1], p["wu"][1], p["wd"][1], tm)


def _prepare(rel_bias, mix_norm, ffn_norm, w_gate, w_up, w_down, attn_w_in, attn_w_out, a_q_norm, a_k_norm,
             a_sink, b_q_norm, b_k_norm, b_lambda, b_subln, conv_w_in, conv_w_out, short_conv_w, conf_dw_w,
             conf_dw_b, conf_ln_g, conf_ln_b, tiles):
    attn_w, attn_gain, seg = _attn_in_params(attn_w_in[0], a_q_norm[0], a_k_norm[0], b_q_norm[0], b_k_norm[0])
    half = A_Q_HEADS * HEAD_DIM
    return {
        "mix_g": [mix_norm[l][None, :].astype(F32) for l in range(2)],
        "ffn_g": [ffn_norm[l][None, :].astype(F32) for l in range(2)],
        "wg": [w_gate[l].astype(BF16) for l in range(2)],
        "wu": [w_up[l].astype(BF16) for l in range(2)],
        "wd": [w_down[l].astype(BF16) for l in range(2)],
        "attn_w": attn_w, "attn_gain": attn_gain, "seg": seg,
        "attn_wo_a": _by_win_head(attn_w_out[0][:half], 0).astype(BF16),
        "attn_wo_b": attn_w_out[0][half:].astype(BF16),
        "win_bias": _win_bias(rel_bias, a_sink[0], a_q_norm[0], a_k_norm[0]),
        "diff_bias": {t: _diff_bias(rel_bias, b_q_norm[0], b_k_norm[0], t) for t in tiles},
        "lam": b_lambda[0].astype(F32),
        "subln": b_subln[0].astype(F32)[:, None],
        "lam_init": 0.8 - 0.6 * math.exp(-0.3 * 0),
        "conv_w": conv_w_in[0].astype(BF16),
        "conv_wo_a": conv_w_out[0][:C_WIDTH].astype(BF16), "conv_wo_b": conv_w_out[0][C_WIDTH:].astype(BF16),
        "scw": jnp.broadcast_to(short_conv_w[0].astype(F32)[:, None, :], (SHORT_CONV, SUBLANES, C_WIDTH)),
        "dww": jnp.broadcast_to(conf_dw_w[0].astype(F32)[:, None, :], (CONF_CONV, SUBLANES, D_WIDTH)),
        "dwb": conf_dw_b[0][None, :].astype(F32),
        "lng": conf_ln_g[0][None, :].astype(F32), "lnb": conf_ln_b[0][None, :].astype(F32),
    }


def _tiling(s_len):
    return dict(tm=min(1024, s_len), tile=min(512, s_len), ts=min(1024, s_len), qb=8)


def kernel(x_prompt, x_sample, rel_bias, mix_norm, ffn_norm, w_gate, w_up, w_down, attn_w_in, attn_w_out,
           a_q_norm, a_k_norm, a_sink, b_q_norm, b_k_norm, b_lambda, b_subln, conv_w_in, conv_w_out,
           short_conv_w, conf_dw_w, conf_dw_b, conf_ln_g, conf_ln_b):
    tp, tsm = _tiling(x_prompt.shape[1]), _tiling(x_sample.shape[1])
    p = _prepare(rel_bias, mix_norm, ffn_norm, w_gate, w_up, w_down, attn_w_in, attn_w_out, a_q_norm,
                 a_k_norm, a_sink, b_q_norm, b_k_norm, b_lambda, b_subln, conv_w_in, conv_w_out,
                 short_conv_w, conf_dw_w, conf_dw_b, conf_ln_g, conf_ln_b, {tp["tile"], tsm["tile"]})
    return (_trunk(x_prompt, p, **tp), _trunk(x_sample, p, **tsm))
```
